```python
import jax
import jax.numpy as jnp
from jax import lax
import numpy as np

D_MODEL = 1024
BATCH = 8
SEQ = 4096
DEPTH = 2

HEAD_DIM = 64
ROPE_DIM = HEAD_DIM // 4
ROPE_THETA = 500000.0
Q_BLOCK = 128
NORM_EPS = 1e-6
N_EVEN = (DEPTH + 1) // 2
N_ODD = DEPTH // 2

DSA_HEADS = D_MODEL // HEAD_DIM
DSA_NOPE = HEAD_DIM - ROPE_DIM
DSA_V_DIM = HEAD_DIM
DSA_Q_LORA = D_MODEL // 4
DSA_KV_LORA = D_MODEL // 8
IDX_HEADS = 8
IDX_DIM = 64
DSA_TOPK = 256
DSA_IN = DSA_Q_LORA + DSA_KV_LORA + ROPE_DIM + IDX_DIM + IDX_HEADS

NSA_HEADS = D_MODEL // HEAD_DIM
NSA_GROUPS = 4
NSA_HPG = NSA_HEADS // NSA_GROUPS
CMP_LEN = 32
CMP_STRIDE = 16
CMP_HIDDEN = 256
SEL_LEN = 64
SEL_BLOCKS = 16
WINDOW = 512
FORCE_SCORE = 1e4
NSA_KV = NSA_GROUPS * HEAD_DIM
NSA_IN = NSA_HEADS * HEAD_DIM + 6 * NSA_KV + 3 * NSA_HEADS

FFN_DIM = 2816
N_EXPERTS = 8
TOP_K = 2
EXPERT_DIM = 3584
MOE_BLOCK = 512

kernel_name = 'hybrid_dsa_nsa_moe_adaln'


def rmsnorm(x, g):
    xf = x.astype(jnp.float32)
    y = xf * lax.rsqrt(jnp.mean(xf * xf, axis=-1, keepdims=True) + NORM_EPS)
    return (y * g.astype(jnp.float32)).astype(x.dtype)


def rope_tables(seq):
    inv = ROPE_THETA ** (-jnp.arange(0, ROPE_DIM, 2, dtype=jnp.float32) / ROPE_DIM)
    ang = jnp.arange(seq, dtype=jnp.float32)[:, None] * inv[None, :]
    return jnp.cos(ang), jnp.sin(ang)


def apply_partial_rope(x, cos, sin):
    shp = (cos.shape[0],) + (1,) * (x.ndim - 3) + (cos.shape[1],)
    co, si = cos.reshape(shp), sin.reshape(shp)
    half = ROPE_DIM // 2
    xr = x[..., :ROPE_DIM].astype(jnp.float32)
    x1, x2 = xr[..., :half], xr[..., half:]
    rot = jnp.concatenate([x1 * co - x2 * si, x2 * co + x1 * si], axis=-1).astype(x.dtype)
    return jnp.concatenate([rot, x[..., ROPE_DIM:]], axis=-1)


def masked_softmax(s, valid):
    s = jnp.where(valid, s.astype(jnp.float32), -jnp.inf)
    m = jnp.max(s, axis=-1, keepdims=True)
    m = jnp.where(jnp.isfinite(m), m, 0.0)
    e = jnp.where(valid, jnp.exp(s - m), 0.0)
    return e / jnp.maximum(jnp.sum(e, axis=-1, keepdims=True), 1e-30)


def modulate(x, c, norm_g, w_ada, b_ada):
    mod = jax.nn.silu(c) @ w_ada + b_ada
    shift, scale, gate = jnp.split(mod, 3, axis=-1)
    h = rmsnorm(x, norm_g) * (1.0 + scale[:, None, :]) + shift[:, None, :]
    return h, gate[:, None, :]


def sweep_query_blocks(fn, batch, seq):
    nq = seq // Q_BLOCK
    out = lax.map(lambda i: fn(i // nq, (i % nq) * Q_BLOCK), jnp.arange(batch * nq))
    return out.reshape((batch, seq) + out.shape[2:])


def dsa_mixer(h, cos, sin, w_in, g_q, w_uq, g_kv, w_uk, w_uv, w_iq, w_o):
    B, S, _ = h.shape
    proj = h @ w_in
    cuts = np.cumsum([DSA_Q_LORA, DSA_KV_LORA, ROPE_DIM, IDX_DIM]).tolist()
    q_lat, kv_lat, k_rope, idx_k, idx_w = jnp.split(proj, cuts, axis=-1)
    q_lat = rmsnorm(q_lat, g_q)
    q = apply_partial_rope((q_lat @ w_uq).reshape(B, S, DSA_HEADS, HEAD_DIM), cos, sin)
    q_rope, q_nope = q[..., :ROPE_DIM], q[..., ROPE_DIM:]
    q_abs = jnp.einsum('bshn,hcn->bshc', q_nope, w_uk)
    q_cat = jnp.concatenate([q_rope, q_abs], axis=-1)
    c_kv = rmsnorm(kv_lat, g_kv)
    k_cat = jnp.concatenate([apply_partial_rope(k_rope, cos, sin), c_kv], axis=-1)
    iq = apply_partial_rope((q_lat @ w_iq).reshape(B, S, IDX_HEADS, IDX_DIM), cos, sin)
    ik = apply_partial_rope(idx_k, cos, sin)
    iw = idx_w.astype(jnp.float32) * (IDX_HEADS ** -0.5)
    k_sel = min(DSA_TOPK, S // 4)
    scale = HEAD_DIM ** -0.5

    def block(b, qs):
        t = qs + jnp.arange(Q_BLOCK)
        causal = jnp.arange(S)[None, :] <= t[:, None]
        iq_b = lax.dynamic_slice_in_dim(iq[b], qs, Q_BLOCK, 0)
        iw_b = lax.dynamic_slice_in_dim(iw[b], qs, Q_BLOCK, 0)
        logits = jnp.einsum('qhd,sd->qhs', iq_b, ik[b]).astype(jnp.float32) * (IDX_DIM ** -0.5)
        score = jnp.einsum('qhs,qh->qs', jax.nn.relu(logits), iw_b)
        score = jnp.where(causal, score, -jnp.inf)
        _, sel = lax.top_k(score, k_sel)
        kv = k_cat[b][sel]
        q_b = lax.dynamic_slice_in_dim(q_cat[b], qs, Q_BLOCK, 0)
        s = jnp.einsum('qhd,qkd->qhk', q_b, kv) * scale
        p = masked_softmax(s, (sel <= t[:, None])[:, None, :]).astype(kv.dtype)
        return jnp.einsum('qhk,qkc->qhc', p, kv[..., ROPE_DIM:])

    o_lat = sweep_query_blocks(block, B, S)
    o = jnp.einsum('bshc,hcv->bshv', o_lat, w_uv).reshape(B, S, DSA_HEADS * DSA_V_DIM)
    return o @ w_o


def nsa_mixer(h, cos, sin, w_in, cmp_pe, cmp_k1, cmp_k2, cmp_v1, cmp_v2, w_o):
    B, S, _ = h.shape
    G, HD = NSA_GROUPS, HEAD_DIM
    proj = h @ w_in
    cuts = np.cumsum([NSA_HEADS * HD] + [NSA_KV] * 6).tolist()
    q, kc, vc, ks, vs, kw, vw, g = jnp.split(proj, cuts, axis=-1)
    q = apply_partial_rope(q.reshape(B, S, NSA_HEADS, HD), cos, sin)
    kc = apply_partial_rope(kc.reshape(B, S, G, HD), cos, sin)
    ks = apply_partial_rope(ks.reshape(B, S, G, HD), cos, sin)
    kw = apply_partial_rope(kw.reshape(B, S, G, HD), cos, sin)
    vc, vs, vw = (v.reshape(B, S, G, HD) for v in (vc, vs, vw))
    gates = jax.nn.sigmoid(g.astype(jnp.float32)).reshape(B, S, NSA_HEADS, 3)

    nc = (S - CMP_LEN) // CMP_STRIDE + 1
    tok = jnp.arange(nc)[:, None] * CMP_STRIDE + jnp.arange(CMP_LEN)[None, :]

    def compress(arr, w1, w2):
        blk = arr[:, tok] + cmp_pe[:, None, :]
        blk = blk.transpose(0, 1, 3, 2, 4).reshape(B, nc, G, CMP_LEN * HD)
        return jax.nn.silu(blk @ w1) @ w2

    kcmp = compress(kc, cmp_k1, cmp_k2)
    vcmp = compress(vc, cmp_v1, cmp_v2)
    cmp_end = jnp.arange(nc) * CMP_STRIDE + CMP_LEN - 1

    nb = S // SEL_LEN
    n_sel = min(SEL_BLOCKS, nb)
    ks_blk = ks.reshape(B, nb, SEL_LEN, G, HD).transpose(0, 3, 1, 2, 4)
    vs_blk = vs.reshape(B, nb, SEL_LEN, G, HD).transpose(0, 3, 1, 2, 4)
    cstart = jnp.arange(nc) * CMP_STRIDE
    bstart = jnp.arange(nb) * SEL_LEN
    cover = ((cstart[:, None] < bstart[None, :] + SEL_LEN)
             & (cstart[:, None] + CMP_LEN > bstart[None, :])).astype(jnp.float32)

    kw_pad = jnp.pad(kw, ((0, 0), (WINDOW, 0), (0, 0), (0, 0)))
    vw_pad = jnp.pad(vw, ((0, 0), (WINDOW, 0), (0, 0), (0, 0)))
    scale = HD ** -0.5
    gidx = jnp.arange(G)[None, :, None]

    def block(b, qs):
        t = qs + jnp.arange(Q_BLOCK)
        qb = lax.dynamic_slice_in_dim(q[b], qs, Q_BLOCK, 0).reshape(Q_BLOCK, G, NSA_HPG, HD)
        sc = jnp.einsum('qghd,ngd->qghn', qb, kcmp[b]) * scale
        pc = masked_softmax(sc, (cmp_end[None, :] <= t[:, None])[:, None, None, :])
        oc = jnp.einsum('qghn,ngd->qghd', pc.astype(qb.dtype), vcmp[b])
        imp = jnp.einsum('qgn,nj->qgj', jnp.sum(pc, axis=2), cover)
        jb = jnp.arange(nb)[None, :]
        cur = (t // SEL_LEN)[:, None]
        forced = (jb == 0) | (jb == cur) | (jb == cur - 1)
        imp = jnp.where(forced[:, None, :], FORCE_SCORE, imp)
        imp = jnp.where((jb <= cur)[:, None, :], imp, -jnp.inf)
        _, sel = lax.top_k(imp, n_sel)
        kb = ks_blk[b][gidx, sel].reshape(Q_BLOCK, G, n_sel * SEL_LEN, HD)
        vb = vs_blk[b][gidx, sel].reshape(Q_BLOCK, G, n_sel * SEL_LEN, HD)
        pos = (sel[..., None] * SEL_LEN + jnp.arange(SEL_LEN)).reshape(Q_BLOCK, G, n_sel * SEL_LEN)
        ss = jnp.einsum('qghd,qgkd->qghk', qb, kb) * scale
        ps = masked_softmax(ss, (pos <= t[:, None, None])[:, :, None, :])
        osel = jnp.einsum('qghk,qgkd->qghd', ps.astype(qb.dtype), vb)
        kwb = lax.dynamic_slice_in_dim(kw_pad[b], qs, WINDOW + Q_BLOCK, 0)
        vwb = lax.dynamic_slice_in_dim(vw_pad[b], qs, WINDOW + Q_BLOCK, 0)
        kpos = qs - WINDOW + jnp.arange(WINDOW + Q_BLOCK)
        wmask = ((kpos[None, :] <= t[:, None]) & (kpos[None, :] > t[:, None] - WINDOW)
                 & (kpos[None, :] >= 0))
        sw = jnp.einsum('qghd,kgd->qghk', qb, kwb) * scale
        pw = masked_softmax(sw, wmask[:, None, None, :])
        ow = jnp.einsum('qghk,kgd->qghd', pw.astype(qb.dtype), vwb)
        gb = lax.dynamic_slice_in_dim(gates[b], qs, Q_BLOCK, 0).reshape(Q_BLOCK, G, NSA_HPG, 3)
        o = gb[..., 0:1] * oc + gb[..., 1:2] * osel + gb[..., 2:3] * ow
        return o.astype(qb.dtype).reshape(Q_BLOCK, NSA_HEADS * HD)

    o = sweep_query_blocks(block, B, S)
    return o @ w_o


def swiglu(h, w1, w3, w2):
    return (jax.nn.silu(h @ w1) * (h @ w3)) @ w2


def moe_swiglu(h, w_router, w1, w3, w2):
    B, S, D = h.shape
    T = B * S
    hf = h.reshape(T, D)
    logits = (hf @ w_router).astype(jnp.float32)
    top_val, top_idx = lax.top_k(logits, TOP_K)
    gate = jax.nn.softmax(top_val, axis=-1)
    flat_e = top_idx.reshape(-1)
    flat_tok = jnp.repeat(jnp.arange(T), TOP_K)
    order = jnp.argsort(flat_e)
    e_sorted = flat_e[order]
    tok_sorted = flat_tok[order]
    gate_sorted = gate.reshape(-1)[order]
    counts = jnp.bincount(flat_e, length=N_EXPERTS)
    padded = (counts + MOE_BLOCK - 1) // MOE_BLOCK * MOE_BLOCK
    ends = jnp.cumsum(padded)
    pstart = ends - padded
    gstart = jnp.cumsum(counts) - counts
    dest = pstart[e_sorted] + (jnp.arange(T * TOP_K) - gstart[e_sorted])
    n_blocks = -(-(T * TOP_K) // MOE_BLOCK) + N_EXPERTS
    slot_tok = jnp.full((n_blocks * MOE_BLOCK,), T, jnp.int32).at[dest].set(tok_sorted)
    block_e = jnp.minimum(jnp.searchsorted(ends, jnp.arange(n_blocks) * MOE_BLOCK, side='right'),
                          N_EXPERTS - 1)
    h_pad = jnp.concatenate([hf, jnp.zeros((1, D), hf.dtype)], axis=0)

    def run(i):
        rows = lax.dynamic_slice_in_dim(slot_tok, i * MOE_BLOCK, MOE_BLOCK)
        e = block_e[i]
        return swiglu(h_pad[rows], w1[e], w3[e], w2[e])

    ys = lax.map(run, jnp.arange(n_blocks)).reshape(-1, D)
    y = ys[dest] * gate_sorted[:, None].astype(ys.dtype)
    return jax.ops.segment_sum(y, tok_sorted, num_segments=T).reshape(B, S, D)


def setup_inputs(seed: int = 0) -> dict:
    key = jax.random.key(seed)
    ks = iter(jax.random.split(key, 40))
    D = D_MODEL

    def w(shape, fan_in):
        return jax.random.normal(next(ks), shape, jnp.float32) * (fan_in ** -0.5)

    def gain(shape):
        return 1.0 + 0.05 * jax.random.normal(next(ks), shape, jnp.float32)

    def small(shape, s):
        return s * jax.random.normal(next(ks), shape, jnp.float32)

    return {
        'x': jax.random.normal(next(ks), (BATCH, SEQ, D), jnp.float32),
        'c': jax.random.normal(next(ks), (BATCH, D), jnp.float32),
        'norm_mix': gain((DEPTH, D)),
        'norm_ffn': gain((DEPTH, D)),
        'ada_w': 0.5 * w((DEPTH, 2, D, 3 * D), D),
        'ada_b': small((DEPTH, 2, 3 * D), 0.02),
        'final_norm': gain((D,)),
        'dsa_w_in': w((N_EVEN, D, DSA_IN), D),
        'dsa_g_q': gain((N_EVEN, DSA_Q_LORA)),
        'dsa_w_uq': w((N_EVEN, DSA_Q_LORA, DSA_HEADS * HEAD_DIM), DSA_Q_LORA),
        'dsa_g_kv': gain((N_EVEN, DSA_KV_LORA)),
        'dsa_w_uk': w((N_EVEN, DSA_HEADS, DSA_KV_LORA, DSA_NOPE), DSA_KV_LORA),
        'dsa_w_uv': w((N_EVEN, DSA_HEADS, DSA_KV_LORA, DSA_V_DIM), DSA_KV_LORA),
        'dsa_w_iq': w((N_EVEN, DSA_Q_LORA, IDX_HEADS * IDX_DIM), DSA_Q_LORA),
        'dsa_w_o': w((N_EVEN, DSA_HEADS * DSA_V_DIM, D), DSA_HEADS * DSA_V_DIM),
        'ffn_w1': w((N_EVEN, D, FFN_DIM), D),
        'ffn_w3': w((N_EVEN, D, FFN_DIM), D),
        'ffn_w2': w((N_EVEN, FFN_DIM, D), FFN_DIM),
        'nsa_w_in': w((N_ODD, D, NSA_IN), D),
        'nsa_cmp_pe': small((N_ODD, CMP_LEN, HEAD_DIM), 0.1),
        'nsa_cmp_k1': w((N_ODD, CMP_LEN * HEAD_DIM, CMP_HIDDEN), CMP_LEN * HEAD_DIM),
        'nsa_cmp_k2': w((N_ODD, CMP_HIDDEN, HEAD_DIM), CMP_HIDDEN),
        'nsa_cmp_v1': w((N_ODD, CMP_LEN * HEAD_DIM, CMP_HIDDEN), CMP_LEN * HEAD_DIM),
        'nsa_cmp_v2': w((N_ODD, CMP_HIDDEN, HEAD_DIM), CMP_HIDDEN),
        'nsa_w_o': w((N_ODD, NSA_HEADS * HEAD_DIM, D), NSA_HEADS * HEAD_DIM),
        'moe_router': w((N_ODD, D, N_EXPERTS), D),
        'moe_w1': w((N_ODD, N_EXPERTS, D, EXPERT_DIM), D),
        'moe_w3': w((N_ODD, N_EXPERTS, D, EXPERT_DIM), D),
        'moe_w2': w((N_ODD, N_EXPERTS, EXPERT_DIM, D), EXPERT_DIM),
    }


def reference(x, c, norm_mix, norm_ffn, ada_w, ada_b, final_norm,
              dsa_w_in, dsa_g_q, dsa_w_uq, dsa_g_kv, dsa_w_uk, dsa_w_uv, dsa_w_iq, dsa_w_o,
              ffn_w1, ffn_w3, ffn_w2,
              nsa_w_in, nsa_cmp_pe, nsa_cmp_k1, nsa_cmp_k2, nsa_cmp_v1, nsa_cmp_v2, nsa_w_o,
              moe_router, moe_w1, moe_w3, moe_w2):
    cos, sin = rope_tables(x.shape[1])
    for i in range(DEPTH):
        j = i // 2
        h, gate = modulate(x, c, norm_mix[i], ada_w[i, 0], ada_b[i, 0])
        if i % 2 == 0:
            mix = dsa_mixer(h, cos, sin, dsa_w_in[j], dsa_g_q[j], dsa_w_uq[j], dsa_g_kv[j],
                            dsa_w_uk[j], dsa_w_uv[j], dsa_w_iq[j], dsa_w_o[j])
        else:
            mix = nsa_mixer(h, cos, sin, nsa_w_in[j], nsa_cmp_pe[j], nsa_cmp_k1[j], nsa_cmp_k2[j],
                            nsa_cmp_v1[j], nsa_cmp_v2[j], nsa_w_o[j])
        x = x + gate * mix
        h, gate = modulate(x, c, norm_ffn[i], ada_w[i, 1], ada_b[i, 1])
        if i % 2 == 0:
            f = swiglu(h, ffn_w1[j], ffn_w3[j], ffn_w2[j])
        else:
            f = moe_swiglu(h, moe_router[j], moe_w1[j], moe_w3[j], moe_w2[j])
        x = x + gate * f
    return rmsnorm(x, final_norm)
```

```python
import functools

import numpy as np
import jax
import jax.numpy as jnp
from jax import lax
from jax.experimental import pallas as pl
from jax.experimental.pallas import tpu as pltpu

F32 = jnp.float32
BF16 = jnp.bfloat16
HIGHEST = lax.Precision.HIGHEST
INT_MIN = -2147483648

HEAD_DIM = 64
ROPE_DIM = 16
ROPE_THETA = 500000.0
Q_BLOCK = 128
NORM_EPS = 1e-6

DSA_HEADS = 16
DSA_NOPE = 48
DSA_Q_LORA = 256
DSA_KV_LORA = 128
IDX_HEADS = 8
IDX_DIM = 64
DSA_TOPK = 256
DSA_KC = 512

NSA_HEADS = 16
NSA_GROUPS = 4
NSA_HPG = 4
CMP_LEN = 32
CMP_STRIDE = 16
CMP_HIDDEN = 256
SEL_LEN = 64
SEL_BLOCKS = 16
WINDOW = 512
FORCE_SCORE = 1e4
NSA_KC = 512

N_EXPERTS = 8
MOE_BLOCK = 512
TOKEN_TILE = 512
ROW_TILE = 256
LANES = 128

ATT_SCALE = HEAD_DIM ** -0.5


def _dot(a, b):
    return jnp.dot(a, b, preferred_element_type=F32)


def _dot_nt(a, b):
    return lax.dot_general(a, b, (((1,), (1,)), ((), ())), preferred_element_type=F32)


def _sigmoid(v):
    return 1.0 / (1.0 + jnp.exp(-v))


def _rms(v, g):
    return v * lax.rsqrt(jnp.mean(v * v, axis=-1, keepdims=True) + NORM_EPS) * g


def _modulate(v, g, scale, shift):
    return _rms(v, g) * (1.0 + scale) + shift


def _rope(v, c, sa, sb):
    return v * c + pltpu.roll(v, LANES - 8, 1) * sa + pltpu.roll(v, 8, 1) * sb


def _softmax_step(s, valid, m, l, acc, v):
    s = jnp.where(valid, s, -jnp.inf)
    m_new = jnp.maximum(m, jnp.max(s, axis=1, keepdims=True))
    p = jnp.exp(s - m_new)
    alpha = jnp.exp(m - m_new)
    l = alpha * l + jnp.sum(p, axis=1, keepdims=True)
    acc = alpha * acc + _dot(p.astype(BF16), v)
    return m_new, l, acc


def _ada_kernel(c_ref, w_ref, b_ref, o_ref):
    cv = c_ref[...]
    sc = cv * _sigmoid(cv)
    o_ref[0] = jnp.dot(sc, w_ref[0], precision=HIGHEST, preferred_element_type=F32) + b_ref[0]


def _ada(c, ada_w, ada_b):
    B, D = c.shape
    w = ada_w.reshape(4, D, 3 * D)
    b = ada_b.reshape(4, 1, 3 * D)
    return pl.pallas_call(
        _ada_kernel, grid=(4, 3),
        in_specs=[pl.BlockSpec((B, D), lambda l, j: (0, 0)),
                  pl.BlockSpec((1, D, D), lambda l, j: (l, 0, j)),
                  pl.BlockSpec((1, 1, D), lambda l, j: (l, 0, j))],
        out_specs=pl.BlockSpec((1, B, D), lambda l, j: (l, 0, j)),
        out_shape=jax.ShapeDtypeStruct((4, B, 3 * D), F32), name="ada")(c, w, b)


def _dsa_proj_kernel(x_ref, sh_ref, sc_ref, gn_ref, win_ref, gq_ref, gkv_ref, wuq_ref, wiq_ref,
                     wcat_ref, rc_ref, ra_ref, rb_ref, qcat_ref, iq_ref, kcat_ref, ik_ref, iw_ref):
    h = _modulate(x_ref[...], gn_ref[...], sc_ref[0], sh_ref[0])
    proj = _dot(h.astype(BF16), win_ref[...])
    q_lat = _rms(proj[:, :256], gq_ref[...]).astype(BF16)
    c_kv = _rms(proj[:, 256:384], gkv_ref[...])
    rc, ra, rb = rc_ref[...], ra_ref[...], rb_ref[...]
    rest = _rope(proj[:, 384:512], rc, ra, rb)
    lane = lax.broadcasted_iota(jnp.int32, rest.shape, 1)
    kcat_ref[:, :128] = c_kv.astype(BF16)
    kcat_ref[:, 128:] = jnp.where((lane >= 64) & (lane < 80), rest, 0.0).astype(BF16)
    ik_ref[...] = jnp.where(lane < 64, rest, 0.0).astype(BF16)
    iw_ref[...] = rest * (IDX_HEADS ** -0.5)
    q = _dot(q_lat, wuq_ref[...])
    for p in range(DSA_HEADS // 2):
        qp = (_rope(q[:, p * 128:(p + 1) * 128], rc, ra, rb) * ATT_SCALE).astype(BF16)
        qcat_ref[:, p * 512:(p + 1) * 512] = _dot(qp, wcat_ref[p]).astype(BF16)
    iqv = _dot(q_lat, wiq_ref[...])
    for hh in range(IDX_HEADS):
        iq_ref[:, hh * 128:(hh + 1) * 128] = (
            _rope(iqv[:, hh * 128:(hh + 1) * 128], rc, ra, rb) * (IDX_DIM ** -0.5)).astype(BF16)


def _dsa_proj(xf, shift, scale, gn, win, gq, gkv, wuq, wiq, wcat, rope, S):
    T, D = xf.shape
    tm = TOKEN_TILE
    nps = S // tm
    row = lambda i: (i, 0)
    bat = lambda i: (i // nps, 0, 0)
    pos = lambda i: (i % nps, 0)
    cst2 = lambda i: (0, 0)
    cst3 = lambda i: (0, 0, 0)
    return pl.pallas_call(
        _dsa_proj_kernel, grid=(T // tm,),
        in_specs=[pl.BlockSpec((tm, D), row),
                  pl.BlockSpec((1, 1, D), bat), pl.BlockSpec((1, 1, D), bat),
                  pl.BlockSpec((1, D), cst2),
                  pl.BlockSpec(win.shape, cst2),
                  pl.BlockSpec((1, DSA_Q_LORA), cst2), pl.BlockSpec((1, DSA_KV_LORA), cst2),
                  pl.BlockSpec(wuq.shape, cst2), pl.BlockSpec(wiq.shape, cst2),
                  pl.BlockSpec(wcat.shape, cst3),
                  pl.BlockSpec((tm, LANES), pos), pl.BlockSpec((tm, LANES), pos),
                  pl.BlockSpec((tm, LANES), pos)],
        out_specs=[pl.BlockSpec((tm, DSA_HEADS * 256), row),
                   pl.BlockSpec((tm, IDX_HEADS * 128), row),
                   pl.BlockSpec((tm, 256), row),
                   pl.BlockSpec((tm, LANES), row),
                   pl.BlockSpec((tm, LANES), row)],
        out_shape=[jax.ShapeDtypeStruct((T, DSA_HEADS * 256), BF16),
                   jax.ShapeDtypeStruct((T, IDX_HEADS * 128), BF16),
                   jax.ShapeDtypeStruct((T, 256), BF16),
                   jax.ShapeDtypeStruct((T, LANES), BF16),
                   jax.ShapeDtypeStruct((T, LANES), F32)],
        name="dsa_proj")(xf, shift, scale, gn, win, gq, gkv, wuq, wiq, wcat, *rope)


def _dsa_attn_kernel(q_ref, iq_ref, iw_ref, kcat_ref, ik_ref, o_ref, keys_ref, *, k_sel, idx_bits):
    QB, KC = Q_BLOCK, DSA_KC
    qi = pl.program_id(1)
    n_ch = (qi * QB) // KC + 1
    row_t = qi * QB + lax.broadcasted_iota(jnp.int32, (QB, 1), 0)
    iw = iw_ref[...]

    def score_chunk(c, carry):
        k0 = pl.multiple_of(c * KC, KC)
        ikc = ik_ref[0, pl.ds(k0, KC), :]
        sc = jnp.zeros((QB, KC), F32)
        for hh in range(IDX_HEADS):
            lg = _dot_nt(iq_ref[:, hh * 128:(hh + 1) * 128], ikc)
            sc = sc + jnp.maximum(lg, 0.0) * iw[:, 80 + hh:81 + hh]
        sc = sc + 0.0
        bits = pltpu.bitcast(sc, jnp.int32)
        key = jnp.where(bits < 0, bits ^ 0x7FFFFFFF, bits)
        pos = k0 + lax.broadcasted_iota(jnp.int32, (QB, KC), 1)
        keys_ref[c] = jnp.where(pos <= row_t, key, INT_MIN)
        return carry
    lax.fori_loop(0, n_ch, score_chunk, 0)

    def count(pred):
        def body(c, a):
            kk = keys_ref[c]
            pos = c * KC + lax.broadcasted_iota(jnp.int32, (QB, KC), 1)
            m = jnp.where(pred(kk, pos), 1.0, 0.0)
            for j in range(KC // LANES):
                a = a + m[:, j * LANES:(j + 1) * LANES]
            return a
        a = lax.fori_loop(0, n_ch, body, jnp.zeros((QB, LANES), F32))
        return jnp.sum(a, axis=1, keepdims=True)

    def bit_body(it, thr):
        cand = thr ^ jnp.left_shift(jnp.int32(1), 31 - it)
        return jnp.where(count(lambda kk, pos: kk >= cand) >= k_sel, cand, thr)
    thr = lax.fori_loop(0, 32, bit_body, jnp.full((QB, 1), INT_MIN, jnp.int32))

    n_gt = count(lambda kk, pos: kk > thr)
    n_ge = count(lambda kk, pos: kk >= thr)
    tie = jnp.where((n_ge > k_sel) & (thr > INT_MIN), 1.0, 0.0)

    @pl.when(jnp.max(tie) > 0.0)
    def _():
        need = k_sel - n_gt
        def jbit(it, jcut):
            cand = jcut | jnp.left_shift(jnp.int32(1), idx_bits - 1 - it)
            f = count(lambda kk, pos: (kk == thr) & (pos < cand))
            return jnp.where(f <= need, cand, jcut)
        jcut = lax.fori_loop(0, idx_bits, jbit, jnp.zeros((QB, 1), jnp.int32))
        def drop(c, carry):
            kk = keys_ref[c]
            pos = c * KC + lax.broadcasted_iota(jnp.int32, (QB, KC), 1)
            keys_ref[c] = jnp.where((kk == thr) & (pos >= jcut), INT_MIN, kk)
            return carry
        lax.fori_loop(0, n_ch, drop, 0)

    thr_eff = jnp.maximum(thr, INT_MIN + 1)
    for h in range(DSA_HEADS):
        qh = q_ref[:, h * 256:(h + 1) * 256]
        def chunk(c, carry, qh=qh):
            k0 = pl.multiple_of(c * KC, KC)
            kc = kcat_ref[0, pl.ds(k0, KC), :]
            s = _dot_nt(qh, kc)
            return _softmax_step(s, keys_ref[c] >= thr_eff, *carry, kc[:, :DSA_KV_LORA])
        init = (jnp.full((QB, 1), -1e30, F32), jnp.zeros((QB, 1), F32), jnp.zeros((QB, DSA_KV_LORA), F32))
        m, l, acc = lax.fori_loop(0, n_ch, chunk, init)
        o_ref[:, h * 128:(h + 1) * 128] = (acc / jnp.maximum(l, 1e-30)).astype(BF16)


def _dsa_attn(qcat, iq, iw, kcat, ik, B, S):
    T = B * S
    nq = S // Q_BLOCK
    row = lambda b, q: (b * nq + q, 0)
    bat = lambda b, q: (b, 0, 0)
    k_sel = min(DSA_TOPK, S // 4)
    kern = functools.partial(_dsa_attn_kernel, k_sel=k_sel, idx_bits=int(S).bit_length())
    return pl.pallas_call(
        kern, grid=(B, nq),
        in_specs=[pl.BlockSpec((Q_BLOCK, DSA_HEADS * 256), row),
                  pl.BlockSpec((Q_BLOCK, IDX_HEADS * 128), row),
                  pl.BlockSpec((Q_BLOCK, LANES), row),
                  pl.BlockSpec((1, S, 256), bat),
                  pl.BlockSpec((1, S, LANES), bat)],
        out_specs=pl.BlockSpec((Q_BLOCK, DSA_HEADS * DSA_KV_LORA), row),
        out_shape=jax.ShapeDtypeStruct((T, DSA_HEADS * DSA_KV_LORA), BF16),
        scratch_shapes=[pltpu.VMEM((S // DSA_KC, Q_BLOCK, DSA_KC), jnp.int32)],
        name="dsa_attn")(qcat, iq, iw, kcat.reshape(B, S, 256), ik.reshape(B, S, LANES))


def _wuvo_kernel(uv_ref, wo_ref, o_ref):
    o_ref[0] = jnp.dot(uv_ref[0], wo_ref[...], precision=HIGHEST,
                       preferred_element_type=F32).astype(BF16)


def _wuvo(w_uv, w_o):
    H, C, V = w_uv.shape
    D = w_o.shape[1]
    out = pl.pallas_call(
        _wuvo_kernel, grid=(H,),
        in_specs=[pl.BlockSpec((1, C, V), lambda h: (h, 0, 0)),
                  pl.BlockSpec((V, D), lambda h: (h, 0))],
        out_specs=pl.BlockSpec((1, C, D), lambda h: (h, 0, 0)),
        out_shape=jax.ShapeDtypeStruct((H, C, D), BF16), name="wuvo")(w_uv, w_o)
    return out.reshape(H * C, D)


def _post_kernel(a_ref, w_ref, x_ref, gate_ref, gn_ref, sc_ref, sh_ref, xo_ref, ho_ref):
    x1 = x_ref[...] + gate_ref[0] * _dot(a_ref[...], w_ref[...])
    xo_ref[...] = x1
    ho_ref[...] = _modulate(x1, gn_ref[...], sc_ref[0], sh_ref[0]).astype(ho_ref.dtype)


def _post(a, w, xf, gate, gn, scale, shift, S, h_dtype):
    T, D = xf.shape
    tm = TOKEN_TILE
    nps = S // tm
    row = lambda i: (i, 0)
    bat = lambda i: (i // nps, 0, 0)
    cst2 = lambda i: (0, 0)
    return pl.pallas_call(
        _post_kernel, grid=(T // tm,),
        in_specs=[pl.BlockSpec((tm, a.shape[1]), row), pl.BlockSpec(w.shape, cst2),
                  pl.BlockSpec((tm, D), row), pl.BlockSpec((1, 1, D), bat),
                  pl.BlockSpec((1, D), cst2), pl.BlockSpec((1, 1, D), bat), pl.BlockSpec((1, 1, D), bat)],
        out_specs=[pl.BlockSpec((tm, D), row), pl.BlockSpec((tm, D), row)],
        out_shape=[jax.ShapeDtypeStruct((T, D), F32), jax.ShapeDtypeStruct((T, D), h_dtype)],
        name="post")(a, w, xf, gate, gn, scale, shift)


def _ffn_kernel(h_ref, x_ref, gate_ref, w1_ref, w3_ref, w2_ref, xo_ref, acc_ref):
    j = pl.program_id(1)

    @pl.when(j == 0)
    def _():
        acc_ref[...] = jnp.zeros_like(acc_ref)
    hb = h_ref[...]
    a = _dot(hb, w1_ref[...])
    b = _dot(hb, w3_ref[...])
    acc_ref[...] += _dot((a * _sigmoid(a) * b).astype(BF16), w2_ref[...])

    @pl.when(j == pl.num_programs(1) - 1)
    def _():
        xo_ref[...] = x_ref[...] + gate_ref[0] * acc_ref[...]


def _ffn(h, xf, gate, w1, w3, w2, S, tn):
    T, D = xf.shape
    F = w1.shape[1]
    tm = TOKEN_TILE
    nps = S // tm
    return pl.pallas_call(
        _ffn_kernel, grid=(T // tm, F // tn),
        in_specs=[pl.BlockSpec((tm, D), lambda i, j: (i, 0)),
                  pl.BlockSpec((tm, D), lambda i, j: (i, 0)),
                  pl.BlockSpec((1, 1, D), lambda i, j: (i // nps, 0, 0)),
                  pl.BlockSpec((D, tn), lambda i, j: (0, j)),
                  pl.BlockSpec((D, tn), lambda i, j: (0, j)),
                  pl.BlockSpec((tn, D), lambda i, j: (j, 0))],
        out_specs=pl.BlockSpec((tm, D), lambda i, j: (i, 0)),
        out_shape=jax.ShapeDtypeStruct((T, D), F32),
        scratch_shapes=[pltpu.VMEM((tm, D), F32)],
        name="ffn")(h, xf, gate, w1, w3, w2)


def _nsa_proj_kernel(x_ref, sh_ref, sc_ref, gn_ref, wq_ref, wkv_ref, wg_ref, rc_ref, ra_ref, rb_ref,
                     q_ref, kc_ref, vc_ref, ks_ref, vs_ref, kw_ref, vw_ref, g_ref):
    hb = _modulate(x_ref[...], gn_ref[...], sc_ref[0], sh_ref[0]).astype(BF16)
    rc, ra, rb = rc_ref[...], ra_ref[...], rb_ref[...]

    def roped(v):
        return jnp.concatenate([_rope(v[:, :128], rc, ra, rb), _rope(v[:, 128:], rc, ra, rb)], axis=1)

    for p in range(NSA_HEADS // 2):
        qv = _dot(hb, wq_ref[:, p * 256:(p + 1) * 256])
        q_ref[:, p * 256:(p + 1) * 256] = (roped(qv) * ATT_SCALE).astype(BF16)
    outs = (kc_ref, vc_ref, ks_ref, vs_ref, kw_ref, vw_ref)
    for n, o_ref in enumerate(outs):
        v = _dot(hb, wkv_ref[:, n * 256:(n + 1) * 256])
        o_ref[...] = (roped(v) if n % 2 == 0 else v).astype(BF16)
    g_ref[...] = _sigmoid(_dot(hb, wg_ref[...]))


def _nsa_proj(xf, shift, scale, gn, wq, wkv, wg, rope, S):
    T, D = xf.shape
    tm = TOKEN_TILE
    nps = S // tm
    row = lambda i: (i, 0)
    bat = lambda i: (i // nps, 0, 0)
    pos = lambda i: (i % nps, 0)
    cst2 = lambda i: (0, 0)
    kv_spec = pl.BlockSpec((tm, 256), row)
    kv_shape = jax.ShapeDtypeStruct((T, 256), BF16)
    return pl.pallas_call(
        _nsa_proj_kernel, grid=(T // tm,),
        in_specs=[pl.BlockSpec((tm, D), row),
                  pl.BlockSpec((1, 1, D), bat), pl.BlockSpec((1, 1, D), bat),
                  pl.BlockSpec((1, D), cst2),
                  pl.BlockSpec(wq.shape, cst2), pl.BlockSpec(wkv.shape, cst2), pl.BlockSpec(wg.shape, cst2),
                  pl.BlockSpec((tm, LANES), pos), pl.BlockSpec((tm, LANES), pos),
                  pl.BlockSpec((tm, LANES), pos)],
        out_specs=[pl.BlockSpec((tm, NSA_HEADS * 128), row)] + [kv_spec] * 6
                  + [pl.BlockSpec((tm, LANES), row)],
        out_shape=[jax.ShapeDtypeStruct((T, NSA_HEADS * 128), BF16)] + [kv_shape] * 6
                  + [jax.ShapeDtypeStruct((T, LANES), F32)],
        name="nsa_proj")(xf, shift, scale, gn, wq, wkv, wg, *rope)


def _compress_kernel(uk_ref, uv_ref, pe_ref, k1_ref, k2_ref, v1_ref, v2_ref, ko_ref, vo_ref):
    half = CMP_STRIDE * HEAD_DIM
    pe = jnp.broadcast_to(pe_ref[...], (8, 2 * half)).astype(BF16)

    def comp(u_ref, w1_ref, w2_ref):
        bias = _dot(pe, w1_ref[...])[0:1]
        out = None
        for gg in range(2):
            u = u_ref[0, gg]
            a = _dot(u, w1_ref[:half, :])
            b = _dot(u, w1_ref[half:, :])
            nrow = b.shape[0]
            hid = a + pltpu.roll(b, nrow - 1, 0) + bias
            hid = (hid * _sigmoid(hid)).astype(BF16)
            o = _dot(hid, w2_ref[gg])
            out = o if out is None else out + o
        return out
    ko_ref[0] = comp(uk_ref, k1_ref, k2_ref).astype(BF16)
    vo_ref[0] = comp(uv_ref, v1_ref, v2_ref).astype(BF16)


def _compress(uk, uv, pe, k1, k2, v1, v2):
    B, G, nch, W = uk.shape
    u_spec = pl.BlockSpec((1, 2, nch, W), lambda b, p: (b, p, 0, 0))
    w1_spec = pl.BlockSpec(k1.shape, lambda b, p: (0, 0))
    w2_spec = pl.BlockSpec(k2.shape, lambda b, p: (0, 0, 0))
    o_spec = pl.BlockSpec((1, nch, LANES), lambda b, p: (b, 0, p))
    o_shape = jax.ShapeDtypeStruct((B, nch, G * HEAD_DIM), BF16)
    return pl.pallas_call(
        _compress_kernel, grid=(B, G // 2),
        in_specs=[u_spec, u_spec, pl.BlockSpec(pe.shape, lambda b, p: (0, 0)),
                  w1_spec, w2_spec, w1_spec, w2_spec],
        out_specs=[o_spec, o_spec], out_shape=[o_shape, o_shape],
        name="compress")(uk, uv, pe, k1, k2, v1, v2)


def _nsa_attn_kernel(q_ref, g_ref, kcmp_ref, vcmp_ref, ks_ref, vs_ref, kw_ref, vw_ref,
                     covt_ref, exp_ref, o_ref, oc_ref, psum_ref, imp_ref, selm_ref, *, n_sel):
    QB, KC = Q_BLOCK, NSA_KC
    qi = pl.program_id(1)
    qs = qi * QB
    row_t = qs + lax.broadcasted_iota(jnp.int32, (QB, 1), 0)
    nch = kcmp_ref.shape[1]
    nb = covt_ref.shape[0]
    n_ch = qs // KC + 1
    gates = g_ref[...]

    cmp_end = lax.broadcasted_iota(jnp.int32, (QB, nch), 1) * CMP_STRIDE + (CMP_LEN - 1)
    cvalid = cmp_end <= row_t
    for h in range(NSA_HEADS):
        g = h // NSA_HPG
        pr = slice((g // 2) * 128, (g // 2) * 128 + 128)
        s = jnp.where(cvalid, _dot_nt(q_ref[:, h * 128:(h + 1) * 128], kcmp_ref[0, :, pr]), -jnp.inf)
        m = jnp.max(s, axis=1, keepdims=True)
        m = jnp.where(m == -jnp.inf, 0.0, m)
        e = jnp.exp(s - m)
        p = e / jnp.maximum(jnp.sum(e, axis=1, keepdims=True), 1e-30)
        oc_ref[h] = _dot(p.astype(BF16), vcmp_ref[0, :, pr])
        if h % NSA_HPG == 0:
            psum_ref[g] = p
        else:
            psum_ref[g] += p

    jb = lax.broadcasted_iota(jnp.int32, (nb, QB), 0)
    cur = (qs + lax.broadcasted_iota(jnp.int32, (nb, QB), 1)) // SEL_LEN
    forced = (jb == 0) | (jb == cur) | (jb == cur - 1)
    ri = lax.broadcasted_iota(jnp.int32, (QB, QB), 0)
    ci = lax.broadcasted_iota(jnp.int32, (QB, QB), 1)
    eye = jnp.where(ri == ci, 1.0, 0.0).astype(BF16)
    for g in range(NSA_GROUPS):
        imp = lax.dot_general(covt_ref[...], psum_ref[g], (((1,), (1,)), ((), ())),
                              precision=HIGHEST, preferred_element_type=F32)
        imp = jnp.where(forced, FORCE_SCORE, imp)
        imp = jnp.where(jb <= cur, imp, -jnp.inf)
        imp_ref[...] = imp

        def rank_body(i, rank, imp=imp):
            ri_ = imp_ref[pl.ds(i, 1), :]
            ahead = (ri_ > imp) | ((ri_ == imp) & (i < jb))
            return rank + jnp.where(ahead, 1.0, 0.0)
        rank = lax.fori_loop(0, nb, rank_body, jnp.zeros((nb, QB), F32))
        selt = jnp.where(rank < n_sel, 1.0, 0.0).astype(BF16)
        sel = _dot_nt(eye, selt).astype(BF16)

        def expand(c, carry, sel=sel, g=g):
            selm_ref[g, c] = _dot(sel, exp_ref[c])
            return carry
        lax.fori_loop(0, n_ch, expand, 0)

    init = (jnp.full((QB, 1), -1e30, F32), jnp.zeros((QB, 1), F32), jnp.zeros((QB, LANES), F32))
    lane = lax.broadcasted_iota(jnp.int32, (QB, LANES), 1)
    pair_out = [None, None]
    for h in range(NSA_HEADS):
        g = h // NSA_HPG
        pr = slice((g // 2) * 128, (g // 2) * 128 + 128)
        qh = q_ref[:, h * 128:(h + 1) * 128]

        def sel_chunk(c, carry, qh=qh, g=g, pr=pr):
            k0 = pl.multiple_of(c * KC, KC)
            s = _dot_nt(qh, ks_ref[0, pl.ds(k0, KC), pr])
            pos = k0 + lax.broadcasted_iota(jnp.int32, (QB, KC), 1)
            valid = (selm_ref[g, c] > 0.5) & (pos <= row_t)
            return _softmax_step(s, valid, *carry, vs_ref[0, pl.ds(k0, KC), pr])
        _, l, acc = lax.fori_loop(0, n_ch, sel_chunk, init)
        osel = acc / jnp.maximum(l, 1e-30)

        def win_chunk(c, carry, qh=qh, pr=pr):
            k0 = pl.multiple_of(c * QB, QB)
            s = _dot_nt(qh, kw_ref[0, pl.ds(k0, QB), pr])
            pos = k0 + lax.broadcasted_iota(jnp.int32, (QB, QB), 1)
            valid = (pos <= row_t) & (pos > row_t - WINDOW)
            return _softmax_step(s, valid, *carry, vw_ref[0, pl.ds(k0, QB), pr])
        _, l, acc = lax.fori_loop(jnp.maximum(qi - WINDOW // QB, 0), qi + 1, win_chunk, init)
        ow = acc / jnp.maximum(l, 1e-30)

        o = (gates[:, 3 * h:3 * h + 1] * oc_ref[h] + gates[:, 3 * h + 1:3 * h + 2] * osel
             + gates[:, 3 * h + 2:3 * h + 3] * ow)
        pair_out[h % 2] = o
        if h % 2 == 1:
            if g % 2 == 0:
                both = jnp.where(lane < 64, pair_out[0], pltpu.roll(pair_out[1], 64, 1))
            else:
                both = jnp.where(lane < 64, pltpu.roll(pair_out[0], 64, 1), pair_out[1])
            o_ref[:, (h // 2) * 128:(h // 2) * 128 + 128] = both.astype(BF16)


def _nsa_attn(q, gates, kcmp, vcmp, ks, vs, kw, vw, covt, expand, B, S):
    T = B * S
    nq = S // Q_BLOCK
    nch = kcmp.shape[1]
    nb = S // SEL_LEN
    row = lambda b, i: (b * nq + i, 0)
    bat = lambda b, i: (b, 0, 0)
    kv = lambda a: a.reshape(B, S, 256)
    kv_spec = pl.BlockSpec((1, S, 256), bat)
    cmp_spec = pl.BlockSpec((1, nch, 256), bat)
    kern = functools.partial(_nsa_attn_kernel, n_sel=min(SEL_BLOCKS, nb))
    return pl.pallas_call(
        kern, grid=(B, nq),
        in_specs=[pl.BlockSpec((Q_BLOCK, NSA_HEADS * 128), row), pl.BlockSpec((Q_BLOCK, LANES), row),
                  cmp_spec, cmp_spec, kv_spec, kv_spec, kv_spec, kv_spec,
                  pl.BlockSpec(covt.shape, lambda b, i: (0, 0)),
                  pl.BlockSpec(expand.shape, lambda b, i: (0, 0, 0))],
        out_specs=pl.BlockSpec((Q_BLOCK, NSA_HEADS * HEAD_DIM), row),
        out_shape=jax.ShapeDtypeStruct((T, NSA_HEADS * HEAD_DIM), BF16),
        scratch_shapes=[pltpu.VMEM((NSA_HEADS, Q_BLOCK, LANES), F32),
                        pltpu.VMEM((NSA_GROUPS, Q_BLOCK, nch), F32),
                        pltpu.VMEM((nb, Q_BLOCK), F32),
                        pltpu.VMEM((NSA_GROUPS, S // NSA_KC, Q_BLOCK, NSA_KC), F32)],
        name="nsa_attn")(q, gates, kcmp, vcmp, kv(ks), kv(vs), kv(kw), kv(vw), covt, expand)


def _router_kernel(h_ref, wr_ref, route_ref, cnt_ref, carry_ref):
    i = pl.program_id(0)
    tm = h_ref.shape[0]

    @pl.when(i == 0)
    def _():
        carry_ref[...] = jnp.zeros_like(carry_ref)
    lane = lax.broadcasted_iota(jnp.int32, (tm, LANES), 1).astype(F32)
    lg = jnp.dot(h_ref[...], wr_ref[...], precision=HIGHEST, preferred_element_type=F32)
    lg = jnp.where(lane < N_EXPERTS, lg, -jnp.inf)
    v1 = jnp.max(lg, axis=1, keepdims=True)
    i1 = jnp.min(jnp.where(lg == v1, lane, float(LANES)), axis=1, keepdims=True)
    lg2 = jnp.where(lane == i1, -jnp.inf, lg)
    v2 = jnp.max(lg2, axis=1, keepdims=True)
    i2 = jnp.min(jnp.where(lg2 == v2, lane, float(LANES)), axis=1, keepdims=True)
    e2 = jnp.exp(v2 - v1)
    g1 = 1.0 / (1.0 + e2)
    g2 = e2 / (1.0 + e2)
    oh1 = jnp.where(lane == i1, 1.0, 0.0)
    oh2 = jnp.where(lane == i2, 1.0, 0.0)
    both = oh1 + oh2
    ri = lax.broadcasted_iota(jnp.int32, (tm, tm), 0)
    ci = lax.broadcasted_iota(jnp.int32, (tm, tm), 1)
    lower = jnp.where(ri > ci, 1.0, 0.0).astype(BF16)
    tot = carry_ref[0:1, :] + _dot(lower, both.astype(BF16))
    r1 = jnp.sum(oh1 * tot, axis=1, keepdims=True)
    r2 = jnp.sum(oh2 * tot, axis=1, keepdims=True)
    new_carry = carry_ref[...] + jnp.sum(both, axis=0, keepdims=True)
    carry_ref[...] = new_carry
    cnt_ref[...] = new_carry
    out = jnp.zeros((tm, LANES), F32)
    for col, val in enumerate((i1, i2, g1, g2, r1, r2)):
        out = jnp.where(lane == col, val, out)
    route_ref[...] = out


def _router(h, wr):
    T, D = h.shape
    tm = TOKEN_TILE
    return pl.pallas_call(
        _router_kernel, grid=(T // tm,),
        in_specs=[pl.BlockSpec((tm, D), lambda i: (i, 0)), pl.BlockSpec(wr.shape, lambda i: (0, 0))],
        out_specs=[pl.BlockSpec((tm, LANES), lambda i: (i, 0)), pl.BlockSpec((8, LANES), lambda i: (0, 0))],
        out_shape=[jax.ShapeDtypeStruct((T, LANES), F32), jax.ShapeDtypeStruct((8, LANES), F32)],
        scratch_shapes=[pltpu.VMEM((8, LANES), F32)],
        name="router")(h, wr)


def _row_copy(src, dst, sem):
    return pltpu.make_async_copy(src, dst, sem)


def _moe_scatter_kernel(dest_ref, h_ref, xs_in_ref, xs_ref, sem):
    del xs_in_ref
    i = pl.program_id(0)
    tm = h_ref.shape[0]

    def start(r, carry):
        for k in range(2):
            d = dest_ref[(i * tm + r) * 2 + k]
            _row_copy(h_ref.at[pl.ds(r, 1), :], xs_ref.at[pl.ds(d, 1), :], sem).start()
        return carry
    lax.fori_loop(0, tm, start, 0)

    def wait(r, carry):
        _row_copy(h_ref.at[pl.ds(0, 1), :], xs_ref.at[pl.ds(0, 1), :], sem).wait()
        return carry
    lax.fori_loop(0, 2 * tm, wait, 0)


def _moe_scatter(dest, h, xs0):
    T, D = h.shape
    tm = ROW_TILE
    return pl.pallas_call(
        _moe_scatter_kernel,
        grid_spec=pltpu.PrefetchScalarGridSpec(
            num_scalar_prefetch=1, grid=(T // tm,),
            in_specs=[pl.BlockSpec((tm, D), lambda i, d: (i, 0)), pl.BlockSpec(memory_space=pl.ANY)],
            out_specs=pl.BlockSpec(memory_space=pl.ANY),
            scratch_shapes=[pltpu.SemaphoreType.DMA(())]),
        out_shape=jax.ShapeDtypeStruct(xs0.shape, xs0.dtype),
        input_output_aliases={2: 0}, name="moe_scatter")(dest, h, xs0)


def _moe_ffn_kernel(be_ref, nu_ref, x_ref, w1_ref, w3_ref, w2_ref, y_ref, xb_ref, acc_ref):
    i = pl.program_id(0)
    j = pl.program_id(1)
    used = i < nu_ref[0]

    @pl.when(j == 0)
    def _():
        xb_ref[...] = x_ref[...].astype(BF16)
        acc_ref[...] = jnp.zeros_like(acc_ref)

    @pl.when(used)
    def _():
        xb = xb_ref[...]
        a = _dot(xb, w1_ref[0])
        b = _dot(xb, w3_ref[0])
        acc_ref[...] += _dot((a * _sigmoid(a) * b).astype(BF16), w2_ref[0])

    @pl.when(j == pl.num_programs(1) - 1)
    def _():
        y_ref[...] = acc_ref[...]


def _moe_ffn(block_e, n_used, xs, w1, w3, w2, tn):
    NS, D = xs.shape
    E, _, F = w1.shape
    nj = F // tn
    jj = lambda i, j, nu: jnp.where(i < nu[0], j, nj - 1)
    return pl.pallas_call(
        _moe_ffn_kernel,
        grid_spec=pltpu.PrefetchScalarGridSpec(
            num_scalar_prefetch=2, grid=(NS // MOE_BLOCK, nj),
            in_specs=[pl.BlockSpec((MOE_BLOCK, D), lambda i, j, be, nu: (i, 0)),
                      pl.BlockSpec((1, D, tn), lambda i, j, be, nu: (be[i], 0, jj(i, j, nu))),
                      pl.BlockSpec((1, D, tn), lambda i, j, be, nu: (be[i], 0, jj(i, j, nu))),
                      pl.BlockSpec((1, tn, D), lambda i, j, be, nu: (be[i], jj(i, j, nu), 0))],
            out_specs=pl.BlockSpec((MOE_BLOCK, D), lambda i, j, be, nu: (i, 0)),
            scratch_shapes=[pltpu.VMEM((MOE_BLOCK, D), BF16), pltpu.VMEM((MOE_BLOCK, D), F32)]),
        out_shape=jax.ShapeDtypeStruct((NS, D), F32), name="moe_ffn")(block_e, n_used, xs, w1, w3, w2)


def _moe_combine_kernel(dest_ref, ys_ref, x_ref, gate_ref, route_ref, fn_ref, o_ref, ybuf_ref, sem):
    i = pl.program_id(0)
    tm = x_ref.shape[0]

    def start(r, carry):
        for k in range(2):
            d = dest_ref[(i * tm + r) * 2 + k]
            _row_copy(ys_ref.at[pl.ds(d, 1), :], ybuf_ref.at[k, pl.ds(r, 1), :], sem).start()
        return carry
    lax.fori_loop(0, tm, start, 0)

    def wait(r, carry):
        _row_copy(ys_ref.at[pl.ds(0, 1), :], ybuf_ref.at[0, pl.ds(0, 1), :], sem).wait()
        return carry
    lax.fori_loop(0, 2 * tm, wait, 0)
    route = route_ref[...]
    y = route[:, 2:3] * ybuf_ref[0] + route[:, 3:4] * ybuf_ref[1]
    o_ref[...] = _rms(x_ref[...] + gate_ref[0] * y, fn_ref[...])


def _moe_combine(dest, ys, xf, gate, route, fn, S):
    T, D = xf.shape
    tm = ROW_TILE
    nps = S // tm
    return pl.pallas_call(
        _moe_combine_kernel,
        grid_spec=pltpu.PrefetchScalarGridSpec(
            num_scalar_prefetch=1, grid=(T // tm,),
            in_specs=[pl.BlockSpec(memory_space=pl.ANY),
                      pl.BlockSpec((tm, D), lambda i, d: (i, 0)),
                      pl.BlockSpec((1, 1, D), lambda i, d: (i // nps, 0, 0)),
                      pl.BlockSpec((tm, LANES), lambda i, d: (i, 0)),
                      pl.BlockSpec((1, D), lambda i, d: (0, 0))],
            out_specs=pl.BlockSpec((tm, D), lambda i, d: (i, 0)),
            scratch_shapes=[pltpu.VMEM((2, tm, D), F32), pltpu.SemaphoreType.DMA(())]),
        out_shape=jax.ShapeDtypeStruct((T, D), F32), name="moe_combine")(dest, ys, xf, gate, route, fn)


def _rope_tables(S):
    inv = ROPE_THETA ** (-jnp.arange(0, ROPE_DIM, 2, dtype=F32) / ROPE_DIM)
    ang = jnp.arange(S, dtype=F32)[:, None] * inv[None, :]
    cos, sin = jnp.cos(ang), jnp.sin(ang)
    pm = np.arange(LANES) % HEAD_DIM
    col = pm % (ROPE_DIM // 2)
    rc = jnp.where((pm < ROPE_DIM)[None, :], cos[:, col], 1.0)
    ra = jnp.where((pm < ROPE_DIM // 2)[None, :], -sin[:, col], 0.0)
    rb = jnp.where(((pm >= ROPE_DIM // 2) & (pm < ROPE_DIM))[None, :], sin[:, col], 0.0)
    return rc, ra, rb


def _dsa_weights(w_in, w_uk, w_iq):
    D = w_in.shape[0]
    a, b, c, d = DSA_Q_LORA, DSA_Q_LORA + DSA_KV_LORA, DSA_Q_LORA + DSA_KV_LORA + ROPE_DIM, \
        DSA_Q_LORA + DSA_KV_LORA + ROPE_DIM + IDX_DIM
    win = jnp.concatenate([w_in[:, :b], w_in[:, c:d], w_in[:, b:c], w_in[:, d:],
                           jnp.zeros((D, 512 - w_in.shape[1]), F32)], axis=1).astype(BF16)
    H = DSA_HEADS
    blk = jnp.zeros((H, HEAD_DIM, 256), F32)
    blk = blk.at[:, ROPE_DIM:, :DSA_KV_LORA].set(jnp.transpose(w_uk, (0, 2, 1)))
    blk = blk.at[:, :ROPE_DIM, 192:192 + ROPE_DIM].set(jnp.eye(ROPE_DIM, dtype=F32))
    z = jnp.zeros((H // 2, HEAD_DIM, 256), F32)
    wcat = jnp.concatenate([jnp.concatenate([blk[0::2], z], axis=2),
                            jnp.concatenate([z, blk[1::2]], axis=2)], axis=1).astype(BF16)
    wiq = w_iq.reshape(DSA_Q_LORA, IDX_HEADS, IDX_DIM)
    wiq = jnp.concatenate([wiq, jnp.zeros_like(wiq)], axis=2).reshape(DSA_Q_LORA, IDX_HEADS * 128)
    return win, wcat, wiq.astype(BF16)


def _nsa_weights(w_in):
    D = w_in.shape[0]
    nq = NSA_HEADS * HEAD_DIM
    wq = w_in[:, :nq].reshape(D, NSA_HEADS, HEAD_DIM)
    z = jnp.zeros_like(wq)
    odd = ((np.arange(NSA_HEADS) // NSA_HPG) % 2 == 1)[None, :, None]
    wq = jnp.concatenate([jnp.where(odd, z, wq), jnp.where(odd, wq, z)], axis=2).reshape(D, NSA_HEADS * 128)
    wkv = w_in[:, nq:nq + 6 * 256]
    wg = jnp.concatenate([w_in[:, nq + 6 * 256:], jnp.zeros((D, LANES - 3 * NSA_HEADS), F32)], axis=1)
    return wq.astype(BF16), wkv.astype(BF16), wg.astype(BF16)


def _nsa_tables(S):
    nch = S // CMP_STRIDE
    nc = (S - CMP_LEN) // CMP_STRIDE + 1
    nb = S // SEL_LEN
    cstart = np.arange(nch) * CMP_STRIDE
    bstart = np.arange(nb) * SEL_LEN
    cov = ((cstart[None, :] < bstart[:, None] + SEL_LEN) & (cstart[None, :] + CMP_LEN > bstart[:, None])
           & (np.arange(nch)[None, :] < nc)).astype(np.float32)
    kpos = np.arange(S).reshape(S // NSA_KC, 1, NSA_KC)
    expand = (kpos // SEL_LEN == np.arange(nb)[None, :, None]).astype(np.float32)
    return jnp.asarray(cov), jnp.asarray(expand, dtype=BF16)


def _chunk_tokens(a, B, S):
    a = a.reshape(B, S // CMP_STRIDE, CMP_STRIDE, NSA_GROUPS, HEAD_DIM)
    return jnp.transpose(a, (0, 3, 1, 2, 4)).reshape(B, NSA_GROUPS, S // CMP_STRIDE, CMP_STRIDE * HEAD_DIM)


def kernel(x, c, norm_mix, norm_ffn, ada_w, ada_b, final_norm, dsa_w_in, dsa_g_q, dsa_w_uq, dsa_g_kv,
           dsa_w_uk, dsa_w_uv, dsa_w_iq, dsa_w_o, ffn_w1, ffn_w3, ffn_w2, nsa_w_in, nsa_cmp_pe,
           nsa_cmp_k1, nsa_cmp_k2, nsa_cmp_v1, nsa_cmp_v2, nsa_w_o, moe_router, moe_w1, moe_w3, moe_w2):
    B, S, D = x.shape
    T = B * S
    xf = x.reshape(T, D)
    mods = _ada(c, ada_w, ada_b).reshape(4, B, 3, 1, D)
    shift = lambda s: mods[s, :, 0]
    scale = lambda s: mods[s, :, 1]
    gate = lambda s: mods[s, :, 2]
    rope = _rope_tables(S)

    win, wcat, wiq = _dsa_weights(dsa_w_in[0], dsa_w_uk[0], dsa_w_iq[0])
    qcat, iq, kcat, ik, iw = _dsa_proj(
        xf, shift(0), scale(0), norm_mix[0:1], win, dsa_g_q[0:1], dsa_g_kv[0:1],
        dsa_w_uq[0].astype(BF16), wiq, wcat, rope, S)
    olat = _dsa_attn(qcat, iq, iw, kcat, ik, B, S)
    wuvo = _wuvo(dsa_w_uv[0], dsa_w_o[0])
    x1, h1 = _post(olat, wuvo, xf, gate(0), norm_ffn[0:1], scale(1), shift(1), S, BF16)
    x2 = _ffn(h1, x1, gate(1), ffn_w1[0].astype(BF16), ffn_w3[0].astype(BF16), ffn_w2[0].astype(BF16),
              S, ffn_w1.shape[2] // 2)

    wq, wkv, wg = _nsa_weights(nsa_w_in[0])
    q, kc, vc, ks, vs, kw, vw, gates = _nsa_proj(xf=x2, shift=shift(2), scale=scale(2), gn=norm_mix[1:2],
                                                 wq=wq, wkv=wkv, wg=wg, rope=rope, S=S)
    zpad = jnp.zeros((CMP_HIDDEN, HEAD_DIM), F32)
    pad2 = lambda w2: jnp.stack([jnp.concatenate([w2, zpad], axis=1),
                                 jnp.concatenate([zpad, w2], axis=1)]).astype(BF16)
    kcmp, vcmp = _compress(_chunk_tokens(kc, B, S), _chunk_tokens(vc, B, S),
                           nsa_cmp_pe[0].reshape(1, CMP_LEN * HEAD_DIM),
                           nsa_cmp_k1[0].astype(BF16), pad2(nsa_cmp_k2[0]),
                           nsa_cmp_v1[0].astype(BF16), pad2(nsa_cmp_v2[0]))
    covt, expand = _nsa_tables(S)
    o = _nsa_attn(q, gates, kcmp, vcmp, ks, vs, kw, vw, covt, expand, B, S)
    x3, h3 = _post(o, nsa_w_o[0].astype(BF16), x2, gate(2), norm_ffn[1:2], scale(3), shift(3), S, F32)

    wr = jnp.concatenate([moe_router[0], jnp.zeros((D, LANES - N_EXPERTS), F32)], axis=1)
    route, cnt = _router(h3, wr)
    counts = cnt[0, :N_EXPERTS].astype(jnp.int32)
    padded = (counts + MOE_BLOCK - 1) // MOE_BLOCK * MOE_BLOCK
    ends = jnp.cumsum(padded)
    pstart = ends - padded
    eidx = route[:, 0:2].astype(jnp.int32)
    dest = (pstart[eidx] + route[:, 4:6].astype(jnp.int32)).reshape(-1)
    n_blocks = -(-(T * 2) // MOE_BLOCK) + N_EXPERTS
    block_e = jnp.minimum(jnp.searchsorted(ends, jnp.arange(n_blocks) * MOE_BLOCK, side='right'),
                          N_EXPERTS - 1).astype(jnp.int32)
    n_used = (ends[-1:] // MOE_BLOCK).astype(jnp.int32)
    xs = _moe_scatter(dest, h3, jnp.zeros((n_blocks * MOE_BLOCK, D), F32))
    ys = _moe_ffn(block_e, n_used, xs, moe_w1[0].astype(BF16), moe_w3[0].astype(BF16),
                  moe_w2[0].astype(BF16), 512)
    out = _moe_combine(dest, ys, x3, gate(3), route, final_norm.reshape(1, D), S)
    return out.reshape(B, S, D)
```

```python
import functools

import numpy as np
import jax
import jax.numpy as jnp
from jax import lax
from jax.experimental import pallas as pl
from jax.experimental.pallas import tpu as pltpu

F32 = jnp.float32
BF16 = jnp.bfloat16
HIGHEST = lax.Precision.HIGHEST
INT_MIN = -2147483648

HEAD_DIM = 64
ROPE_DIM = 16
ROPE_THETA = 500000.0
Q_BLOCK = 128
NORM_EPS = 1e-6

DSA_HEADS = 16
DSA_NOPE = 48
DSA_Q_LORA = 256
DSA_KV_LORA = 128
IDX_HEADS = 8
IDX_DIM = 64
DSA_TOPK = 256
DSA_KC = 512

NSA_HEADS = 16
NSA_GROUPS = 4
NSA_HPG = 4
CMP_LEN = 32
CMP_STRIDE = 16
CMP_HIDDEN = 256
SEL_LEN = 64
SEL_BLOCKS = 16
WINDOW = 512
FORCE_SCORE = 1e4
NSA_KC = 512

N_EXPERTS = 8
MOE_BLOCK = 512
TOKEN_TILE = 512
ROW_TILE = 256
LANES = 128

LOG2E = 1.4426950408889634
ATT_SCALE = HEAD_DIM ** -0.5 * LOG2E


def _dot(a, b):
    return jnp.dot(a, b, preferred_element_type=F32)


def _dot_nt(a, b):
    return lax.dot_general(a, b, (((1,), (1,)), ((), ())), preferred_element_type=F32)


def _sigmoid(v):
    return 1.0 / (1.0 + jnp.exp(-v))


def _rms(v, g):
    return v * lax.rsqrt(jnp.mean(v * v, axis=-1, keepdims=True) + NORM_EPS) * g


def _modulate(v, g, scale, shift):
    return _rms(v, g) * (1.0 + scale) + shift


def _rope(v, c, sa, sb):
    return v * c + pltpu.roll(v, LANES - 8, 1) * sa + pltpu.roll(v, 8, 1) * sb


def _tile_rows(a, n):
    return jnp.concatenate([a] * n, axis=0) if n > 1 else a


def _tile_lanes(a, n):
    return jnp.concatenate([a] * n, axis=1) if n > 1 else a


def _flash_init(m_ref, l_ref, acc_ref):
    m_ref[...] = jnp.full(m_ref.shape, -1e30, F32)
    l_ref[...] = jnp.zeros(l_ref.shape, F32)
    acc_ref[...] = jnp.zeros(acc_ref.shape, F32)


def _flash_update(s, v, m_ref, l_ref, acc_ref, row0):
    n = s.shape[0]
    ps = []
    for r in range(0, n, Q_BLOCK):
        rs = slice(row0 + r, row0 + r + Q_BLOCK)
        sl = s[r:r + Q_BLOCK]
        m_old = m_ref[rs, :]
        m_new = jnp.maximum(m_old, jnp.max(sl, axis=1, keepdims=True))
        p = jnp.exp2(sl - _tile_lanes(m_new, sl.shape[1] // LANES))
        alpha = jnp.exp2(m_old - m_new)
        l_ref[rs, :] = alpha * l_ref[rs, :] + jnp.sum(p, axis=1, keepdims=True)
        acc_ref[rs, :] = alpha * acc_ref[rs, :]
        m_ref[rs, :] = m_new
        ps.append(p.astype(BF16))
    pv = _dot(jnp.concatenate(ps, axis=0), v)
    acc_ref[row0:row0 + n, :] += pv


def _ada_kernel(c_ref, w_ref, b_ref, o_ref):
    cv = c_ref[...]
    sc = cv * _sigmoid(cv)
    o_ref[0] = jnp.dot(sc, w_ref[0], precision=HIGHEST, preferred_element_type=F32) + b_ref[0]


def _ada(c, ada_w, ada_b):
    B, D = c.shape
    w = ada_w.reshape(4, D, 3 * D)
    b = ada_b.reshape(4, 1, 3 * D)
    return pl.pallas_call(
        _ada_kernel, grid=(4, 3),
        in_specs=[pl.BlockSpec((B, D), lambda l, j: (0, 0)),
                  pl.BlockSpec((1, D, D), lambda l, j: (l, 0, j)),
                  pl.BlockSpec((1, 1, D), lambda l, j: (l, 0, j))],
        out_specs=pl.BlockSpec((1, B, D), lambda l, j: (l, 0, j)),
        out_shape=jax.ShapeDtypeStruct((4, B, 3 * D), F32), name="ada")(c, w, b)


def _dsa_proj_kernel(x_ref, sh_ref, sc_ref, gn_ref, win_ref, gq_ref, gkv_ref, wuq_ref, wiq_ref,
                     wcat_ref, rc_ref, ra_ref, rb_ref, qcat_ref, iq_ref, kcat_ref, ik_ref, iw_ref):
    h = _modulate(x_ref[...], gn_ref[...], sc_ref[0], sh_ref[0])
    proj = _dot(h.astype(BF16), win_ref[...])
    q_lat = _rms(proj[:, :256], gq_ref[...]).astype(BF16)
    c_kv = _rms(proj[:, 256:384], gkv_ref[...])
    rc, ra, rb = rc_ref[...], ra_ref[...], rb_ref[...]
    rest = _rope(proj[:, 384:512], rc, ra, rb)
    lane = lax.broadcasted_iota(jnp.int32, rest.shape, 1)
    kcat_ref[:, :128] = c_kv.astype(BF16)
    kcat_ref[:, 128:] = jnp.where((lane >= 64) & (lane < 80), rest, 0.0).astype(BF16)
    ik_ref[...] = jnp.where(lane < 64, rest, 0.0).astype(BF16)
    iw_ref[...] = rest * (IDX_HEADS ** -0.5)
    q = _dot(q_lat, wuq_ref[...])
    for p in range(DSA_HEADS // 2):
        qp = (_rope(q[:, p * 128:(p + 1) * 128], rc, ra, rb) * ATT_SCALE).astype(BF16)
        res = _dot(qp, wcat_ref[p]).astype(BF16)
        qcat_ref[2 * p] = res[:, :256]
        qcat_ref[2 * p + 1] = res[:, 256:]
    iqv = _dot(q_lat, wiq_ref[...])
    for hh in range(IDX_HEADS):
        iq_ref[hh] = (_rope(iqv[:, hh * 128:(hh + 1) * 128], rc, ra, rb) * (IDX_DIM ** -0.5)).astype(BF16)


def _dsa_proj(xf, shift, scale, gn, win, gq, gkv, wuq, wiq, wcat, rope, S):
    T, D = xf.shape
    tm = TOKEN_TILE
    nps = S // tm
    row = lambda i: (i, 0)
    bat = lambda i: (i // nps, 0, 0)
    pos = lambda i: (i % nps, 0)
    cst2 = lambda i: (0, 0)
    cst3 = lambda i: (0, 0, 0)
    return pl.pallas_call(
        _dsa_proj_kernel, grid=(T // tm,),
        in_specs=[pl.BlockSpec((tm, D), row),
                  pl.BlockSpec((1, 1, D), bat), pl.BlockSpec((1, 1, D), bat),
                  pl.BlockSpec((1, D), cst2),
                  pl.BlockSpec(win.shape, cst2),
                  pl.BlockSpec((1, DSA_Q_LORA), cst2), pl.BlockSpec((1, DSA_KV_LORA), cst2),
                  pl.BlockSpec(wuq.shape, cst2), pl.BlockSpec(wiq.shape, cst2),
                  pl.BlockSpec(wcat.shape, cst3),
                  pl.BlockSpec((tm, LANES), pos), pl.BlockSpec((tm, LANES), pos),
                  pl.BlockSpec((tm, LANES), pos)],
        out_specs=[pl.BlockSpec((DSA_HEADS, tm, 256), lambda i: (0, i, 0)),
                   pl.BlockSpec((IDX_HEADS, tm, LANES), lambda i: (0, i, 0)),
                   pl.BlockSpec((tm, 256), row),
                   pl.BlockSpec((tm, LANES), row),
                   pl.BlockSpec((tm, LANES), row)],
        out_shape=[jax.ShapeDtypeStruct((DSA_HEADS, T, 256), BF16),
                   jax.ShapeDtypeStruct((IDX_HEADS, T, LANES), BF16),
                   jax.ShapeDtypeStruct((T, 256), BF16),
                   jax.ShapeDtypeStruct((T, LANES), BF16),
                   jax.ShapeDtypeStruct((T, LANES), F32)],
        name="dsa_proj")(xf, shift, scale, gn, win, gq, gkv, wuq, wiq, wcat, *rope)


def _dsa_attn_kernel(q_ref, iq_ref, iw_ref, kcat_ref, ik_ref, o_ref, keys_ref, w_ref, m_ref, l_ref, acc_ref,
                     *, k_sel, idx_bits):
    QB, KC, H = Q_BLOCK, DSA_KC, DSA_HEADS
    qi = pl.program_id(1)
    n_ch = (qi * QB) // KC + 1
    row_t = qi * QB + lax.broadcasted_iota(jnp.int32, (QB, 1), 0)
    iw = iw_ref[...]
    for hh in range(IDX_HEADS):
        w_ref[hh * QB:(hh + 1) * QB, :] = jnp.broadcast_to(iw[:, 80 + hh:81 + hh], (QB, KC))
    iq_all = iq_ref[...].reshape(IDX_HEADS * QB, LANES)

    def score_chunk(c, carry):
        k0 = pl.multiple_of(c * KC, KC)
        ikc = ik_ref[0, pl.ds(k0, KC), :]
        r = jnp.maximum(_dot_nt(iq_all, ikc), 0.0) * w_ref[...]
        sc = r[0:QB]
        for hh in range(1, IDX_HEADS):
            sc = sc + r[hh * QB:(hh + 1) * QB]
        sc = sc + 0.0
        bits = pltpu.bitcast(sc, jnp.int32)
        key = jnp.where(bits < 0, bits ^ 0x7FFFFFFF, bits)
        pos = k0 + lax.broadcasted_iota(jnp.int32, (QB, KC), 1)
        keys_ref[c] = jnp.where(pos <= row_t, key, INT_MIN)
        return carry
    lax.fori_loop(0, n_ch, score_chunk, 0)

    def count(pred):
        def body(c, a):
            kk = keys_ref[c]
            pos = c * KC + lax.broadcasted_iota(jnp.int32, (QB, KC), 1)
            m = jnp.where(pred(kk, pos), 1.0, 0.0)
            for j in range(KC // LANES):
                a = a + m[:, j * LANES:(j + 1) * LANES]
            return a
        a = lax.fori_loop(0, n_ch, body, jnp.zeros((QB, LANES), F32))
        return jnp.sum(a, axis=1, keepdims=True)

    def bit_body(it, thr):
        cand = thr ^ jnp.left_shift(jnp.int32(1), 31 - it)
        return jnp.where(count(lambda kk, pos: kk >= cand) >= k_sel, cand, thr)
    thr = lax.fori_loop(0, 32, bit_body, jnp.full((QB, 1), INT_MIN, jnp.int32))

    n_gt = count(lambda kk, pos: kk > thr)
    n_ge = count(lambda kk, pos: kk >= thr)
    tie = jnp.where((n_ge > k_sel) & (thr > INT_MIN), 1.0, 0.0)

    @pl.when(jnp.max(tie) > 0.0)
    def _():
        need = k_sel - n_gt
        def jbit(it, jcut):
            cand = jcut | jnp.left_shift(jnp.int32(1), idx_bits - 1 - it)
            f = count(lambda kk, pos: (kk == thr) & (pos < cand))
            return jnp.where(f <= need, cand, jcut)
        jcut = lax.fori_loop(0, idx_bits, jbit, jnp.zeros((QB, 1), jnp.int32))
        def drop(c, carry):
            kk = keys_ref[c]
            pos = c * KC + lax.broadcasted_iota(jnp.int32, (QB, KC), 1)
            keys_ref[c] = jnp.where((kk == thr) & (pos >= jcut), INT_MIN, kk)
            return carry
        lax.fori_loop(0, n_ch, drop, 0)

    thr_eff = jnp.maximum(thr, INT_MIN + 1)
    _flash_init(m_ref, l_ref, acc_ref)
    q_all = q_ref[...].reshape(H * QB, 256)

    def chunk(c, carry):
        k0 = pl.multiple_of(c * KC, KC)
        kc = kcat_ref[0, pl.ds(k0, KC), :]
        bias = jnp.where(keys_ref[c] >= thr_eff, 0.0, -jnp.inf)
        s = _dot_nt(q_all, kc) + _tile_rows(bias, H)
        _flash_update(s, kc[:, :DSA_KV_LORA], m_ref, l_ref, acc_ref, 0)
        return carry
    lax.fori_loop(0, n_ch, chunk, 0)
    o = acc_ref[...] / jnp.maximum(l_ref[...], 1e-30)
    for h in range(H):
        o_ref[:, h * 128:(h + 1) * 128] = o[h * QB:(h + 1) * QB].astype(BF16)


def _dsa_attn(qcat, iq, iw, kcat, ik, B, S):
    T = B * S
    nq = S // Q_BLOCK
    row = lambda b, q: (b * nq + q, 0)
    bat = lambda b, q: (b, 0, 0)
    k_sel = min(DSA_TOPK, S // 4)
    kern = functools.partial(_dsa_attn_kernel, k_sel=k_sel, idx_bits=int(S).bit_length())
    return pl.pallas_call(
        kern, grid=(B, nq),
        in_specs=[pl.BlockSpec((DSA_HEADS, Q_BLOCK, 256), lambda b, q: (0, b * nq + q, 0)),
                  pl.BlockSpec((IDX_HEADS, Q_BLOCK, LANES), lambda b, q: (0, b * nq + q, 0)),
                  pl.BlockSpec((Q_BLOCK, LANES), row),
                  pl.BlockSpec((1, S, 256), bat),
                  pl.BlockSpec((1, S, LANES), bat)],
        out_specs=pl.BlockSpec((Q_BLOCK, DSA_HEADS * DSA_KV_LORA), row),
        out_shape=jax.ShapeDtypeStruct((T, DSA_HEADS * DSA_KV_LORA), BF16),
        scratch_shapes=[pltpu.VMEM((S // DSA_KC, Q_BLOCK, DSA_KC), jnp.int32),
                        pltpu.VMEM((IDX_HEADS * Q_BLOCK, DSA_KC), F32),
                        pltpu.VMEM((DSA_HEADS * Q_BLOCK, LANES), F32),
                        pltpu.VMEM((DSA_HEADS * Q_BLOCK, LANES), F32),
                        pltpu.VMEM((DSA_HEADS * Q_BLOCK, DSA_KV_LORA), F32)],
        name="dsa_attn")(qcat, iq, iw, kcat.reshape(B, S, 256), ik.reshape(B, S, LANES))


def _wuvo_kernel(uv_ref, wo_ref, o_ref):
    o_ref[0] = jnp.dot(uv_ref[0], wo_ref[...], precision=HIGHEST,
                       preferred_element_type=F32).astype(BF16)


def _wuvo(w_uv, w_o):
    H, C, V = w_uv.shape
    D = w_o.shape[1]
    out = pl.pallas_call(
        _wuvo_kernel, grid=(H,),
        in_specs=[pl.BlockSpec((1, C, V), lambda h: (h, 0, 0)),
                  pl.BlockSpec((V, D), lambda h: (h, 0))],
        out_specs=pl.BlockSpec((1, C, D), lambda h: (h, 0, 0)),
        out_shape=jax.ShapeDtypeStruct((H, C, D), BF16), name="wuvo")(w_uv, w_o)
    return out.reshape(H * C, D)


def _post_kernel(a_ref, w_ref, x_ref, gate_ref, gn_ref, sc_ref, sh_ref, xo_ref, ho_ref):
    x1 = x_ref[...] + gate_ref[0] * _dot(a_ref[...], w_ref[...])
    xo_ref[...] = x1
    ho_ref[...] = _modulate(x1, gn_ref[...], sc_ref[0], sh_ref[0]).astype(ho_ref.dtype)


def _post(a, w, xf, gate, gn, scale, shift, S, h_dtype):
    T, D = xf.shape
    tm = TOKEN_TILE
    nps = S // tm
    row = lambda i: (i, 0)
    bat = lambda i: (i // nps, 0, 0)
    cst2 = lambda i: (0, 0)
    return pl.pallas_call(
        _post_kernel, grid=(T // tm,),
        in_specs=[pl.BlockSpec((tm, a.shape[1]), row), pl.BlockSpec(w.shape, cst2),
                  pl.BlockSpec((tm, D), row), pl.BlockSpec((1, 1, D), bat),
                  pl.BlockSpec((1, D), cst2), pl.BlockSpec((1, 1, D), bat), pl.BlockSpec((1, 1, D), bat)],
        out_specs=[pl.BlockSpec((tm, D), row), pl.BlockSpec((tm, D), row)],
        out_shape=[jax.ShapeDtypeStruct((T, D), F32), jax.ShapeDtypeStruct((T, D), h_dtype)],
        name="post")(a, w, xf, gate, gn, scale, shift)


def _ffn_kernel(h_ref, x_ref, gate_ref, w1_ref, w3_ref, w2_ref, xo_ref, acc_ref):
    j = pl.program_id(1)

    @pl.when(j == 0)
    def _():
        acc_ref[...] = jnp.zeros_like(acc_ref)
    hb = h_ref[...]
    a = _dot(hb, w1_ref[...])
    b = _dot(hb, w3_ref[...])
    acc_ref[...] += _dot((a * _sigmoid(a) * b).astype(BF16), w2_ref[...])

    @pl.when(j == pl.num_programs(1) - 1)
    def _():
        xo_ref[...] = x_ref[...] + gate_ref[0] * acc_ref[...]


def _ffn(h, xf, gate, w1, w3, w2, S, tn):
    T, D = xf.shape
    F = w1.shape[1]
    tm = TOKEN_TILE
    nps = S // tm
    return pl.pallas_call(
        _ffn_kernel, grid=(T // tm, F // tn),
        in_specs=[pl.BlockSpec((tm, D), lambda i, j: (i, 0)),
                  pl.BlockSpec((tm, D), lambda i, j: (i, 0)),
                  pl.BlockSpec((1, 1, D), lambda i, j: (i // nps, 0, 0)),
                  pl.BlockSpec((D, tn), lambda i, j: (0, j)),
                  pl.BlockSpec((D, tn), lambda i, j: (0, j)),
                  pl.BlockSpec((tn, D), lambda i, j: (j, 0))],
        out_specs=pl.BlockSpec((tm, D), lambda i, j: (i, 0)),
        out_shape=jax.ShapeDtypeStruct((T, D), F32),
        scratch_shapes=[pltpu.VMEM((tm, D), F32)],
        name="ffn")(h, xf, gate, w1, w3, w2)


def _nsa_proj_kernel(x_ref, sh_ref, sc_ref, gn_ref, wq_ref, wkv_ref, wg_ref, rc_ref, ra_ref, rb_ref,
                     q_ref, kc_ref, vc_ref, ks_ref, vs_ref, kw_ref, vw_ref, g_ref):
    hb = _modulate(x_ref[...], gn_ref[...], sc_ref[0], sh_ref[0]).astype(BF16)
    rc, ra, rb = rc_ref[...], ra_ref[...], rb_ref[...]

    def roped(v):
        return jnp.concatenate([_rope(v[:, :128], rc, ra, rb), _rope(v[:, 128:], rc, ra, rb)], axis=1)

    for p in range(NSA_HEADS // 2):
        qv = (roped(_dot(hb, wq_ref[:, p * 256:(p + 1) * 256])) * ATT_SCALE).astype(BF16)
        q_ref[2 * p] = qv[:, :128]
        q_ref[2 * p + 1] = qv[:, 128:]
    outs = (kc_ref, vc_ref, ks_ref, vs_ref, kw_ref, vw_ref)
    for n, o_ref in enumerate(outs):
        v = _dot(hb, wkv_ref[:, n * 256:(n + 1) * 256])
        o_ref[...] = (roped(v) if n % 2 == 0 else v).astype(BF16)
    g_ref[...] = _sigmoid(_dot(hb, wg_ref[...]))


def _nsa_proj(xf, shift, scale, gn, wq, wkv, wg, rope, S):
    T, D = xf.shape
    tm = TOKEN_TILE
    nps = S // tm
    row = lambda i: (i, 0)
    bat = lambda i: (i // nps, 0, 0)
    pos = lambda i: (i % nps, 0)
    cst2 = lambda i: (0, 0)
    kv_spec = pl.BlockSpec((tm, 256), row)
    kv_shape = jax.ShapeDtypeStruct((T, 256), BF16)
    return pl.pallas_call(
        _nsa_proj_kernel, grid=(T // tm,),
        in_specs=[pl.BlockSpec((tm, D), row),
                  pl.BlockSpec((1, 1, D), bat), pl.BlockSpec((1, 1, D), bat),
                  pl.BlockSpec((1, D), cst2),
                  pl.BlockSpec(wq.shape, cst2), pl.BlockSpec(wkv.shape, cst2), pl.BlockSpec(wg.shape, cst2),
                  pl.BlockSpec((tm, LANES), pos), pl.BlockSpec((tm, LANES), pos),
                  pl.BlockSpec((tm, LANES), pos)],
        out_specs=[pl.BlockSpec((NSA_HEADS, tm, LANES), lambda i: (0, i, 0))] + [kv_spec] * 6
                  + [pl.BlockSpec((tm, LANES), row)],
        out_shape=[jax.ShapeDtypeStruct((NSA_HEADS, T, LANES), BF16)] + [kv_shape] * 6
                  + [jax.ShapeDtypeStruct((T, LANES), F32)],
        name="nsa_proj")(xf, shift, scale, gn, wq, wkv, wg, *rope)


def _compress_kernel(uk_ref, uv_ref, pe_ref, k1_ref, k2_ref, v1_ref, v2_ref, ko_ref, vo_ref):
    half = CMP_STRIDE * HEAD_DIM
    pe = jnp.broadcast_to(pe_ref[...], (8, 2 * half)).astype(BF16)

    def comp(u_ref, w1_ref, w2_ref):
        bias = _dot(pe, w1_ref[...])[0:1]
        out = None
        for gg in range(2):
            u = u_ref[0, gg]
            a = _dot(u, w1_ref[:half, :])
            b = _dot(u, w1_ref[half:, :])
            nrow = b.shape[0]
            hid = a + pltpu.roll(b, nrow - 1, 0) + bias
            hid = (hid * _sigmoid(hid)).astype(BF16)
            o = _dot(hid, w2_ref[gg])
            out = o if out is None else out + o
        return out
    ko_ref[0] = comp(uk_ref, k1_ref, k2_ref).astype(BF16)
    vo_ref[0] = comp(uv_ref, v1_ref, v2_ref).astype(BF16)


def _compress(uk, uv, pe, k1, k2, v1, v2):
    B, G, nch, W = uk.shape
    u_spec = pl.BlockSpec((1, 2, nch, W), lambda b, p: (b, p, 0, 0))
    w1_spec = pl.BlockSpec(k1.shape, lambda b, p: (0, 0))
    w2_spec = pl.BlockSpec(k2.shape, lambda b, p: (0, 0, 0))
    o_spec = pl.BlockSpec((1, nch, LANES), lambda b, p: (b, 0, p))
    o_shape = jax.ShapeDtypeStruct((B, nch, G * HEAD_DIM), BF16)
    return pl.pallas_call(
        _compress_kernel, grid=(B, G // 2),
        in_specs=[u_spec, u_spec, pl.BlockSpec(pe.shape, lambda b, p: (0, 0)),
                  w1_spec, w2_spec, w1_spec, w2_spec],
        out_specs=[o_spec, o_spec], out_shape=[o_shape, o_shape],
        name="compress")(uk, uv, pe, k1, k2, v1, v2)


def _nsa_attn_kernel(q_ref, g_ref, kcmp_ref, vcmp_ref, ks_ref, vs_ref, kw_ref, vw_ref,
                     covt_ref, exp_ref, o_ref, oc_ref, psum_ref, imp_ref, sel_ref, m_ref, l_ref, acc_ref,
                     osel_ref, *, n_sel):
    QB, KC, HPG = Q_BLOCK, NSA_KC, NSA_HPG
    GR = HPG * QB
    qi = pl.program_id(1)
    qs = qi * QB
    row_t = qs + lax.broadcasted_iota(jnp.int32, (QB, 1), 0)
    nch = kcmp_ref.shape[1]
    nb = covt_ref.shape[0]
    n_ch = qs // KC + 1
    gates = g_ref[...]
    pair = lambda g: slice((g // 2) * 128, (g // 2) * 128 + 128)
    rows = lambda g: slice(g * GR, (g + 1) * GR)
    q_grp = lambda g: q_ref[g * HPG:(g + 1) * HPG].reshape(GR, LANES)

    cmp_end = lax.broadcasted_iota(jnp.int32, (QB, nch), 1) * CMP_STRIDE + (CMP_LEN - 1)
    cbias = _tile_rows(jnp.where(cmp_end <= row_t, 0.0, -jnp.inf), HPG)
    for g in range(NSA_GROUPS):
        s = _dot_nt(q_grp(g), kcmp_ref[0, :, pair(g)]) + cbias
        m = jnp.max(s, axis=1, keepdims=True)
        m = jnp.where(m == -jnp.inf, 0.0, m)
        e = jnp.exp2(s - m)
        p = e / jnp.maximum(jnp.sum(e, axis=1, keepdims=True), 1e-30)
        oc_ref[rows(g), :] = _dot(p.astype(BF16), vcmp_ref[0, :, pair(g)])
        psum_ref[g] = p[0:QB] + p[QB:2 * QB] + p[2 * QB:3 * QB] + p[3 * QB:4 * QB]

    jb = lax.broadcasted_iota(jnp.int32, (nb, QB), 0)
    cur = (qs + lax.broadcasted_iota(jnp.int32, (nb, QB), 1)) // SEL_LEN
    forced = (jb == 0) | (jb == cur) | (jb == cur - 1)
    ri = lax.broadcasted_iota(jnp.int32, (QB, QB), 0)
    ci = lax.broadcasted_iota(jnp.int32, (QB, QB), 1)
    eye = jnp.where(ri == ci, 1.0, 0.0).astype(BF16)
    for g in range(NSA_GROUPS):
        imp = lax.dot_general(covt_ref[...], psum_ref[g], (((1,), (1,)), ((), ())),
                              precision=HIGHEST, preferred_element_type=F32)
        imp = jnp.where(forced, FORCE_SCORE, imp)
        imp = jnp.where(jb <= cur, imp, -jnp.inf)
        imp_ref[...] = imp

        def rank_body(i, rank, imp=imp):
            ri_ = imp_ref[pl.ds(i, 1), :]
            ahead = (ri_ > imp) | ((ri_ == imp) & (i < jb))
            return rank + jnp.where(ahead, 1.0, 0.0)
        n_live = jnp.minimum((qs + QB - 1) // SEL_LEN + 1, nb)
        rank = lax.fori_loop(0, n_live, rank_body, jnp.zeros((nb, QB), F32))
        selt = jnp.where(rank < n_sel, 1.0, 0.0).astype(BF16)
        sel_ref[g] = _dot_nt(eye, selt).astype(BF16)

    _flash_init(m_ref, l_ref, acc_ref)

    def sel_chunk(c, carry):
        k0 = pl.multiple_of(c * KC, KC)
        causal = (k0 + lax.broadcasted_iota(jnp.int32, (QB, KC), 1)) <= row_t
        for g in range(NSA_GROUPS):
            keep = (_dot(sel_ref[g], exp_ref[c]) > 0.5) & causal
            bias = _tile_rows(jnp.where(keep, 0.0, -jnp.inf), HPG)
            s = _dot_nt(q_grp(g), ks_ref[0, pl.ds(k0, KC), pair(g)]) + bias
            _flash_update(s, vs_ref[0, pl.ds(k0, KC), pair(g)], m_ref, l_ref, acc_ref, g * GR)
        return carry
    lax.fori_loop(0, n_ch, sel_chunk, 0)
    osel_ref[...] = acc_ref[...] / jnp.maximum(l_ref[...], 1e-30)

    wl = WINDOW + QB
    w0 = pl.multiple_of(jnp.maximum(qs - WINDOW, 0), QB)
    wpos = w0 + lax.broadcasted_iota(jnp.int32, (QB, wl), 1)
    wbias = _tile_rows(jnp.where((wpos <= row_t) & (wpos > row_t - WINDOW), 0.0, -jnp.inf), HPG)
    for g in range(NSA_GROUPS):
        s = _dot_nt(q_grp(g), kw_ref[0, pl.ds(w0, wl), pair(g)]) + wbias
        m = jnp.max(s, axis=1, keepdims=True)
        m = jnp.where(m == -jnp.inf, 0.0, m)
        e = jnp.exp2(s - m)
        den = jnp.maximum(jnp.sum(e, axis=1, keepdims=True), 1e-30)
        acc_ref[rows(g), :] = _dot(e.astype(BF16), vw_ref[0, pl.ds(w0, wl), pair(g)]) / den

    lane = lax.broadcasted_iota(jnp.int32, (QB, LANES), 1)
    pair_out = [None, None]
    for h in range(NSA_HEADS):
        g = h // HPG
        hr = slice(h * QB, (h + 1) * QB)
        o = (gates[:, 3 * h:3 * h + 1] * oc_ref[hr, :] + gates[:, 3 * h + 1:3 * h + 2] * osel_ref[hr, :]
             + gates[:, 3 * h + 2:3 * h + 3] * acc_ref[hr, :])
        pair_out[h % 2] = o
        if h % 2 == 1:
            if g % 2 == 0:
                both = jnp.where(lane < 64, pair_out[0], pltpu.roll(pair_out[1], 64, 1))
            else:
                both = jnp.where(lane < 64, pltpu.roll(pair_out[0], 64, 1), pair_out[1])
            o_ref[:, (h // 2) * 128:(h // 2) * 128 + 128] = both.astype(BF16)


def _nsa_attn(q, gates, kcmp, vcmp, ks, vs, kw, vw, covt, expand, B, S):
    T = B * S
    nq = S // Q_BLOCK
    nch = kcmp.shape[1]
    nb = S // SEL_LEN
    row = lambda b, i: (b * nq + i, 0)
    bat = lambda b, i: (b, 0, 0)
    kv = lambda a: a.reshape(B, S, 256)
    kv_spec = pl.BlockSpec((1, S, 256), bat)
    cmp_spec = pl.BlockSpec((1, nch, 256), bat)
    kern = functools.partial(_nsa_attn_kernel, n_sel=min(SEL_BLOCKS, nb))
    return pl.pallas_call(
        kern, grid=(B, nq),
        in_specs=[pl.BlockSpec((NSA_HEADS, Q_BLOCK, LANES), lambda b, i: (0, b * nq + i, 0)),
                  pl.BlockSpec((Q_BLOCK, LANES), row),
                  cmp_spec, cmp_spec, kv_spec, kv_spec, kv_spec, kv_spec,
                  pl.BlockSpec(covt.shape, lambda b, i: (0, 0)),
                  pl.BlockSpec(expand.shape, lambda b, i: (0, 0, 0))],
        out_specs=pl.BlockSpec((Q_BLOCK, NSA_HEADS * HEAD_DIM), row),
        out_shape=jax.ShapeDtypeStruct((T, NSA_HEADS * HEAD_DIM), BF16),
        scratch_shapes=[pltpu.VMEM((NSA_HEADS * Q_BLOCK, LANES), F32),
                        pltpu.VMEM((NSA_GROUPS, Q_BLOCK, nch), F32),
                        pltpu.VMEM((nb, Q_BLOCK), F32),
                        pltpu.VMEM((NSA_GROUPS, Q_BLOCK, nb), BF16),
                        pltpu.VMEM((NSA_HEADS * Q_BLOCK, LANES), F32),
                        pltpu.VMEM((NSA_HEADS * Q_BLOCK, LANES), F32),
                        pltpu.VMEM((NSA_HEADS * Q_BLOCK, LANES), F32),
                        pltpu.VMEM((NSA_HEADS * Q_BLOCK, LANES), F32)],
        name="nsa_attn")(q, gates, kcmp, vcmp, kv(ks), kv(vs), kv(kw), kv(vw), covt, expand)


def _router_kernel(h_ref, wr_ref, route_ref, cnt_ref, carry_ref):
    i = pl.program_id(0)
    tm = h_ref.shape[0]

    @pl.when(i == 0)
    def _():
        carry_ref[...] = jnp.zeros_like(carry_ref)
    lane = lax.broadcasted_iota(jnp.int32, (tm, LANES), 1).astype(F32)
    lg = jnp.dot(h_ref[...], wr_ref[...], precision=HIGHEST, preferred_element_type=F32)
    lg = jnp.where(lane < N_EXPERTS, lg, -jnp.inf)
    v1 = jnp.max(lg, axis=1, keepdims=True)
    i1 = jnp.min(jnp.where(lg == v1, lane, float(LANES)), axis=1, keepdims=True)
    lg2 = jnp.where(lane == i1, -jnp.inf, lg)
    v2 = jnp.max(lg2, axis=1, keepdims=True)
    i2 = jnp.min(jnp.where(lg2 == v2, lane, float(LANES)), axis=1, keepdims=True)
    e2 = jnp.exp(v2 - v1)
    g1 = 1.0 / (1.0 + e2)
    g2 = e2 / (1.0 + e2)
    oh1 = jnp.where(lane == i1, 1.0, 0.0)
    oh2 = jnp.where(lane == i2, 1.0, 0.0)
    both = oh1 + oh2
    ri = lax.broadcasted_iota(jnp.int32, (tm, tm), 0)
    ci = lax.broadcasted_iota(jnp.int32, (tm, tm), 1)
    lower = jnp.where(ri > ci, 1.0, 0.0).astype(BF16)
    tot = carry_ref[0:1, :] + _dot(lower, both.astype(BF16))
    r1 = jnp.sum(oh1 * tot, axis=1, keepdims=True)
    r2 = jnp.sum(oh2 * tot, axis=1, keepdims=True)
    new_carry = carry_ref[...] + jnp.sum(both, axis=0, keepdims=True)
    carry_ref[...] = new_carry
    cnt_ref[...] = new_carry
    out = jnp.zeros((tm, LANES), F32)
    for col, val in enumerate((i1, i2, g1, g2, r1, r2)):
        out = jnp.where(lane == col, val, out)
    route_ref[...] = out


def _router(h, wr):
    T, D = h.shape
    tm = TOKEN_TILE
    return pl.pallas_call(
        _router_kernel, grid=(T // tm,),
        in_specs=[pl.BlockSpec((tm, D), lambda i: (i, 0)), pl.BlockSpec(wr.shape, lambda i: (0, 0))],
        out_specs=[pl.BlockSpec((tm, LANES), lambda i: (i, 0)), pl.BlockSpec((8, LANES), lambda i: (0, 0))],
        out_shape=[jax.ShapeDtypeStruct((T, LANES), F32), jax.ShapeDtypeStruct((8, LANES), F32)],
        scratch_shapes=[pltpu.VMEM((8, LANES), F32)],
        name="router")(h, wr)


def _row_copy(src, dst, sem):
    return pltpu.make_async_copy(src, dst, sem)


def _moe_scatter_kernel(dest_ref, h_ref, xs_in_ref, xs_ref, sem):
    del xs_in_ref
    i = pl.program_id(0)
    tm = h_ref.shape[0]

    def start(r, carry):
        for k in range(2):
            d = dest_ref[(i * tm + r) * 2 + k]
            _row_copy(h_ref.at[pl.ds(r, 1), :], xs_ref.at[pl.ds(d, 1), :], sem).start()
        return carry
    lax.fori_loop(0, tm, start, 0)

    def wait(r, carry):
        _row_copy(h_ref.at[pl.ds(0, 1), :], xs_ref.at[pl.ds(0, 1), :], sem).wait()
        return carry
    lax.fori_loop(0, 2 * tm, wait, 0)


def _moe_scatter(dest, h, xs0):
    T, D = h.shape
    tm = ROW_TILE
    return pl.pallas_call(
        _moe_scatter_kernel,
        grid_spec=pltpu.PrefetchScalarGridSpec(
            num_scalar_prefetch=1, grid=(T // tm,),
            in_specs=[pl.BlockSpec((tm, D), lambda i, d: (i, 0)), pl.BlockSpec(memory_space=pl.ANY)],
            out_specs=pl.BlockSpec(memory_space=pl.ANY),
            scratch_shapes=[pltpu.SemaphoreType.DMA(())]),
        out_shape=jax.ShapeDtypeStruct(xs0.shape, xs0.dtype),
        input_output_aliases={2: 0}, name="moe_scatter")(dest, h, xs0)


def _moe_ffn_kernel(be_ref, nu_ref, x_ref, w1_ref, w3_ref, w2_ref, y_ref, xb_ref, acc_ref):
    i = pl.program_id(0)
    j = pl.program_id(1)
    used = i < nu_ref[0]

    @pl.when(j == 0)
    def _():
        xb_ref[...] = x_ref[...].astype(BF16)
        acc_ref[...] = jnp.zeros_like(acc_ref)

    @pl.when(used)
    def _():
        xb = xb_ref[...]
        a = _dot(xb, w1_ref[0])
        b = _dot(xb, w3_ref[0])
        acc_ref[...] += _dot((a * _sigmoid(a) * b).astype(BF16), w2_ref[0])

    @pl.when(j == pl.num_programs(1) - 1)
    def _():
        y_ref[...] = acc_ref[...]


def _moe_ffn(block_e, n_used, xs, w1, w3, w2, tn):
    NS, D = xs.shape
    E, _, F = w1.shape
    nj = F // tn
    jj = lambda i, j, nu: jnp.where(i < nu[0], j, nj - 1)
    return pl.pallas_call(
        _moe_ffn_kernel,
        grid_spec=pltpu.PrefetchScalarGridSpec(
            num_scalar_prefetch=2, grid=(NS // MOE_BLOCK, nj),
            in_specs=[pl.BlockSpec((MOE_BLOCK, D), lambda i, j, be, nu: (i, 0)),
                      pl.BlockSpec((1, D, tn), lambda i, j, be, nu: (be[i], 0, jj(i, j, nu))),
                      pl.BlockSpec((1, D, tn), lambda i, j, be, nu: (be[i], 0, jj(i, j, nu))),
                      pl.BlockSpec((1, tn, D), lambda i, j, be, nu: (be[i], jj(i, j, nu), 0))],
            out_specs=pl.BlockSpec((MOE_BLOCK, D), lambda i, j, be, nu: (i, 0)),
            scratch_shapes=[pltpu.VMEM((MOE_BLOCK, D), BF16), pltpu.VMEM((MOE_BLOCK, D), F32)]),
        out_shape=jax.ShapeDtypeStruct((NS, D), F32), name="moe_ffn")(block_e, n_used, xs, w1, w3, w2)


def _moe_combine_kernel(dest_ref, ys_ref, x_ref, gate_ref, route_ref, fn_ref, o_ref, ybuf_ref, sem):
    i = pl.program_id(0)
    tm = x_ref.shape[0]

    def start(r, carry):
        for k in range(2):
            d = dest_ref[(i * tm + r) * 2 + k]
            _row_copy(ys_ref.at[pl.ds(d, 1), :], ybuf_ref.at[k, pl.ds(r, 1), :], sem).start()
        return carry
    lax.fori_loop(0, tm, start, 0)

    def wait(r, carry):
        _row_copy(ys_ref.at[pl.ds(0, 1), :], ybuf_ref.at[0, pl.ds(0, 1), :], sem).wait()
        return carry
    lax.fori_loop(0, 2 * tm, wait, 0)
    route = route_ref[...]
    y = route[:, 2:3] * ybuf_ref[0] + route[:, 3:4] * ybuf_ref[1]
    o_ref[...] = _rms(x_ref[...] + gate_ref[0] * y, fn_ref[...])


def _moe_combine(dest, ys, xf, gate, route, fn, S):
    T, D = xf.shape
    tm = ROW_TILE
    nps = S // tm
    return pl.pallas_call(
        _moe_combine_kernel,
        grid_spec=pltpu.PrefetchScalarGridSpec(
            num_scalar_prefetch=1, grid=(T // tm,),
            in_specs=[pl.BlockSpec(memory_space=pl.ANY),
                      pl.BlockSpec((tm, D), lambda i, d: (i, 0)),
                      pl.BlockSpec((1, 1, D), lambda i, d: (i // nps, 0, 0)),
                      pl.BlockSpec((tm, LANES), lambda i, d: (i, 0)),
                      pl.BlockSpec((1, D), lambda i, d: (0, 0))],
            out_specs=pl.BlockSpec((tm, D), lambda i, d: (i, 0)),
            scratch_shapes=[pltpu.VMEM((2, tm, D), F32), pltpu.SemaphoreType.DMA(())]),
        out_shape=jax.ShapeDtypeStruct((T, D), F32), name="moe_combine")(dest, ys, xf, gate, route, fn)


def _rope_tables(S):
    inv = ROPE_THETA ** (-jnp.arange(0, ROPE_DIM, 2, dtype=F32) / ROPE_DIM)
    ang = jnp.arange(S, dtype=F32)[:, None] * inv[None, :]
    cos, sin = jnp.cos(ang), jnp.sin(ang)
    pm = np.arange(LANES) % HEAD_DIM
    col = pm % (ROPE_DIM // 2)
    rc = jnp.where((pm < ROPE_DIM)[None, :], cos[:, col], 1.0)
    ra = jnp.where((pm < ROPE_DIM // 2)[None, :], -sin[:, col], 0.0)
    rb = jnp.where(((pm >= ROPE_DIM // 2) & (pm < ROPE_DIM))[None, :], sin[:, col], 0.0)
    return rc, ra, rb


def _dsa_weights(w_in, w_uk, w_iq):
    D = w_in.shape[0]
    a, b, c, d = DSA_Q_LORA, DSA_Q_LORA + DSA_KV_LORA, DSA_Q_LORA + DSA_KV_LORA + ROPE_DIM, \
        DSA_Q_LORA + DSA_KV_LORA + ROPE_DIM + IDX_DIM
    win = jnp.concatenate([w_in[:, :b], w_in[:, c:d], w_in[:, b:c], w_in[:, d:],
                           jnp.zeros((D, 512 - w_in.shape[1]), F32)], axis=1).astype(BF16)
    H = DSA_HEADS
    blk = jnp.zeros((H, HEAD_DIM, 256), F32)
    blk = blk.at[:, ROPE_DIM:, :DSA_KV_LORA].set(jnp.transpose(w_uk, (0, 2, 1)))
    blk = blk.at[:, :ROPE_DIM, 192:192 + ROPE_DIM].set(jnp.eye(ROPE_DIM, dtype=F32))
    z = jnp.zeros((H // 2, HEAD_DIM, 256), F32)
    wcat = jnp.concatenate([jnp.concatenate([blk[0::2], z], axis=2),
                            jnp.concatenate([z, blk[1::2]], axis=2)], axis=1).astype(BF16)
    wiq = w_iq.reshape(DSA_Q_LORA, IDX_HEADS, IDX_DIM)
    wiq = jnp.concatenate([wiq, jnp.zeros_like(wiq)], axis=2).reshape(DSA_Q_LORA, IDX_HEADS * 128)
    return win, wcat, wiq.astype(BF16)


def _nsa_weights(w_in):
    D = w_in.shape[0]
    nq = NSA_HEADS * HEAD_DIM
    wq = w_in[:, :nq].reshape(D, NSA_HEADS, HEAD_DIM)
    z = jnp.zeros_like(wq)
    odd = ((np.arange(NSA_HEADS) // NSA_HPG) % 2 == 1)[None, :, None]
    wq = jnp.concatenate([jnp.where(odd, z, wq), jnp.where(odd, wq, z)], axis=2).reshape(D, NSA_HEADS * 128)
    wkv = w_in[:, nq:nq + 6 * 256]
    wg = jnp.concatenate([w_in[:, nq + 6 * 256:], jnp.zeros((D, LANES - 3 * NSA_HEADS), F32)], axis=1)
    return wq.astype(BF16), wkv.astype(BF16), wg.astype(BF16)


def _nsa_tables(S):
    nch = S // CMP_STRIDE
    nc = (S - CMP_LEN) // CMP_STRIDE + 1
    nb = S // SEL_LEN
    cstart = np.arange(nch) * CMP_STRIDE
    bstart = np.arange(nb) * SEL_LEN
    cov = ((cstart[None, :] < bstart[:, None] + SEL_LEN) & (cstart[None, :] + CMP_LEN > bstart[:, None])
           & (np.arange(nch)[None, :] < nc)).astype(np.float32)
    kpos = np.arange(S).reshape(S // NSA_KC, 1, NSA_KC)
    expand = (kpos // SEL_LEN == np.arange(nb)[None, :, None]).astype(np.float32)
    return jnp.asarray(cov), jnp.asarray(expand, dtype=BF16)


def _chunk_tokens(a, B, S):
    a = a.reshape(B, S // CMP_STRIDE, CMP_STRIDE, NSA_GROUPS, HEAD_DIM)
    return jnp.transpose(a, (0, 3, 1, 2, 4)).reshape(B, NSA_GROUPS, S // CMP_STRIDE, CMP_STRIDE * HEAD_DIM)


def kernel(x, c, norm_mix, norm_ffn, ada_w, ada_b, final_norm, dsa_w_in, dsa_g_q, dsa_w_uq, dsa_g_kv,
           dsa_w_uk, dsa_w_uv, dsa_w_iq, dsa_w_o, ffn_w1, ffn_w3, ffn_w2, nsa_w_in, nsa_cmp_pe,
           nsa_cmp_k1, nsa_cmp_k2, nsa_cmp_v1, nsa_cmp_v2, nsa_w_o, moe_router, moe_w1, moe_w3, moe_w2):
    B, S, D = x.shape
    T = B * S
    xf = x.reshape(T, D)
    mods = _ada(c, ada_w, ada_b).reshape(4, B, 3, 1, D)
    shift = lambda s: mods[s, :, 0]
    scale = lambda s: mods[s, :, 1]
    gate = lambda s: mods[s, :, 2]
    rope = _rope_tables(S)

    win, wcat, wiq = _dsa_weights(dsa_w_in[0], dsa_w_uk[0], dsa_w_iq[0])
    qcat, iq, kcat, ik, iw = _dsa_proj(
        xf, shift(0), scale(0), norm_mix[0:1], win, dsa_g_q[0:1], dsa_g_kv[0:1],
        dsa_w_uq[0].astype(BF16), wiq, wcat, rope, S)
    olat = _dsa_attn(qcat, iq, iw, kcat, ik, B, S)
    wuvo = _wuvo(dsa_w_uv[0], dsa_w_o[0])
    x1, h1 = _post(olat, wuvo, xf, gate(0), norm_ffn[0:1], scale(1), shift(1), S, BF16)
    x2 = _ffn(h1, x1, gate(1), ffn_w1[0].astype(BF16), ffn_w3[0].astype(BF16), ffn_w2[0].astype(BF16),
              S, ffn_w1.shape[2] // 2)

    wq, wkv, wg = _nsa_weights(nsa_w_in[0])
    q, kc, vc, ks, vs, kw, vw, gates = _nsa_proj(xf=x2, shift=shift(2), scale=scale(2), gn=norm_mix[1:2],
                                                 wq=wq, wkv=wkv, wg=wg, rope=rope, S=S)
    zpad = jnp.zeros((CMP_HIDDEN, HEAD_DIM), F32)
    pad2 = lambda w2: jnp.stack([jnp.concatenate([w2, zpad], axis=1),
                                 jnp.concatenate([zpad, w2], axis=1)]).astype(BF16)
    kcmp, vcmp = _compress(_chunk_tokens(kc, B, S), _chunk_tokens(vc, B, S),
                           nsa_cmp_pe[0].reshape(1, CMP_LEN * HEAD_DIM),
                           nsa_cmp_k1[0].astype(BF16), pad2(nsa_cmp_k2[0]),
                           nsa_cmp_v1[0].astype(BF16), pad2(nsa_cmp_v2[0]))
    covt, expand = _nsa_tables(S)
    o = _nsa_attn(q, gates, kcmp, vcmp, ks, vs, kw, vw, covt, expand, B, S)
    x3, h3 = _post(o, nsa_w_o[0].astype(BF16), x2, gate(2), norm_ffn[1:2], scale(3), shift(3), S, F32)

    wr = jnp.concatenate([moe_router[0], jnp.zeros((D, LANES - N_EXPERTS), F32)], axis=1)
    route, cnt = _router(h3, wr)
    counts = cnt[0, :N_EXPERTS].astype(jnp.int32)
    padded = (counts + MOE_BLOCK - 1) // MOE_BLOCK * MOE_BLOCK
    ends = jnp.cumsum(padded)
    pstart = ends - padded
    eidx = route[:, 0:2].astype(jnp.int32)
    dest = (pstart[eidx] + route[:, 4:6].astype(jnp.int32)).reshape(-1)
    n_blocks = -(-(T * 2) // MOE_BLOCK) + N_EXPERTS
    block_start = jnp.arange(n_blocks, dtype=jnp.int32) * MOE_BLOCK
    block_e = jnp.minimum(jnp.sum((ends[None, :] <= block_start[:, None]).astype(jnp.int32), axis=1),
                          N_EXPERTS - 1)
    n_used = (ends[-1:] // MOE_BLOCK).astype(jnp.int32)
    xs = _moe_scatter(dest, h3, jnp.zeros((n_blocks * MOE_BLOCK, D), F32))
    ys = _moe_ffn(block_e, n_used, xs, moe_w1[0].astype(BF16), moe_w3[0].astype(BF16),
                  moe_w2[0].astype(BF16), 512)
    out = _moe_combine(dest, ys, x3, gate(3), route, final_norm.reshape(1, D), S)
    return out.reshape(B, S, D)
```

```python
import functools

import numpy as np
import jax
import jax.numpy as jnp
from jax import lax
from jax.experimental import pallas as pl
from jax.experimental.pallas import tpu as pltpu

F32 = jnp.float32
BF16 = jnp.bfloat16
HIGHEST = lax.Precision.HIGHEST
INT_MIN = -2147483648

HEAD_DIM = 64
ROPE_DIM = 16
ROPE_THETA = 500000.0
Q_BLOCK = 128
NORM_EPS = 1e-6

DSA_HEADS = 16
DSA_NOPE = 48
DSA_Q_LORA = 256
DSA_KV_LORA = 128
IDX_HEADS = 8
IDX_DIM = 64
DSA_TOPK = 256
DSA_KC = 512

NSA_HEADS = 16
NSA_GROUPS = 4
NSA_HPG = 4
CMP_LEN = 32
CMP_STRIDE = 16
CMP_HIDDEN = 256
SEL_LEN = 64
SEL_BLOCKS = 16
WINDOW = 512
FORCE_SCORE = 1e4
NSA_KC = 512

N_EXPERTS = 8
MOE_BLOCK = 512
TOKEN_TILE = 512
ROW_TILE = 256
LANES = 128

LOG2E = 1.4426950408889634
ATT_SCALE = HEAD_DIM ** -0.5 * LOG2E


def _dot(a, b):
    return jnp.dot(a, b, preferred_element_type=F32)


def _dot_nt(a, b):
    return lax.dot_general(a, b, (((1,), (1,)), ((), ())), preferred_element_type=F32)


def _sigmoid(v):
    return 1.0 / (1.0 + jnp.exp(-v))


def _rms(v, g):
    return v * lax.rsqrt(jnp.mean(v * v, axis=-1, keepdims=True) + NORM_EPS) * g


def _modulate(v, g, scale, shift):
    return _rms(v, g) * (1.0 + scale) + shift


def _rope(v, c, sa, sb):
    return v * c + pltpu.roll(v, LANES - 8, 1) * sa + pltpu.roll(v, 8, 1) * sb


def _tile_rows(a, n):
    return jnp.concatenate([a] * n, axis=0) if n > 1 else a


def _tile_lanes(a, n):
    return jnp.concatenate([a] * n, axis=1) if n > 1 else a


def _flash_init(m_ref, l_ref, acc_ref):
    m_ref[...] = jnp.full(m_ref.shape, -1e30, F32)
    l_ref[...] = jnp.zeros(l_ref.shape, F32)
    acc_ref[...] = jnp.zeros(acc_ref.shape, F32)


def _flash_update(s, v, m_ref, l_ref, acc_ref, row0):
    n = s.shape[0]
    ps = []
    for r in range(0, n, Q_BLOCK):
        rs = slice(row0 + r, row0 + r + Q_BLOCK)
        sl = s[r:r + Q_BLOCK]
        m_old = m_ref[rs, :]
        m_new = jnp.maximum(m_old, jnp.max(sl, axis=1, keepdims=True))
        p = jnp.exp2(sl - _tile_lanes(m_new, sl.shape[1] // LANES))
        alpha = jnp.exp2(m_old - m_new)
        l_ref[rs, :] = alpha * l_ref[rs, :] + jnp.sum(p, axis=1, keepdims=True)
        acc_ref[rs, :] = alpha * acc_ref[rs, :]
        m_ref[rs, :] = m_new
        ps.append(p.astype(BF16))
    pv = _dot(jnp.concatenate(ps, axis=0), v)
    acc_ref[row0:row0 + n, :] += pv


def _ada_kernel(c_ref, w_ref, b_ref, o_ref):
    cv = c_ref[...]
    sc = cv * _sigmoid(cv)
    o_ref[0] = jnp.dot(sc, w_ref[0], precision=HIGHEST, preferred_element_type=F32) + b_ref[0]


def _ada(c, ada_w, ada_b):
    B, D = c.shape
    w = ada_w.reshape(4, D, 3 * D)
    b = ada_b.reshape(4, 1, 3 * D)
    return pl.pallas_call(
        _ada_kernel, grid=(4, 3),
        in_specs=[pl.BlockSpec((B, D), lambda l, j: (0, 0)),
                  pl.BlockSpec((1, D, D), lambda l, j: (l, 0, j)),
                  pl.BlockSpec((1, 1, D), lambda l, j: (l, 0, j))],
        out_specs=pl.BlockSpec((1, B, D), lambda l, j: (l, 0, j)),
        out_shape=jax.ShapeDtypeStruct((4, B, 3 * D), F32), name="ada")(c, w, b)


def _dsa_proj_kernel(x_ref, sh_ref, sc_ref, gn_ref, win_ref, gq_ref, gkv_ref, wuq_ref, wiq_ref,
                     wcat_ref, rc_ref, ra_ref, rb_ref, qcat_ref, iq_ref, kcat_ref, ik_ref, iw_ref):
    h = _modulate(x_ref[...], gn_ref[...], sc_ref[0], sh_ref[0])
    proj = _dot(h.astype(BF16), win_ref[...])
    q_lat = _rms(proj[:, :256], gq_ref[...]).astype(BF16)
    c_kv = _rms(proj[:, 256:384], gkv_ref[...])
    rc, ra, rb = rc_ref[...], ra_ref[...], rb_ref[...]
    rest = _rope(proj[:, 384:512], rc, ra, rb)
    lane = lax.broadcasted_iota(jnp.int32, rest.shape, 1)
    kcat_ref[:, :128] = c_kv.astype(BF16)
    kcat_ref[:, 128:] = jnp.where((lane >= 64) & (lane < 80), rest, 0.0).astype(BF16)
    ik_ref[...] = jnp.where(lane < 64, rest, 0.0).astype(BF16)
    iw_ref[...] = rest * (IDX_HEADS ** -0.5)
    q = _dot(q_lat, wuq_ref[...])
    for p in range(DSA_HEADS // 2):
        qp = (_rope(q[:, p * 128:(p + 1) * 128], rc, ra, rb) * ATT_SCALE).astype(BF16)
        res = _dot(qp, wcat_ref[p]).astype(BF16)
        qcat_ref[2 * p] = res[:, :256]
        qcat_ref[2 * p + 1] = res[:, 256:]
    iqv = _dot(q_lat, wiq_ref[...])
    for hh in range(IDX_HEADS):
        iq_ref[hh] = (_rope(iqv[:, hh * 128:(hh + 1) * 128], rc, ra, rb) * (IDX_DIM ** -0.5)).astype(BF16)


def _dsa_proj(xf, shift, scale, gn, win, gq, gkv, wuq, wiq, wcat, rope, S):
    T, D = xf.shape
    tm = TOKEN_TILE
    nps = S // tm
    row = lambda i: (i, 0)
    bat = lambda i: (i // nps, 0, 0)
    pos = lambda i: (i % nps, 0)
    cst2 = lambda i: (0, 0)
    cst3 = lambda i: (0, 0, 0)
    return pl.pallas_call(
        _dsa_proj_kernel, grid=(T // tm,),
        in_specs=[pl.BlockSpec((tm, D), row),
                  pl.BlockSpec((1, 1, D), bat), pl.BlockSpec((1, 1, D), bat),
                  pl.BlockSpec((1, D), cst2),
                  pl.BlockSpec(win.shape, cst2),
                  pl.BlockSpec((1, DSA_Q_LORA), cst2), pl.BlockSpec((1, DSA_KV_LORA), cst2),
                  pl.BlockSpec(wuq.shape, cst2), pl.BlockSpec(wiq.shape, cst2),
                  pl.BlockSpec(wcat.shape, cst3),
                  pl.BlockSpec((tm, LANES), pos), pl.BlockSpec((tm, LANES), pos),
                  pl.BlockSpec((tm, LANES), pos)],
        out_specs=[pl.BlockSpec((DSA_HEADS, tm, 256), lambda i: (0, i, 0)),
                   pl.BlockSpec((IDX_HEADS, tm, LANES), lambda i: (0, i, 0)),
                   pl.BlockSpec((tm, 256), row),
                   pl.BlockSpec((tm, LANES), row),
                   pl.BlockSpec((tm, LANES), row)],
        out_shape=[jax.ShapeDtypeStruct((DSA_HEADS, T, 256), BF16),
                   jax.ShapeDtypeStruct((IDX_HEADS, T, LANES), BF16),
                   jax.ShapeDtypeStruct((T, 256), BF16),
                   jax.ShapeDtypeStruct((T, LANES), BF16),
                   jax.ShapeDtypeStruct((T, LANES), F32)],
        name="dsa_proj")(xf, shift, scale, gn, win, gq, gkv, wuq, wiq, wcat, *rope)


def _dsa_attn_kernel(q_ref, iq_ref, iw_ref, kcat_ref, ik_ref, o_ref, keys_ref, w_ref, m_ref, l_ref, acc_ref,
                     *, k_sel, idx_bits):
    QB, KC, H = Q_BLOCK, DSA_KC, DSA_HEADS
    qi = pl.program_id(1)
    n_ch = (qi * QB) // KC + 1
    row_t = qi * QB + lax.broadcasted_iota(jnp.int32, (QB, 1), 0)
    iw = iw_ref[...]
    for hh in range(IDX_HEADS):
        w_ref[hh * QB:(hh + 1) * QB, :] = jnp.broadcast_to(iw[:, 80 + hh:81 + hh], (QB, KC))
    iq_all = iq_ref[...].reshape(IDX_HEADS * QB, LANES)

    def score_chunk(c, carry):
        k0 = pl.multiple_of(c * KC, KC)
        ikc = ik_ref[0, pl.ds(k0, KC), :]
        r = jnp.maximum(_dot_nt(iq_all, ikc), 0.0) * w_ref[...]
        sc = r[0:QB]
        for hh in range(1, IDX_HEADS):
            sc = sc + r[hh * QB:(hh + 1) * QB]
        sc = sc + 0.0
        bits = pltpu.bitcast(sc, jnp.int32)
        key = jnp.where(bits < 0, bits ^ 0x7FFFFFFF, bits)
        pos = k0 + lax.broadcasted_iota(jnp.int32, (QB, KC), 1)
        keys_ref[c] = jnp.where(pos <= row_t, key, INT_MIN)
        return carry
    lax.fori_loop(0, n_ch, score_chunk, 0)

    def count(pred):
        def body(c, a):
            kk = keys_ref[c]
            pos = c * KC + lax.broadcasted_iota(jnp.int32, (QB, KC), 1)
            m = jnp.where(pred(kk, pos), 1.0, 0.0)
            for j in range(KC // LANES):
                a = a + m[:, j * LANES:(j + 1) * LANES]
            return a
        a = lax.fori_loop(0, n_ch, body, jnp.zeros((QB, LANES), F32))
        return jnp.sum(a, axis=1, keepdims=True)

    def bit_body(it, thr):
        cand = thr ^ jnp.left_shift(jnp.int32(1), 31 - it)
        return jnp.where(count(lambda kk, pos: kk >= cand) >= k_sel, cand, thr)
    thr = lax.fori_loop(0, 32, bit_body, jnp.full((QB, 1), INT_MIN, jnp.int32))

    n_gt = count(lambda kk, pos: kk > thr)
    n_ge = count(lambda kk, pos: kk >= thr)
    tie = jnp.where((n_ge > k_sel) & (thr > INT_MIN), 1.0, 0.0)

    @pl.when(jnp.max(tie) > 0.0)
    def _():
        need = k_sel - n_gt
        def jbit(it, jcut):
            cand = jcut | jnp.left_shift(jnp.int32(1), idx_bits - 1 - it)
            f = count(lambda kk, pos: (kk == thr) & (pos < cand))
            return jnp.where(f <= need, cand, jcut)
        jcut = lax.fori_loop(0, idx_bits, jbit, jnp.zeros((QB, 1), jnp.int32))
        def drop(c, carry):
            kk = keys_ref[c]
            pos = c * KC + lax.broadcasted_iota(jnp.int32, (QB, KC), 1)
            keys_ref[c] = jnp.where((kk == thr) & (pos >= jcut), INT_MIN, kk)
            return carry
        lax.fori_loop(0, n_ch, drop, 0)

    thr_eff = jnp.maximum(thr, INT_MIN + 1)
    _flash_init(m_ref, l_ref, acc_ref)
    q_all = q_ref[...].reshape(H * QB, 256)

    def chunk(c, carry):
        k0 = pl.multiple_of(c * KC, KC)
        kc = kcat_ref[0, pl.ds(k0, KC), :]
        bias = jnp.where(keys_ref[c] >= thr_eff, 0.0, -jnp.inf)
        s = _dot_nt(q_all, kc) + _tile_rows(bias, H)
        _flash_update(s, kc[:, :DSA_KV_LORA], m_ref, l_ref, acc_ref, 0)
        return carry
    lax.fori_loop(0, n_ch, chunk, 0)
    o = acc_ref[...] / jnp.maximum(l_ref[...], 1e-30)
    for h in range(H):
        o_ref[:, h * 128:(h + 1) * 128] = o[h * QB:(h + 1) * QB].astype(BF16)


def _dsa_attn(qcat, iq, iw, kcat, ik, B, S):
    T = B * S
    nq = S // Q_BLOCK
    row = lambda b, q: (b * nq + q, 0)
    bat = lambda b, q: (b, 0, 0)
    k_sel = min(DSA_TOPK, S // 4)
    kern = functools.partial(_dsa_attn_kernel, k_sel=k_sel, idx_bits=int(S).bit_length())
    return pl.pallas_call(
        kern, grid=(B, nq),
        in_specs=[pl.BlockSpec((DSA_HEADS, Q_BLOCK, 256), lambda b, q: (0, b * nq + q, 0)),
                  pl.BlockSpec((IDX_HEADS, Q_BLOCK, LANES), lambda b, q: (0, b * nq + q, 0)),
                  pl.BlockSpec((Q_BLOCK, LANES), row),
                  pl.BlockSpec((1, S, 256), bat),
                  pl.BlockSpec((1, S, LANES), bat)],
        out_specs=pl.BlockSpec((Q_BLOCK, DSA_HEADS * DSA_KV_LORA), row),
        out_shape=jax.ShapeDtypeStruct((T, DSA_HEADS * DSA_KV_LORA), BF16),
        scratch_shapes=[pltpu.VMEM((S // DSA_KC, Q_BLOCK, DSA_KC), jnp.int32),
                        pltpu.VMEM((IDX_HEADS * Q_BLOCK, DSA_KC), F32),
                        pltpu.VMEM((DSA_HEADS * Q_BLOCK, LANES), F32),
                        pltpu.VMEM((DSA_HEADS * Q_BLOCK, LANES), F32),
                        pltpu.VMEM((DSA_HEADS * Q_BLOCK, DSA_KV_LORA), F32)],
        name="dsa_attn")(qcat, iq, iw, kcat.reshape(B, S, 256), ik.reshape(B, S, LANES))


def _wuvo_kernel(uv_ref, wo_ref, o_ref):
    o_ref[0] = jnp.dot(uv_ref[0], wo_ref[...], precision=HIGHEST,
                       preferred_element_type=F32).astype(BF16)


def _wuvo(w_uv, w_o):
    H, C, V = w_uv.shape
    D = w_o.shape[1]
    out = pl.pallas_call(
        _wuvo_kernel, grid=(H,),
        in_specs=[pl.BlockSpec((1, C, V), lambda h: (h, 0, 0)),
                  pl.BlockSpec((V, D), lambda h: (h, 0))],
        out_specs=pl.BlockSpec((1, C, D), lambda h: (h, 0, 0)),
        out_shape=jax.ShapeDtypeStruct((H, C, D), BF16), name="wuvo")(w_uv, w_o)
    return out.reshape(H * C, D)


def _post_kernel(a_ref, w_ref, x_ref, gate_ref, gn_ref, sc_ref, sh_ref, xo_ref, ho_ref):
    x1 = x_ref[...] + gate_ref[0] * _dot(a_ref[...], w_ref[...])
    xo_ref[...] = x1
    ho_ref[...] = _modulate(x1, gn_ref[...], sc_ref[0], sh_ref[0]).astype(ho_ref.dtype)


def _post(a, w, xf, gate, gn, scale, shift, S, h_dtype):
    T, D = xf.shape
    tm = TOKEN_TILE
    nps = S // tm
    row = lambda i: (i, 0)
    bat = lambda i: (i // nps, 0, 0)
    cst2 = lambda i: (0, 0)
    return pl.pallas_call(
        _post_kernel, grid=(T // tm,),
        in_specs=[pl.BlockSpec((tm, a.shape[1]), row), pl.BlockSpec(w.shape, cst2),
                  pl.BlockSpec((tm, D), row), pl.BlockSpec((1, 1, D), bat),
                  pl.BlockSpec((1, D), cst2), pl.BlockSpec((1, 1, D), bat), pl.BlockSpec((1, 1, D), bat)],
        out_specs=[pl.BlockSpec((tm, D), row), pl.BlockSpec((tm, D), row)],
        out_shape=[jax.ShapeDtypeStruct((T, D), F32), jax.ShapeDtypeStruct((T, D), h_dtype)],
        name="post")(a, w, xf, gate, gn, scale, shift)


def _ffn_kernel(h_ref, x_ref, gate_ref, w1_ref, w3_ref, w2_ref, xo_ref, acc_ref):
    j = pl.program_id(1)

    @pl.when(j == 0)
    def _():
        acc_ref[...] = jnp.zeros_like(acc_ref)
    hb = h_ref[...]
    a = _dot(hb, w1_ref[...])
    b = _dot(hb, w3_ref[...])
    acc_ref[...] += _dot((a * _sigmoid(a) * b).astype(BF16), w2_ref[...])

    @pl.when(j == pl.num_programs(1) - 1)
    def _():
        xo_ref[...] = x_ref[...] + gate_ref[0] * acc_ref[...]


def _ffn(h, xf, gate, w1, w3, w2, S, tn):
    T, D = xf.shape
    F = w1.shape[1]
    tm = TOKEN_TILE
    nps = S // tm
    return pl.pallas_call(
        _ffn_kernel, grid=(T // tm, F // tn),
        in_specs=[pl.BlockSpec((tm, D), lambda i, j: (i, 0)),
                  pl.BlockSpec((tm, D), lambda i, j: (i, 0)),
                  pl.BlockSpec((1, 1, D), lambda i, j: (i // nps, 0, 0)),
                  pl.BlockSpec((D, tn), lambda i, j: (0, j)),
                  pl.BlockSpec((D, tn), lambda i, j: (0, j)),
                  pl.BlockSpec((tn, D), lambda i, j: (j, 0))],
        out_specs=pl.BlockSpec((tm, D), lambda i, j: (i, 0)),
        out_shape=jax.ShapeDtypeStruct((T, D), F32),
        scratch_shapes=[pltpu.VMEM((tm, D), F32)],
        name="ffn")(h, xf, gate, w1, w3, w2)


def _nsa_proj_kernel(x_ref, sh_ref, sc_ref, gn_ref, wq_ref, wkv_ref, wg_ref, rc_ref, ra_ref, rb_ref,
                     q_ref, kc_ref, vc_ref, ks_ref, vs_ref, kw_ref, vw_ref, g_ref):
    hb = _modulate(x_ref[...], gn_ref[...], sc_ref[0], sh_ref[0]).astype(BF16)
    rc, ra, rb = rc_ref[...], ra_ref[...], rb_ref[...]

    def roped(v):
        return jnp.concatenate([_rope(v[:, :128], rc, ra, rb), _rope(v[:, 128:], rc, ra, rb)], axis=1)

    for p in range(NSA_HEADS // 2):
        qv = (roped(_dot(hb, wq_ref[:, p * 256:(p + 1) * 256])) * ATT_SCALE).astype(BF16)
        q_ref[2 * p] = qv[:, :128]
        q_ref[2 * p + 1] = qv[:, 128:]
    outs = (kc_ref, vc_ref, ks_ref, vs_ref, kw_ref, vw_ref)
    for n, o_ref in enumerate(outs):
        v = _dot(hb, wkv_ref[:, n * 256:(n + 1) * 256])
        o_ref[...] = (roped(v) if n % 2 == 0 else v).astype(BF16)
    g_ref[...] = _sigmoid(_dot(hb, wg_ref[...]))


def _nsa_proj(xf, shift, scale, gn, wq, wkv, wg, rope, S):
    T, D = xf.shape
    tm = TOKEN_TILE
    nps = S // tm
    row = lambda i: (i, 0)
    bat = lambda i: (i // nps, 0, 0)
    pos = lambda i: (i % nps, 0)
    cst2 = lambda i: (0, 0)
    kv_spec = pl.BlockSpec((tm, 256), row)
    kv_shape = jax.ShapeDtypeStruct((T, 256), BF16)
    return pl.pallas_call(
        _nsa_proj_kernel, grid=(T // tm,),
        in_specs=[pl.BlockSpec((tm, D), row),
                  pl.BlockSpec((1, 1, D), bat), pl.BlockSpec((1, 1, D), bat),
                  pl.BlockSpec((1, D), cst2),
                  pl.BlockSpec(wq.shape, cst2), pl.BlockSpec(wkv.shape, cst2), pl.BlockSpec(wg.shape, cst2),
                  pl.BlockSpec((tm, LANES), pos), pl.BlockSpec((tm, LANES), pos),
                  pl.BlockSpec((tm, LANES), pos)],
        out_specs=[pl.BlockSpec((NSA_HEADS, tm, LANES), lambda i: (0, i, 0))] + [kv_spec] * 6
                  + [pl.BlockSpec((tm, LANES), row)],
        out_shape=[jax.ShapeDtypeStruct((NSA_HEADS, T, LANES), BF16)] + [kv_shape] * 6
                  + [jax.ShapeDtypeStruct((T, LANES), F32)],
        name="nsa_proj")(xf, shift, scale, gn, wq, wkv, wg, *rope)


def _compress_kernel(uk_ref, uv_ref, pe_ref, k1_ref, k2_ref, v1_ref, v2_ref, ko_ref, vo_ref):
    half = CMP_STRIDE * HEAD_DIM
    pe = jnp.broadcast_to(pe_ref[...], (8, 2 * half)).astype(BF16)

    def comp(u_ref, w1_ref, w2_ref):
        bias = _dot(pe, w1_ref[...])[0:1]
        out = None
        for gg in range(2):
            u = u_ref[0, gg]
            a = _dot(u, w1_ref[:half, :])
            b = _dot(u, w1_ref[half:, :])
            nrow = b.shape[0]
            hid = a + pltpu.roll(b, nrow - 1, 0) + bias
            hid = (hid * _sigmoid(hid)).astype(BF16)
            o = _dot(hid, w2_ref[gg])
            out = o if out is None else out + o
        return out
    ko_ref[0] = comp(uk_ref, k1_ref, k2_ref).astype(BF16)
    vo_ref[0] = comp(uv_ref, v1_ref, v2_ref).astype(BF16)


def _compress(uk, uv, pe, k1, k2, v1, v2):
    B, G, nch, W = uk.shape
    u_spec = pl.BlockSpec((1, 2, nch, W), lambda b, p: (b, p, 0, 0))
    w1_spec = pl.BlockSpec(k1.shape, lambda b, p: (0, 0))
    w2_spec = pl.BlockSpec(k2.shape, lambda b, p: (0, 0, 0))
    o_spec = pl.BlockSpec((1, nch, LANES), lambda b, p: (b, 0, p))
    o_shape = jax.ShapeDtypeStruct((B, nch, G * HEAD_DIM), BF16)
    return pl.pallas_call(
        _compress_kernel, grid=(B, G // 2),
        in_specs=[u_spec, u_spec, pl.BlockSpec(pe.shape, lambda b, p: (0, 0)),
                  w1_spec, w2_spec, w1_spec, w2_spec],
        out_specs=[o_spec, o_spec], out_shape=[o_shape, o_shape],
        name="compress")(uk, uv, pe, k1, k2, v1, v2)


def _nsa_attn_kernel(q_ref, g_ref, kcmp_ref, vcmp_ref, ks_ref, vs_ref, kw_ref, vw_ref,
                     covt_ref, exp_ref, o_ref, oc_ref, psum_ref, imp_ref, sel_ref, m_ref, l_ref, acc_ref,
                     osel_ref, *, n_sel):
    QB, KC, HPG = Q_BLOCK, NSA_KC, NSA_HPG
    GR = HPG * QB
    qi = pl.program_id(1)
    qs = qi * QB
    row_t = qs + lax.broadcasted_iota(jnp.int32, (QB, 1), 0)
    nch = kcmp_ref.shape[1]
    nb = covt_ref.shape[0]
    n_ch = qs // KC + 1
    gates = g_ref[...]
    pair = lambda g: slice((g // 2) * 128, (g // 2) * 128 + 128)
    rows = lambda g: slice(g * GR, (g + 1) * GR)
    q_grp = lambda g: q_ref[g * HPG:(g + 1) * HPG].reshape(GR, LANES)

    cmp_end = lax.broadcasted_iota(jnp.int32, (QB, nch), 1) * CMP_STRIDE + (CMP_LEN - 1)
    cbias = _tile_rows(jnp.where(cmp_end <= row_t, 0.0, -jnp.inf), HPG)
    for g in range(NSA_GROUPS):
        s = _dot_nt(q_grp(g), kcmp_ref[0, :, pair(g)]) + cbias
        m = jnp.max(s, axis=1, keepdims=True)
        m = jnp.where(m == -jnp.inf, 0.0, m)
        e = jnp.exp2(s - m)
        p = e / jnp.maximum(jnp.sum(e, axis=1, keepdims=True), 1e-30)
        oc_ref[rows(g), :] = _dot(p.astype(BF16), vcmp_ref[0, :, pair(g)])
        psum_ref[g] = p[0:QB] + p[QB:2 * QB] + p[2 * QB:3 * QB] + p[3 * QB:4 * QB]

    jb = lax.broadcasted_iota(jnp.int32, (nb, QB), 0)
    cur = (qs + lax.broadcasted_iota(jnp.int32, (nb, QB), 1)) // SEL_LEN
    forced = (jb == 0) | (jb == cur) | (jb == cur - 1)
    ri = lax.broadcasted_iota(jnp.int32, (QB, QB), 0)
    ci = lax.broadcasted_iota(jnp.int32, (QB, QB), 1)
    eye = jnp.where(ri == ci, 1.0, 0.0).astype(BF16)
    for g in range(NSA_GROUPS):
        imp = lax.dot_general(covt_ref[...], psum_ref[g], (((1,), (1,)), ((), ())),
                              precision=HIGHEST, preferred_element_type=F32)
        imp = jnp.where(forced, FORCE_SCORE, imp)
        imp = jnp.where(jb <= cur, imp, -jnp.inf)
        imp_ref[...] = imp

        def rank_body(i, rank, imp=imp):
            ri_ = imp_ref[pl.ds(i, 1), :]
            ahead = (ri_ > imp) | ((ri_ == imp) & (i < jb))
            return rank + jnp.where(ahead, 1.0, 0.0)
        n_live = jnp.minimum((qs + QB - 1) // SEL_LEN + 1, nb)
        rank = lax.fori_loop(0, n_live, rank_body, jnp.zeros((nb, QB), F32))
        selt = jnp.where(rank < n_sel, 1.0, 0.0).astype(BF16)
        sel_ref[g] = _dot_nt(eye, selt).astype(BF16)

    _flash_init(m_ref, l_ref, acc_ref)

    def sel_chunk(c, carry):
        k0 = pl.multiple_of(c * KC, KC)
        causal = (k0 + lax.broadcasted_iota(jnp.int32, (QB, KC), 1)) <= row_t
        for g in range(NSA_GROUPS):
            keep = (_dot(sel_ref[g], exp_ref[c]) > 0.5) & causal
            bias = _tile_rows(jnp.where(keep, 0.0, -jnp.inf), HPG)
            s = _dot_nt(q_grp(g), ks_ref[0, pl.ds(k0, KC), pair(g)]) + bias
            _flash_update(s, vs_ref[0, pl.ds(k0, KC), pair(g)], m_ref, l_ref, acc_ref, g * GR)
        return carry
    lax.fori_loop(0, n_ch, sel_chunk, 0)
    osel_ref[...] = acc_ref[...] / jnp.maximum(l_ref[...], 1e-30)

    wl = WINDOW + QB
    w0 = pl.multiple_of(jnp.maximum(qs - WINDOW, 0), QB)
    wpos = w0 + lax.broadcasted_iota(jnp.int32, (QB, wl), 1)
    wbias = _tile_rows(jnp.where((wpos <= row_t) & (wpos > row_t - WINDOW), 0.0, -jnp.inf), HPG)
    for g in range(NSA_GROUPS):
        s = _dot_nt(q_grp(g), kw_ref[0, pl.ds(w0, wl), pair(g)]) + wbias
        m = jnp.max(s, axis=1, keepdims=True)
        m = jnp.where(m == -jnp.inf, 0.0, m)
        e = jnp.exp2(s - m)
        den = jnp.maximum(jnp.sum(e, axis=1, keepdims=True), 1e-30)
        acc_ref[rows(g), :] = _dot(e.astype(BF16), vw_ref[0, pl.ds(w0, wl), pair(g)]) / den

    lane = lax.broadcasted_iota(jnp.int32, (QB, LANES), 1)
    pair_out = [None, None]
    for h in range(NSA_HEADS):
        g = h // HPG
        hr = slice(h * QB, (h + 1) * QB)
        o = (gates[:, 3 * h:3 * h + 1] * oc_ref[hr, :] + gates[:, 3 * h + 1:3 * h + 2] * osel_ref[hr, :]
             + gates[:, 3 * h + 2:3 * h + 3] * acc_ref[hr, :])
        pair_out[h % 2] = o
        if h % 2 == 1:
            if g % 2 == 0:
                both = jnp.where(lane < 64, pair_out[0], pltpu.roll(pair_out[1], 64, 1))
            else:
                both = jnp.where(lane < 64, pltpu.roll(pair_out[0], 64, 1), pair_out[1])
            o_ref[:, (h // 2) * 128:(h // 2) * 128 + 128] = both.astype(BF16)


def _nsa_attn(q, gates, kcmp, vcmp, ks, vs, kw, vw, covt, expand, B, S):
    T = B * S
    nq = S // Q_BLOCK
    nch = kcmp.shape[1]
    nb = S // SEL_LEN
    row = lambda b, i: (b * nq + i, 0)
    bat = lambda b, i: (b, 0, 0)
    kv = lambda a: a.reshape(B, S, 256)
    kv_spec = pl.BlockSpec((1, S, 256), bat)
    cmp_spec = pl.BlockSpec((1, nch, 256), bat)
    kern = functools.partial(_nsa_attn_kernel, n_sel=min(SEL_BLOCKS, nb))
    return pl.pallas_call(
        kern, grid=(B, nq),
        in_specs=[pl.BlockSpec((NSA_HEADS, Q_BLOCK, LANES), lambda b, i: (0, b * nq + i, 0)),
                  pl.BlockSpec((Q_BLOCK, LANES), row),
                  cmp_spec, cmp_spec, kv_spec, kv_spec, kv_spec, kv_spec,
                  pl.BlockSpec(covt.shape, lambda b, i: (0, 0)),
                  pl.BlockSpec(expand.shape, lambda b, i: (0, 0, 0))],
        out_specs=pl.BlockSpec((Q_BLOCK, NSA_HEADS * HEAD_DIM), row),
        out_shape=jax.ShapeDtypeStruct((T, NSA_HEADS * HEAD_DIM), BF16),
        scratch_shapes=[pltpu.VMEM((NSA_HEADS * Q_BLOCK, LANES), F32),
                        pltpu.VMEM((NSA_GROUPS, Q_BLOCK, nch), F32),
                        pltpu.VMEM((nb, Q_BLOCK), F32),
                        pltpu.VMEM((NSA_GROUPS, Q_BLOCK, nb), BF16),
                        pltpu.VMEM((NSA_HEADS * Q_BLOCK, LANES), F32),
                        pltpu.VMEM((NSA_HEADS * Q_BLOCK, LANES), F32),
                        pltpu.VMEM((NSA_HEADS * Q_BLOCK, LANES), F32),
                        pltpu.VMEM((NSA_HEADS * Q_BLOCK, LANES), F32)],
        name="nsa_attn")(q, gates, kcmp, vcmp, kv(ks), kv(vs), kv(kw), kv(vw), covt, expand)


def _router_kernel(h_ref, wr_ref, route_ref, cnt_ref, carry_ref):
    i = pl.program_id(0)
    tm = h_ref.shape[0]

    @pl.when(i == 0)
    def _():
        carry_ref[...] = jnp.zeros_like(carry_ref)
    lane = lax.broadcasted_iota(jnp.int32, (tm, LANES), 1).astype(F32)
    lg = jnp.dot(h_ref[...], wr_ref[...], precision=HIGHEST, preferred_element_type=F32)
    lg = jnp.where(lane < N_EXPERTS, lg, -jnp.inf)
    v1 = jnp.max(lg, axis=1, keepdims=True)
    i1 = jnp.min(jnp.where(lg == v1, lane, float(LANES)), axis=1, keepdims=True)
    lg2 = jnp.where(lane == i1, -jnp.inf, lg)
    v2 = jnp.max(lg2, axis=1, keepdims=True)
    i2 = jnp.min(jnp.where(lg2 == v2, lane, float(LANES)), axis=1, keepdims=True)
    e2 = jnp.exp(v2 - v1)
    g1 = 1.0 / (1.0 + e2)
    g2 = e2 / (1.0 + e2)
    oh1 = jnp.where(lane == i1, 1.0, 0.0)
    oh2 = jnp.where(lane == i2, 1.0, 0.0)
    both = oh1 + oh2
    ri = lax.broadcasted_iota(jnp.int32, (tm, tm), 0)
    ci = lax.broadcasted_iota(jnp.int32, (tm, tm), 1)
    lower = jnp.where(ri > ci, 1.0, 0.0).astype(BF16)
    tot = carry_ref[0:1, :] + _dot(lower, both.astype(BF16))
    r1 = jnp.sum(oh1 * tot, axis=1, keepdims=True)
    r2 = jnp.sum(oh2 * tot, axis=1, keepdims=True)
    new_carry = carry_ref[...] + jnp.sum(both, axis=0, keepdims=True)
    carry_ref[...] = new_carry
    cnt_ref[...] = new_carry
    out = jnp.zeros((tm, LANES), F32)
    for col, val in enumerate((i1, i2, g1, g2, r1, r2)):
        out = jnp.where(lane == col, val, out)
    route_ref[...] = out


def _router(h, wr):
    T, D = h.shape
    tm = TOKEN_TILE
    return pl.pallas_call(
        _router_kernel, grid=(T // tm,),
        in_specs=[pl.BlockSpec((tm, D), lambda i: (i, 0)), pl.BlockSpec(wr.shape, lambda i: (0, 0))],
        out_specs=[pl.BlockSpec((tm, LANES), lambda i: (i, 0)), pl.BlockSpec((8, LANES), lambda i: (0, 0))],
        out_shape=[jax.ShapeDtypeStruct((T, LANES), F32), jax.ShapeDtypeStruct((8, LANES), F32)],
        scratch_shapes=[pltpu.VMEM((8, LANES), F32)],
        name="router")(h, wr)


def _row_copy(src, dst, sem):
    return pltpu.make_async_copy(src, dst, sem)


def _moe_ffn_kernel(be_ref, nu_ref, src_ref, dst_ref, h_ref, w1_ref, w3_ref, w2_ref, y_ref,
                    xin_ref, xb_ref, acc_ref, yout_ref, sem_in, sem_out, *, nj):
    i = pl.program_id(0)
    j = pl.program_id(1)
    nblk = src_ref.shape[0]
    per_step = MOE_BLOCK // nj
    used = i < nu_ref[0]
    slot = i % 2
    nxt = jnp.minimum(i + 1, nblk - 1)
    prv = jnp.maximum(i - 1, 0)

    def gather(blk, r, buf):
        return _row_copy(h_ref.at[pl.ds(src_ref[blk, r], 1), :], xin_ref.at[buf, pl.ds(r, 1), :],
                         sem_in.at[buf])

    def scatter(blk, r):
        return _row_copy(yout_ref.at[pl.ds(r, 1), :], y_ref.at[pl.ds(dst_ref[blk, r], 1), :], sem_out)

    def issue(r0, n):
        for u in range(n):
            gather(nxt, r0 + u, 1 - slot).start()
            scatter(prv, r0 + u).start()

    def wait_gathered(buf):
        _row_copy(h_ref.at[pl.ds(0, MOE_BLOCK), :], xin_ref.at[buf], sem_in.at[buf]).wait()

    def wait_scattered():
        _row_copy(yout_ref, y_ref.at[pl.ds(0, MOE_BLOCK), :], sem_out).wait()

    @pl.when((i == 0) & (j == 0))
    def _():
        yout_ref[...] = jnp.zeros_like(yout_ref)

        def first(r, carry):
            gather(0, r, 0).start()
            return carry
        lax.fori_loop(0, MOE_BLOCK, first, 0)

    @pl.when(j == 0)
    def _():
        wait_gathered(slot)
        xb_ref[...] = xin_ref[slot].astype(BF16)
        acc_ref[...] = jnp.zeros_like(acc_ref)

    @pl.when(used)
    def _():
        issue(j * per_step, per_step)
        xb = xb_ref[...]
        a = _dot(xb, w1_ref[0])
        b = _dot(xb, w3_ref[0])
        acc_ref[...] += _dot((a * _sigmoid(a) * b).astype(BF16), w2_ref[0])

    @pl.when(jnp.logical_not(used))
    def _():
        issue(j * per_step, per_step)

    @pl.when(j == nj - 1)
    def _():
        def rest(r, carry):
            gather(nxt, r, 1 - slot).start()
            scatter(prv, r).start()
            return carry
        lax.fori_loop(nj * per_step, MOE_BLOCK, rest, 0)
        wait_scattered()
        yout_ref[...] = acc_ref[...]

    @pl.when((i == pl.num_programs(0) - 1) & (j == nj - 1))
    def _():
        wait_gathered(1 - slot)


def _moe_ffn(block_e, n_used, slot_src, slot_dst, h, w1, w3, w2, tn):
    T, D = h.shape
    nblk = slot_src.shape[0]
    E, _, F = w1.shape
    nj = F // tn
    bi = lambda i: jnp.minimum(i, nblk - 1)
    jj = lambda i, j, nu: jnp.where(i < nu[0], j, nj - 1)
    return pl.pallas_call(
        functools.partial(_moe_ffn_kernel, nj=nj),
        grid_spec=pltpu.PrefetchScalarGridSpec(
            num_scalar_prefetch=4, grid=(nblk + 1, nj),
            in_specs=[pl.BlockSpec(memory_space=pl.ANY),
                      pl.BlockSpec((1, D, tn), lambda i, j, be, nu, s, d: (be[bi(i)], 0, jj(i, j, nu))),
                      pl.BlockSpec((1, D, tn), lambda i, j, be, nu, s, d: (be[bi(i)], 0, jj(i, j, nu))),
                      pl.BlockSpec((1, tn, D), lambda i, j, be, nu, s, d: (be[bi(i)], jj(i, j, nu), 0))],
            out_specs=pl.BlockSpec(memory_space=pl.ANY),
            scratch_shapes=[pltpu.VMEM((2, MOE_BLOCK, D), F32), pltpu.VMEM((MOE_BLOCK, D), BF16),
                            pltpu.VMEM((MOE_BLOCK, D), F32), pltpu.VMEM((MOE_BLOCK, D), F32),
                            pltpu.SemaphoreType.DMA((2,)), pltpu.SemaphoreType.DMA(())]),
        out_shape=jax.ShapeDtypeStruct((2 * T + MOE_BLOCK, D), F32),
        name="moe_ffn")(block_e, n_used, slot_src, slot_dst, h, w1, w3, w2)


def _moe_combine_kernel(y0_ref, y1_ref, x_ref, gate_ref, route_ref, fn_ref, o_ref):
    route = route_ref[...]
    y = route[:, 2:3] * y0_ref[...] + route[:, 3:4] * y1_ref[...]
    o_ref[...] = _rms(x_ref[...] + gate_ref[0] * y, fn_ref[...])


def _moe_combine(y, xf, gate, route, fn, S):
    T, D = xf.shape
    tm = TOKEN_TILE
    nps = S // tm
    nt = T // tm
    return pl.pallas_call(
        _moe_combine_kernel, grid=(nt,),
        in_specs=[pl.BlockSpec((tm, D), lambda i: (i, 0)),
                  pl.BlockSpec((tm, D), lambda i: (nt + i, 0)),
                  pl.BlockSpec((tm, D), lambda i: (i, 0)),
                  pl.BlockSpec((1, 1, D), lambda i: (i // nps, 0, 0)),
                  pl.BlockSpec((tm, LANES), lambda i: (i, 0)),
                  pl.BlockSpec((1, D), lambda i: (0, 0))],
        out_specs=pl.BlockSpec((tm, D), lambda i: (i, 0)),
        out_shape=jax.ShapeDtypeStruct((T, D), F32), name="moe_combine")(y, y, xf, gate, route, fn)


def _rope_tables(S):
    inv = ROPE_THETA ** (-jnp.arange(0, ROPE_DIM, 2, dtype=F32) / ROPE_DIM)
    ang = jnp.arange(S, dtype=F32)[:, None] * inv[None, :]
    cos, sin = jnp.cos(ang), jnp.sin(ang)
    pm = np.arange(LANES) % HEAD_DIM
    col = pm % (ROPE_DIM // 2)
    rc = jnp.where((pm < ROPE_DIM)[None, :], cos[:, col], 1.0)
    ra = jnp.where((pm < ROPE_DIM // 2)[None, :], -sin[:, col], 0.0)
    rb = jnp.where(((pm >= ROPE_DIM // 2) & (pm < ROPE_DIM))[None, :], sin[:, col], 0.0)
    return rc, ra, rb


def _dsa_weights(w_in, w_uk, w_iq):
    D = w_in.shape[0]
    a, b, c, d = DSA_Q_LORA, DSA_Q_LORA + DSA_KV_LORA, DSA_Q_LORA + DSA_KV_LORA + ROPE_DIM, \
        DSA_Q_LORA + DSA_KV_LORA + ROPE_DIM + IDX_DIM
    win = jnp.concatenate([w_in[:, :b], w_in[:, c:d], w_in[:, b:c], w_in[:, d:],
                           jnp.zeros((D, 512 - w_in.shape[1]), F32)], axis=1).astype(BF16)
    H = DSA_HEADS
    blk = jnp.zeros((H, HEAD_DIM, 256), F32)
    blk = blk.at[:, ROPE_DIM:, :DSA_KV_LORA].set(jnp.transpose(w_uk, (0, 2, 1)))
    blk = blk.at[:, :ROPE_DIM, 192:192 + ROPE_DIM].set(jnp.eye(ROPE_DIM, dtype=F32))
    z = jnp.zeros((H // 2, HEAD_DIM, 256), F32)
    wcat = jnp.concatenate([jnp.concatenate([blk[0::2], z], axis=2),
                            jnp.concatenate([z, blk[1::2]], axis=2)], axis=1).astype(BF16)
    wiq = w_iq.reshape(DSA_Q_LORA, IDX_HEADS, IDX_DIM)
    wiq = jnp.concatenate([wiq, jnp.zeros_like(wiq)], axis=2).reshape(DSA_Q_LORA, IDX_HEADS * 128)
    return win, wcat, wiq.astype(BF16)


def _nsa_weights(w_in):
    D = w_in.shape[0]
    nq = NSA_HEADS * HEAD_DIM
    wq = w_in[:, :nq].reshape(D, NSA_HEADS, HEAD_DIM)
    z = jnp.zeros_like(wq)
    odd = ((np.arange(NSA_HEADS) // NSA_HPG) % 2 == 1)[None, :, None]
    wq = jnp.concatenate([jnp.where(odd, z, wq), jnp.where(odd, wq, z)], axis=2).reshape(D, NSA_HEADS * 128)
    wkv = w_in[:, nq:nq + 6 * 256]
    wg = jnp.concatenate([w_in[:, nq + 6 * 256:], jnp.zeros((D, LANES - 3 * NSA_HEADS), F32)], axis=1)
    return wq.astype(BF16), wkv.astype(BF16), wg.astype(BF16)


def _nsa_tables(S):
    nch = S // CMP_STRIDE
    nc = (S - CMP_LEN) // CMP_STRIDE + 1
    nb = S // SEL_LEN
    cstart = np.arange(nch) * CMP_STRIDE
    bstart = np.arange(nb) * SEL_LEN
    cov = ((cstart[None, :] < bstart[:, None] + SEL_LEN) & (cstart[None, :] + CMP_LEN > bstart[:, None])
           & (np.arange(nch)[None, :] < nc)).astype(np.float32)
    kpos = np.arange(S).reshape(S // NSA_KC, 1, NSA_KC)
    expand = (kpos // SEL_LEN == np.arange(nb)[None, :, None]).astype(np.float32)
    return jnp.asarray(cov), jnp.asarray(expand, dtype=BF16)


def _slot_tables(dest, T, n_blocks):
    ns = n_blocks * MOE_BLOCK
    a = jnp.arange(2 * T, dtype=jnp.int32)
    src = jnp.zeros((ns,), jnp.int32).at[dest].set(a // 2)
    spare = 2 * T + jnp.arange(ns, dtype=jnp.int32) % MOE_BLOCK
    dst = spare.at[dest].set((a % 2) * T + a // 2)
    return src.reshape(n_blocks, MOE_BLOCK), dst.reshape(n_blocks, MOE_BLOCK)


def _chunk_tokens(a, B, S):
    a = a.reshape(B, S // CMP_STRIDE, CMP_STRIDE, NSA_GROUPS, HEAD_DIM)
    return jnp.transpose(a, (0, 3, 1, 2, 4)).reshape(B, NSA_GROUPS, S // CMP_STRIDE, CMP_STRIDE * HEAD_DIM)


def kernel(x, c, norm_mix, norm_ffn, ada_w, ada_b, final_norm, dsa_w_in, dsa_g_q, dsa_w_uq, dsa_g_kv,
           dsa_w_uk, dsa_w_uv, dsa_w_iq, dsa_w_o, ffn_w1, ffn_w3, ffn_w2, nsa_w_in, nsa_cmp_pe,
           nsa_cmp_k1, nsa_cmp_k2, nsa_cmp_v1, nsa_cmp_v2, nsa_w_o, moe_router, moe_w1, moe_w3, moe_w2):
    B, S, D = x.shape
    T = B * S
    xf = x.reshape(T, D)
    mods = _ada(c, ada_w, ada_b).reshape(4, B, 3, 1, D)
    shift = lambda s: mods[s, :, 0]
    scale = lambda s: mods[s, :, 1]
    gate = lambda s: mods[s, :, 2]
    rope = _rope_tables(S)

    win, wcat, wiq = _dsa_weights(dsa_w_in[0], dsa_w_uk[0], dsa_w_iq[0])
    qcat, iq, kcat, ik, iw = _dsa_proj(
        xf, shift(0), scale(0), norm_mix[0:1], win, dsa_g_q[0:1], dsa_g_kv[0:1],
        dsa_w_uq[0].astype(BF16), wiq, wcat, rope, S)
    olat = _dsa_attn(qcat, iq, iw, kcat, ik, B, S)
    wuvo = _wuvo(dsa_w_uv[0], dsa_w_o[0])
    x1, h1 = _post(olat, wuvo, xf, gate(0), norm_ffn[0:1], scale(1), shift(1), S, BF16)
    x2 = _ffn(h1, x1, gate(1), ffn_w1[0].astype(BF16), ffn_w3[0].astype(BF16), ffn_w2[0].astype(BF16),
              S, ffn_w1.shape[2] // 2)

    wq, wkv, wg = _nsa_weights(nsa_w_in[0])
    q, kc, vc, ks, vs, kw, vw, gates = _nsa_proj(xf=x2, shift=shift(2), scale=scale(2), gn=norm_mix[1:2],
                                                 wq=wq, wkv=wkv, wg=wg, rope=rope, S=S)
    zpad = jnp.zeros((CMP_HIDDEN, HEAD_DIM), F32)
    pad2 = lambda w2: jnp.stack([jnp.concatenate([w2, zpad], axis=1),
                                 jnp.concatenate([zpad, w2], axis=1)]).astype(BF16)
    kcmp, vcmp = _compress(_chunk_tokens(kc, B, S), _chunk_tokens(vc, B, S),
                           nsa_cmp_pe[0].reshape(1, CMP_LEN * HEAD_DIM),
                           nsa_cmp_k1[0].astype(BF16), pad2(nsa_cmp_k2[0]),
                           nsa_cmp_v1[0].astype(BF16), pad2(nsa_cmp_v2[0]))
    covt, expand = _nsa_tables(S)
    o = _nsa_attn(q, gates, kcmp, vcmp, ks, vs, kw, vw, covt, expand, B, S)
    x3, h3 = _post(o, nsa_w_o[0].astype(BF16), x2, gate(2), norm_ffn[1:2], scale(3), shift(3), S, F32)

    wr = jnp.concatenate([moe_router[0], jnp.zeros((D, LANES - N_EXPERTS), F32)], axis=1)
    route, cnt = _router(h3, wr)
    counts = cnt[0, :N_EXPERTS].astype(jnp.int32)
    padded = (counts + MOE_BLOCK - 1) // MOE_BLOCK * MOE_BLOCK
    ends = jnp.cumsum(padded)
    pstart = ends - padded
    eidx = route[:, 0:2].astype(jnp.int32)
    dest = (pstart[eidx] + route[:, 4:6].astype(jnp.int32)).reshape(-1)
    n_blocks = -(-(T * 2) // MOE_BLOCK) + N_EXPERTS
    block_start = jnp.arange(n_blocks, dtype=jnp.int32) * MOE_BLOCK
    block_e = jnp.minimum(jnp.sum((ends[None, :] <= block_start[:, None]).astype(jnp.int32), axis=1),
                          N_EXPERTS - 1)
    n_used = (ends[-1:] // MOE_BLOCK).astype(jnp.int32)
    slot_src, slot_dst = _slot_tables(dest, T, n_blocks)
    y = _moe_ffn(block_e, n_used, slot_src, slot_dst, h3, moe_w1[0].astype(BF16), moe_w3[0].astype(BF16),
                 moe_w2[0].astype(BF16), 512)
    out = _moe_combine(y, x3, gate(3), route, final_norm.reshape(1, D), S)
    return out.reshape(B, S, D)
```

```python
import functools

import numpy as np
import jax
import jax.numpy as jnp
from jax import lax
from jax.experimental import pallas as pl
from jax.experimental.pallas import tpu as pltpu

F32 = jnp.float32
BF16 = jnp.bfloat16
HIGHEST = lax.Precision.HIGHEST
INT_MIN = -2147483648

HEAD_DIM = 64
ROPE_DIM = 16
ROPE_THETA = 500000.0
Q_BLOCK = 128
NORM_EPS = 1e-6

DSA_HEADS = 16
DSA_NOPE = 48
DSA_Q_LORA = 256
DSA_KV_LORA = 128
IDX_HEADS = 8
IDX_DIM = 64
DSA_TOPK = 256
DSA_KC = 512

NSA_HEADS = 16
NSA_GROUPS = 4
NSA_HPG = 4
CMP_LEN = 32
CMP_STRIDE = 16
CMP_HIDDEN = 256
SEL_LEN = 64
SEL_BLOCKS = 16
WINDOW = 512
FORCE_SCORE = 1e4
NSA_KC = 512

N_EXPERTS = 8
MOE_BLOCK = 512
TOKEN_TILE = 512
ROW_TILE = 256
LANES = 128

LOG2E = 1.4426950408889634
ATT_SCALE = HEAD_DIM ** -0.5 * LOG2E


def _dot(a, b):
    return jnp.dot(a, b, preferred_element_type=F32)


def _dot_nt(a, b):
    return lax.dot_general(a, b, (((1,), (1,)), ((), ())), preferred_element_type=F32)


def _sigmoid(v):
    return 1.0 / (1.0 + jnp.exp(-v))


def _rms(v, g):
    return v * lax.rsqrt(jnp.mean(v * v, axis=-1, keepdims=True) + NORM_EPS) * g


def _modulate(v, g, scale, shift):
    return _rms(v, g) * (1.0 + scale) + shift


def _rope(v, c, sa, sb):
    return v * c + pltpu.roll(v, LANES - 8, 1) * sa + pltpu.roll(v, 8, 1) * sb


def _tile_rows(a, n):
    return jnp.concatenate([a] * n, axis=0) if n > 1 else a


def _tile_lanes(a, n):
    return jnp.concatenate([a] * n, axis=1) if n > 1 else a


def _flash_init(m_ref, l_ref, acc_ref):
    m_ref[...] = jnp.full(m_ref.shape, -1e30, F32)
    l_ref[...] = jnp.zeros(l_ref.shape, F32)
    acc_ref[...] = jnp.zeros(acc_ref.shape, F32)


def _flash_update(s, v, m_ref, l_ref, acc_ref, row0):
    n = s.shape[0]
    ps = []
    for r in range(0, n, Q_BLOCK):
        rs = slice(row0 + r, row0 + r + Q_BLOCK)
        sl = s[r:r + Q_BLOCK]
        m_old = m_ref[rs, :]
        m_new = jnp.maximum(m_old, jnp.max(sl, axis=1, keepdims=True))
        p = jnp.exp2(sl - _tile_lanes(m_new, sl.shape[1] // LANES))
        alpha = jnp.exp2(m_old - m_new)
        l_ref[rs, :] = alpha * l_ref[rs, :] + jnp.sum(p, axis=1, keepdims=True)
        acc_ref[rs, :] = alpha * acc_ref[rs, :]
        m_ref[rs, :] = m_new
        ps.append(p.astype(BF16))
    pv = _dot(jnp.concatenate(ps, axis=0), v)
    acc_ref[row0:row0 + n, :] += pv


def _ada_kernel(c_ref, w_ref, b_ref, o_ref):
    cv = c_ref[...]
    sc = cv * _sigmoid(cv)
    o_ref[0] = jnp.dot(sc, w_ref[0], precision=HIGHEST, preferred_element_type=F32) + b_ref[0]


def _ada(c, ada_w, ada_b):
    B, D = c.shape
    w = ada_w.reshape(4, D, 3 * D)
    b = ada_b.reshape(4, 1, 3 * D)
    return pl.pallas_call(
        _ada_kernel, grid=(4, 3),
        in_specs=[pl.BlockSpec((B, D), lambda l, j: (0, 0)),
                  pl.BlockSpec((1, D, D), lambda l, j: (l, 0, j)),
                  pl.BlockSpec((1, 1, D), lambda l, j: (l, 0, j))],
        out_specs=pl.BlockSpec((1, B, D), lambda l, j: (l, 0, j)),
        out_shape=jax.ShapeDtypeStruct((4, B, 3 * D), F32), name="ada")(c, w, b)


def _dsa_proj_kernel(x_ref, sh_ref, sc_ref, gn_ref, win_ref, gq_ref, gkv_ref, wuq_ref, wiq_ref,
                     wcat_ref, rc_ref, ra_ref, rb_ref, qcat_ref, iq_ref, kcat_ref, ik_ref, iw_ref):
    h = _modulate(x_ref[...], gn_ref[...], sc_ref[0], sh_ref[0])
    proj = _dot(h.astype(BF16), win_ref[...])
    q_lat = _rms(proj[:, :256], gq_ref[...]).astype(BF16)
    c_kv = _rms(proj[:, 256:384], gkv_ref[...])
    rc, ra, rb = rc_ref[...], ra_ref[...], rb_ref[...]
    rest = _rope(proj[:, 384:512], rc, ra, rb)
    lane = lax.broadcasted_iota(jnp.int32, rest.shape, 1)
    kcat_ref[:, :128] = c_kv.astype(BF16)
    kcat_ref[:, 128:] = jnp.where((lane >= 64) & (lane < 80), rest, 0.0).astype(BF16)
    ik_ref[...] = jnp.where(lane < 64, rest, 0.0).astype(BF16)
    iw_ref[...] = rest * (IDX_HEADS ** -0.5)
    q = _dot(q_lat, wuq_ref[...])
    for p in range(DSA_HEADS // 2):
        qp = (_rope(q[:, p * 128:(p + 1) * 128], rc, ra, rb) * ATT_SCALE).astype(BF16)
        res = _dot(qp, wcat_ref[p]).astype(BF16)
        qcat_ref[2 * p] = res[:, :256]
        qcat_ref[2 * p + 1] = res[:, 256:]
    iqv = _dot(q_lat, wiq_ref[...])
    for hh in range(IDX_HEADS):
        iq_ref[hh] = (_rope(iqv[:, hh * 128:(hh + 1) * 128], rc, ra, rb) * (IDX_DIM ** -0.5)).astype(BF16)


def _dsa_proj(xf, shift, scale, gn, win, gq, gkv, wuq, wiq, wcat, rope, S):
    T, D = xf.shape
    tm = TOKEN_TILE
    nps = S // tm
    row = lambda i: (i, 0)
    bat = lambda i: (i // nps, 0, 0)
    pos = lambda i: (i % nps, 0)
    cst2 = lambda i: (0, 0)
    cst3 = lambda i: (0, 0, 0)
    return pl.pallas_call(
        _dsa_proj_kernel, grid=(T // tm,),
        in_specs=[pl.BlockSpec((tm, D), row),
                  pl.BlockSpec((1, 1, D), bat), pl.BlockSpec((1, 1, D), bat),
                  pl.BlockSpec((1, D), cst2),
                  pl.BlockSpec(win.shape, cst2),
                  pl.BlockSpec((1, DSA_Q_LORA), cst2), pl.BlockSpec((1, DSA_KV_LORA), cst2),
                  pl.BlockSpec(wuq.shape, cst2), pl.BlockSpec(wiq.shape, cst2),
                  pl.BlockSpec(wcat.shape, cst3),
                  pl.BlockSpec((tm, LANES), pos), pl.BlockSpec((tm, LANES), pos),
                  pl.BlockSpec((tm, LANES), pos)],
        out_specs=[pl.BlockSpec((DSA_HEADS, tm, 256), lambda i: (0, i, 0)),
                   pl.BlockSpec((IDX_HEADS, tm, LANES), lambda i: (0, i, 0)),
                   pl.BlockSpec((tm, 256), row),
                   pl.BlockSpec((tm, LANES), row),
                   pl.BlockSpec((tm, LANES), row)],
        out_shape=[jax.ShapeDtypeStruct((DSA_HEADS, T, 256), BF16),
                   jax.ShapeDtypeStruct((IDX_HEADS, T, LANES), BF16),
                   jax.ShapeDtypeStruct((T, 256), BF16),
                   jax.ShapeDtypeStruct((T, LANES), BF16),
                   jax.ShapeDtypeStruct((T, LANES), F32)],
        name="dsa_proj")(xf, shift, scale, gn, win, gq, gkv, wuq, wiq, wcat, *rope)


def _dsa_attn_kernel(q_ref, iq_ref, iw_ref, kcat_ref, ik_ref, o_ref, keys_ref, w_ref, m_ref, l_ref, acc_ref,
                     *, k_sel, idx_bits):
    QB, KC, H = Q_BLOCK, DSA_KC, DSA_HEADS
    qi = pl.program_id(1)
    n_ch = (qi * QB) // KC + 1
    row_t = qi * QB + lax.broadcasted_iota(jnp.int32, (QB, 1), 0)
    iw = iw_ref[...]
    for hh in range(IDX_HEADS):
        w_ref[hh * QB:(hh + 1) * QB, :] = jnp.broadcast_to(iw[:, 80 + hh:81 + hh], (QB, KC))
    iq_all = iq_ref[...].reshape(IDX_HEADS * QB, LANES)

    def score_chunk(c, carry):
        k0 = pl.multiple_of(c * KC, KC)
        ikc = ik_ref[0, pl.ds(k0, KC), :]
        r = jnp.maximum(_dot_nt(iq_all, ikc), 0.0) * w_ref[...]
        sc = r[0:QB]
        for hh in range(1, IDX_HEADS):
            sc = sc + r[hh * QB:(hh + 1) * QB]
        sc = sc + 0.0
        bits = pltpu.bitcast(sc, jnp.int32)
        key = jnp.where(bits < 0, bits ^ 0x7FFFFFFF, bits)
        pos = k0 + lax.broadcasted_iota(jnp.int32, (QB, KC), 1)
        keys_ref[c] = jnp.where(pos <= row_t, key, INT_MIN)
        return carry
    lax.fori_loop(0, n_ch, score_chunk, 0)

    def count(pred):
        def body(c, a):
            kk = keys_ref[c]
            pos = c * KC + lax.broadcasted_iota(jnp.int32, (QB, KC), 1)
            m = jnp.where(pred(kk, pos), 1.0, 0.0)
            for j in range(KC // LANES):
                a = a + m[:, j * LANES:(j + 1) * LANES]
            return a
        a = lax.fori_loop(0, n_ch, body, jnp.zeros((QB, LANES), F32))
        return jnp.sum(a, axis=1, keepdims=True)

    def count3(c1, c2, c3):
        def body(c, a):
            for j in range(KC // LANES):
                kk = keys_ref[c, :, j * LANES:(j + 1) * LANES]
                a = a + jnp.where(kk >= c3, 4161, jnp.where(kk >= c2, 65, jnp.where(kk >= c1, 1, 0)))
            return a
        a = lax.fori_loop(0, n_ch, body, jnp.zeros((QB, LANES), jnp.int32))
        lane_sum = lambda x: jnp.sum(x.astype(F32), axis=1, keepdims=True)
        return lane_sum(a & 63), lane_sum((a >> 6) & 63), lane_sum(a >> 12)

    def bit_body(it, thr):
        b1 = jnp.left_shift(jnp.int32(1), 31 - 2 * it)
        b0 = jnp.left_shift(jnp.int32(1), 30 - 2 * it)
        c1, c2 = thr ^ b0, thr ^ b1
        c3 = c2 ^ b0
        n1, n2, n3 = count3(c1, c2, c3)
        return jnp.where(n3 >= k_sel, c3, jnp.where(n2 >= k_sel, c2, jnp.where(n1 >= k_sel, c1, thr)))
    thr = lax.fori_loop(0, 16, bit_body, jnp.full((QB, 1), INT_MIN, jnp.int32))

    n_gt = count(lambda kk, pos: kk > thr)
    n_ge = count(lambda kk, pos: kk >= thr)
    tie = jnp.where((n_ge > k_sel) & (thr > INT_MIN), 1.0, 0.0)

    @pl.when(jnp.max(tie) > 0.0)
    def _():
        need = k_sel - n_gt
        def jbit(it, jcut):
            cand = jcut | jnp.left_shift(jnp.int32(1), idx_bits - 1 - it)
            f = count(lambda kk, pos: (kk == thr) & (pos < cand))
            return jnp.where(f <= need, cand, jcut)
        jcut = lax.fori_loop(0, idx_bits, jbit, jnp.zeros((QB, 1), jnp.int32))
        def drop(c, carry):
            kk = keys_ref[c]
            pos = c * KC + lax.broadcasted_iota(jnp.int32, (QB, KC), 1)
            keys_ref[c] = jnp.where((kk == thr) & (pos >= jcut), INT_MIN, kk)
            return carry
        lax.fori_loop(0, n_ch, drop, 0)

    thr_eff = jnp.maximum(thr, INT_MIN + 1)
    _flash_init(m_ref, l_ref, acc_ref)
    q_all = q_ref[...].reshape(H * QB, 256)

    def chunk(c, carry):
        k0 = pl.multiple_of(c * KC, KC)
        kc = kcat_ref[0, pl.ds(k0, KC), :]
        bias = jnp.where(keys_ref[c] >= thr_eff, 0.0, -jnp.inf)
        s = _dot_nt(q_all, kc) + _tile_rows(bias, H)
        _flash_update(s, kc[:, :DSA_KV_LORA], m_ref, l_ref, acc_ref, 0)
        return carry
    lax.fori_loop(0, n_ch, chunk, 0)
    o = acc_ref[...] / jnp.maximum(l_ref[...], 1e-30)
    for h in range(H):
        o_ref[:, h * 128:(h + 1) * 128] = o[h * QB:(h + 1) * QB].astype(BF16)


def _dsa_attn(qcat, iq, iw, kcat, ik, B, S):
    T = B * S
    nq = S // Q_BLOCK
    row = lambda b, q: (b * nq + q, 0)
    bat = lambda b, q: (b, 0, 0)
    k_sel = min(DSA_TOPK, S // 4)
    assert S // LANES < 64, "per-lane key counts are packed 6 bits each"
    kern = functools.partial(_dsa_attn_kernel, k_sel=k_sel, idx_bits=int(S).bit_length())
    return pl.pallas_call(
        kern, grid=(B, nq),
        in_specs=[pl.BlockSpec((DSA_HEADS, Q_BLOCK, 256), lambda b, q: (0, b * nq + q, 0)),
                  pl.BlockSpec((IDX_HEADS, Q_BLOCK, LANES), lambda b, q: (0, b * nq + q, 0)),
                  pl.BlockSpec((Q_BLOCK, LANES), row),
                  pl.BlockSpec((1, S, 256), bat),
                  pl.BlockSpec((1, S, LANES), bat)],
        out_specs=pl.BlockSpec((Q_BLOCK, DSA_HEADS * DSA_KV_LORA), row),
        out_shape=jax.ShapeDtypeStruct((T, DSA_HEADS * DSA_KV_LORA), BF16),
        scratch_shapes=[pltpu.VMEM((S // DSA_KC, Q_BLOCK, DSA_KC), jnp.int32),
                        pltpu.VMEM((IDX_HEADS * Q_BLOCK, DSA_KC), F32),
                        pltpu.VMEM((DSA_HEADS * Q_BLOCK, LANES), F32),
                        pltpu.VMEM((DSA_HEADS * Q_BLOCK, LANES), F32),
                        pltpu.VMEM((DSA_HEADS * Q_BLOCK, DSA_KV_LORA), F32)],
        name="dsa_attn")(qcat, iq, iw, kcat.reshape(B, S, 256), ik.reshape(B, S, LANES))


def _wuvo_kernel(uv_ref, wo_ref, o_ref):
    o_ref[0] = jnp.dot(uv_ref[0], wo_ref[...], precision=HIGHEST,
                       preferred_element_type=F32).astype(BF16)


def _wuvo(w_uv, w_o):
    H, C, V = w_uv.shape
    D = w_o.shape[1]
    out = pl.pallas_call(
        _wuvo_kernel, grid=(H,),
        in_specs=[pl.BlockSpec((1, C, V), lambda h: (h, 0, 0)),
                  pl.BlockSpec((V, D), lambda h: (h, 0))],
        out_specs=pl.BlockSpec((1, C, D), lambda h: (h, 0, 0)),
        out_shape=jax.ShapeDtypeStruct((H, C, D), BF16), name="wuvo")(w_uv, w_o)
    return out.reshape(H * C, D)


def _post_kernel(a_ref, w_ref, x_ref, gate_ref, gn_ref, sc_ref, sh_ref, xo_ref, ho_ref):
    x1 = x_ref[...] + gate_ref[0] * _dot(a_ref[...], w_ref[...])
    xo_ref[...] = x1
    ho_ref[...] = _modulate(x1, gn_ref[...], sc_ref[0], sh_ref[0]).astype(ho_ref.dtype)


def _post(a, w, xf, gate, gn, scale, shift, S, h_dtype):
    T, D = xf.shape
    tm = TOKEN_TILE
    nps = S // tm
    row = lambda i: (i, 0)
    bat = lambda i: (i // nps, 0, 0)
    cst2 = lambda i: (0, 0)
    return pl.pallas_call(
        _post_kernel, grid=(T // tm,),
        in_specs=[pl.BlockSpec((tm, a.shape[1]), row), pl.BlockSpec(w.shape, cst2),
                  pl.BlockSpec((tm, D), row), pl.BlockSpec((1, 1, D), bat),
                  pl.BlockSpec((1, D), cst2), pl.BlockSpec((1, 1, D), bat), pl.BlockSpec((1, 1, D), bat)],
        out_specs=[pl.BlockSpec((tm, D), row), pl.BlockSpec((tm, D), row)],
        out_shape=[jax.ShapeDtypeStruct((T, D), F32), jax.ShapeDtypeStruct((T, D), h_dtype)],
        name="post")(a, w, xf, gate, gn, scale, shift)


def _ffn_kernel(h_ref, x_ref, gate_ref, w1_ref, w3_ref, w2_ref, xo_ref, acc_ref):
    j = pl.program_id(1)

    @pl.when(j == 0)
    def _():
        acc_ref[...] = jnp.zeros_like(acc_ref)
    hb = h_ref[...]
    a = _dot(hb, w1_ref[...])
    b = _dot(hb, w3_ref[...])
    acc_ref[...] += _dot((a * _sigmoid(a) * b).astype(BF16), w2_ref[...])

    @pl.when(j == pl.num_programs(1) - 1)
    def _():
        xo_ref[...] = x_ref[...] + gate_ref[0] * acc_ref[...]


def _ffn(h, xf, gate, w1, w3, w2, S, tn):
    T, D = xf.shape
    F = w1.shape[1]
    tm = TOKEN_TILE
    nps = S // tm
    return pl.pallas_call(
        _ffn_kernel, grid=(T // tm, F // tn),
        in_specs=[pl.BlockSpec((tm, D), lambda i, j: (i, 0)),
                  pl.BlockSpec((tm, D), lambda i, j: (i, 0)),
                  pl.BlockSpec((1, 1, D), lambda i, j: (i // nps, 0, 0)),
                  pl.BlockSpec((D, tn), lambda i, j: (0, j)),
                  pl.BlockSpec((D, tn), lambda i, j: (0, j)),
                  pl.BlockSpec((tn, D), lambda i, j: (j, 0))],
        out_specs=pl.BlockSpec((tm, D), lambda i, j: (i, 0)),
        out_shape=jax.ShapeDtypeStruct((T, D), F32),
        scratch_shapes=[pltpu.VMEM((tm, D), F32)],
        name="ffn")(h, xf, gate, w1, w3, w2)


def _nsa_proj_kernel(x_ref, sh_ref, sc_ref, gn_ref, wq_ref, wkv_ref, wg_ref, rc_ref, ra_ref, rb_ref,
                     q_ref, kc_ref, vc_ref, ks_ref, vs_ref, kw_ref, vw_ref, g_ref):
    hb = _modulate(x_ref[...], gn_ref[...], sc_ref[0], sh_ref[0]).astype(BF16)
    rc, ra, rb = rc_ref[...], ra_ref[...], rb_ref[...]

    def roped(v):
        return jnp.concatenate([_rope(v[:, :128], rc, ra, rb), _rope(v[:, 128:], rc, ra, rb)], axis=1)

    for p in range(NSA_HEADS // 2):
        qv = (roped(_dot(hb, wq_ref[:, p * 256:(p + 1) * 256])) * ATT_SCALE).astype(BF16)
        q_ref[2 * p] = qv[:, :128]
        q_ref[2 * p + 1] = qv[:, 128:]
    outs = (kc_ref, vc_ref, ks_ref, vs_ref, kw_ref, vw_ref)
    for n, o_ref in enumerate(outs):
        v = _dot(hb, wkv_ref[:, n * 256:(n + 1) * 256])
        o_ref[...] = (roped(v) if n % 2 == 0 else v).astype(BF16)
    g_ref[...] = _sigmoid(_dot(hb, wg_ref[...]))


def _nsa_proj(xf, shift, scale, gn, wq, wkv, wg, rope, S):
    T, D = xf.shape
    tm = TOKEN_TILE
    nps = S // tm
    row = lambda i: (i, 0)
    bat = lambda i: (i // nps, 0, 0)
    pos = lambda i: (i % nps, 0)
    cst2 = lambda i: (0, 0)
    kv_spec = pl.BlockSpec((tm, 256), row)
    kv_shape = jax.ShapeDtypeStruct((T, 256), BF16)
    return pl.pallas_call(
        _nsa_proj_kernel, grid=(T // tm,),
        in_specs=[pl.BlockSpec((tm, D), row),
                  pl.BlockSpec((1, 1, D), bat), pl.BlockSpec((1, 1, D), bat),
                  pl.BlockSpec((1, D), cst2),
                  pl.BlockSpec(wq.shape, cst2), pl.BlockSpec(wkv.shape, cst2), pl.BlockSpec(wg.shape, cst2),
                  pl.BlockSpec((tm, LANES), pos), pl.BlockSpec((tm, LANES), pos),
                  pl.BlockSpec((tm, LANES), pos)],
        out_specs=[pl.BlockSpec((NSA_HEADS, tm, LANES), lambda i: (0, i, 0))] + [kv_spec] * 6
                  + [pl.BlockSpec((tm, LANES), row)],
        out_shape=[jax.ShapeDtypeStruct((NSA_HEADS, T, LANES), BF16)] + [kv_shape] * 6
                  + [jax.ShapeDtypeStruct((T, LANES), F32)],
        name="nsa_proj")(xf, shift, scale, gn, wq, wkv, wg, *rope)


def _compress_kernel(uk_ref, uv_ref, pe_ref, k1_ref, k2_ref, v1_ref, v2_ref, ko_ref, vo_ref):
    half = CMP_STRIDE * HEAD_DIM
    pe = jnp.broadcast_to(pe_ref[...], (8, 2 * half)).astype(BF16)

    def comp(u_ref, w1_ref, w2_ref):
        bias = _dot(pe, w1_ref[...])[0:1]
        out = None
        for gg in range(2):
            u = u_ref[0, gg]
            a = _dot(u, w1_ref[:half, :])
            b = _dot(u, w1_ref[half:, :])
            nrow = b.shape[0]
            hid = a + pltpu.roll(b, nrow - 1, 0) + bias
            hid = (hid * _sigmoid(hid)).astype(BF16)
            o = _dot(hid, w2_ref[gg])
            out = o if out is None else out + o
        return out
    ko_ref[0] = comp(uk_ref, k1_ref, k2_ref).astype(BF16)
    vo_ref[0] = comp(uv_ref, v1_ref, v2_ref).astype(BF16)


def _compress(uk, uv, pe, k1, k2, v1, v2):
    B, G, nch, W = uk.shape
    u_spec = pl.BlockSpec((1, 2, nch, W), lambda b, p: (b, p, 0, 0))
    w1_spec = pl.BlockSpec(k1.shape, lambda b, p: (0, 0))
    w2_spec = pl.BlockSpec(k2.shape, lambda b, p: (0, 0, 0))
    o_spec = pl.BlockSpec((1, nch, LANES), lambda b, p: (b, 0, p))
    o_shape = jax.ShapeDtypeStruct((B, nch, G * HEAD_DIM), BF16)
    return pl.pallas_call(
        _compress_kernel, grid=(B, G // 2),
        in_specs=[u_spec, u_spec, pl.BlockSpec(pe.shape, lambda b, p: (0, 0)),
                  w1_spec, w2_spec, w1_spec, w2_spec],
        out_specs=[o_spec, o_spec], out_shape=[o_shape, o_shape],
        name="compress")(uk, uv, pe, k1, k2, v1, v2)


def _nsa_attn_kernel(q_ref, g_ref, kcmp_ref, vcmp_ref, ks_ref, vs_ref, kw_ref, vw_ref,
                     covt_ref, exp_ref, o_ref, oc_ref, psum_ref, imp_ref, sel_ref, m_ref, l_ref, acc_ref,
                     osel_ref, *, n_sel):
    QB, KC, HPG = Q_BLOCK, NSA_KC, NSA_HPG
    GR = HPG * QB
    qi = pl.program_id(1)
    qs = qi * QB
    row_t = qs + lax.broadcasted_iota(jnp.int32, (QB, 1), 0)
    nch = kcmp_ref.shape[1]
    nb = covt_ref.shape[0]
    n_ch = qs // KC + 1
    gates = g_ref[...]
    pair = lambda g: slice((g // 2) * 128, (g // 2) * 128 + 128)
    rows = lambda g: slice(g * GR, (g + 1) * GR)
    q_grp = lambda g: q_ref[g * HPG:(g + 1) * HPG].reshape(GR, LANES)

    cmp_end = lax.broadcasted_iota(jnp.int32, (QB, nch), 1) * CMP_STRIDE + (CMP_LEN - 1)
    cbias = _tile_rows(jnp.where(cmp_end <= row_t, 0.0, -jnp.inf), HPG)
    for g in range(NSA_GROUPS):
        s = _dot_nt(q_grp(g), kcmp_ref[0, :, pair(g)]) + cbias
        m = jnp.max(s, axis=1, keepdims=True)
        m = jnp.where(m == -jnp.inf, 0.0, m)
        e = jnp.exp2(s - m)
        p = e / jnp.maximum(jnp.sum(e, axis=1, keepdims=True), 1e-30)
        oc_ref[rows(g), :] = _dot(p.astype(BF16), vcmp_ref[0, :, pair(g)])
        psum_ref[g] = p[0:QB] + p[QB:2 * QB] + p[2 * QB:3 * QB] + p[3 * QB:4 * QB]

    jb = lax.broadcasted_iota(jnp.int32, (nb, QB), 0)
    cur = (qs + lax.broadcasted_iota(jnp.int32, (nb, QB), 1)) // SEL_LEN
    forced = (jb == 0) | (jb == cur) | (jb == cur - 1)
    ri = lax.broadcasted_iota(jnp.int32, (QB, QB), 0)
    ci = lax.broadcasted_iota(jnp.int32, (QB, QB), 1)
    eye = jnp.where(ri == ci, 1.0, 0.0).astype(BF16)
    imps = []
    for g in range(NSA_GROUPS):
        imp = lax.dot_general(covt_ref[...], psum_ref[g], (((1,), (1,)), ((), ())),
                              precision=HIGHEST, preferred_element_type=F32)
        imp = jnp.where(forced, FORCE_SCORE, imp)
        imp = jnp.where(jb <= cur, imp, -jnp.inf)
        imp_ref[g] = imp
        imps.append(imp)

    def rank_body(i, ranks):
        before = jnp.where(i < jb, 1.0, 0.0)
        out = []
        for g in range(NSA_GROUPS):
            ri_ = imp_ref[g, pl.ds(i, 1), :]
            out.append(ranks[g] + jnp.where(ri_ > imps[g], 1.0, 0.0) + jnp.where(ri_ == imps[g], before, 0.0))
        return tuple(out)
    n_live = jnp.minimum((qs + QB - 1) // SEL_LEN + 1, nb)
    ranks = lax.fori_loop(0, n_live, rank_body, (jnp.zeros((nb, QB), F32),) * NSA_GROUPS)
    for g in range(NSA_GROUPS):
        selt = jnp.where(ranks[g] < n_sel, 1.0, 0.0).astype(BF16)
        sel_ref[g] = _dot_nt(eye, selt).astype(BF16)

    _flash_init(m_ref, l_ref, acc_ref)

    def sel_chunk(c, carry):
        k0 = pl.multiple_of(c * KC, KC)
        causal = (k0 + lax.broadcasted_iota(jnp.int32, (QB, KC), 1)) <= row_t
        for g in range(NSA_GROUPS):
            keep = (_dot(sel_ref[g], exp_ref[c]) > 0.5) & causal
            bias = _tile_rows(jnp.where(keep, 0.0, -jnp.inf), HPG)
            s = _dot_nt(q_grp(g), ks_ref[0, pl.ds(k0, KC), pair(g)]) + bias
            _flash_update(s, vs_ref[0, pl.ds(k0, KC), pair(g)], m_ref, l_ref, acc_ref, g * GR)
        return carry
    lax.fori_loop(0, n_ch, sel_chunk, 0)
    osel_ref[...] = acc_ref[...] / jnp.maximum(l_ref[...], 1e-30)

    wl = WINDOW + QB
    w0 = pl.multiple_of(jnp.maximum(qs - WINDOW, 0), QB)
    wpos = w0 + lax.broadcasted_iota(jnp.int32, (QB, wl), 1)
    wbias = _tile_rows(jnp.where((wpos <= row_t) & (wpos > row_t - WINDOW), 0.0, -jnp.inf), HPG)
    for g in range(NSA_GROUPS):
        s = _dot_nt(q_grp(g), kw_ref[0, pl.ds(w0, wl), pair(g)]) + wbias
        m = jnp.max(s, axis=1, keepdims=True)
        m = jnp.where(m == -jnp.inf, 0.0, m)
        e = jnp.exp2(s - m)
        den = jnp.maximum(jnp.sum(e, axis=1, keepdims=True), 1e-30)
        acc_ref[rows(g), :] = _dot(e.astype(BF16), vw_ref[0, pl.ds(w0, wl), pair(g)]) / den

    lane = lax.broadcasted_iota(jnp.int32, (QB, LANES), 1)
    pair_out = [None, None]
    for h in range(NSA_HEADS):
        g = h // HPG
        hr = slice(h * QB, (h + 1) * QB)
        o = (gates[:, 3 * h:3 * h + 1] * oc_ref[hr, :] + gates[:, 3 * h + 1:3 * h + 2] * osel_ref[hr, :]
             + gates[:, 3 * h + 2:3 * h + 3] * acc_ref[hr, :])
        pair_out[h % 2] = o
        if h % 2 == 1:
            if g % 2 == 0:
                both = jnp.where(lane < 64, pair_out[0], pltpu.roll(pair_out[1], 64, 1))
            else:
                both = jnp.where(lane < 64, pltpu.roll(pair_out[0], 64, 1), pair_out[1])
            o_ref[:, (h // 2) * 128:(h // 2) * 128 + 128] = both.astype(BF16)


def _nsa_attn(q, gates, kcmp, vcmp, ks, vs, kw, vw, covt, expand, B, S):
    T = B * S
    nq = S // Q_BLOCK
    nch = kcmp.shape[1]
    nb = S // SEL_LEN
    row = lambda b, i: (b * nq + i, 0)
    bat = lambda b, i: (b, 0, 0)
    kv = lambda a: a.reshape(B, S, 256)
    kv_spec = pl.BlockSpec((1, S, 256), bat)
    cmp_spec = pl.BlockSpec((1, nch, 256), bat)
    kern = functools.partial(_nsa_attn_kernel, n_sel=min(SEL_BLOCKS, nb))
    return pl.pallas_call(
        kern, grid=(B, nq),
        in_specs=[pl.BlockSpec((NSA_HEADS, Q_BLOCK, LANES), lambda b, i: (0, b * nq + i, 0)),
                  pl.BlockSpec((Q_BLOCK, LANES), row),
                  cmp_spec, cmp_spec, kv_spec, kv_spec, kv_spec, kv_spec,
                  pl.BlockSpec(covt.shape, lambda b, i: (0, 0)),
                  pl.BlockSpec(expand.shape, lambda b, i: (0, 0, 0))],
        out_specs=pl.BlockSpec((Q_BLOCK, NSA_HEADS * HEAD_DIM), row),
        out_shape=jax.ShapeDtypeStruct((T, NSA_HEADS * HEAD_DIM), BF16),
        scratch_shapes=[pltpu.VMEM((NSA_HEADS * Q_BLOCK, LANES), F32),
                        pltpu.VMEM((NSA_GROUPS, Q_BLOCK, nch), F32),
                        pltpu.VMEM((NSA_GROUPS, nb, Q_BLOCK), F32),
                        pltpu.VMEM((NSA_GROUPS, Q_BLOCK, nb), BF16),
                        pltpu.VMEM((NSA_HEADS * Q_BLOCK, LANES), F32),
                        pltpu.VMEM((NSA_HEADS * Q_BLOCK, LANES), F32),
                        pltpu.VMEM((NSA_HEADS * Q_BLOCK, LANES), F32),
                        pltpu.VMEM((NSA_HEADS * Q_BLOCK, LANES), F32)],
        name="nsa_attn")(q, gates, kcmp, vcmp, kv(ks), kv(vs), kv(kw), kv(vw), covt, expand)


def _router_kernel(h_ref, wr_ref, route_ref, cnt_ref, carry_ref):
    i = pl.program_id(0)
    tm = h_ref.shape[0]

    @pl.when(i == 0)
    def _():
        carry_ref[...] = jnp.zeros_like(carry_ref)
    lane = lax.broadcasted_iota(jnp.int32, (tm, LANES), 1).astype(F32)
    lg = jnp.dot(h_ref[...], wr_ref[...], precision=HIGHEST, preferred_element_type=F32)
    lg = jnp.where(lane < N_EXPERTS, lg, -jnp.inf)
    v1 = jnp.max(lg, axis=1, keepdims=True)
    i1 = jnp.min(jnp.where(lg == v1, lane, float(LANES)), axis=1, keepdims=True)
    lg2 = jnp.where(lane == i1, -jnp.inf, lg)
    v2 = jnp.max(lg2, axis=1, keepdims=True)
    i2 = jnp.min(jnp.where(lg2 == v2, lane, float(LANES)), axis=1, keepdims=True)
    e2 = jnp.exp(v2 - v1)
    g1 = 1.0 / (1.0 + e2)
    g2 = e2 / (1.0 + e2)
    oh1 = jnp.where(lane == i1, 1.0, 0.0)
    oh2 = jnp.where(lane == i2, 1.0, 0.0)
    both = oh1 + oh2
    ri = lax.broadcasted_iota(jnp.int32, (tm, tm), 0)
    ci = lax.broadcasted_iota(jnp.int32, (tm, tm), 1)
    lower = jnp.where(ri > ci, 1.0, 0.0).astype(BF16)
    tot = carry_ref[0:1, :] + _dot(lower, both.astype(BF16))
    r1 = jnp.sum(oh1 * tot, axis=1, keepdims=True)
    r2 = jnp.sum(oh2 * tot, axis=1, keepdims=True)
    new_carry = carry_ref[...] + jnp.sum(both, axis=0, keepdims=True)
    carry_ref[...] = new_carry
    cnt_ref[...] = new_carry
    out = jnp.zeros((tm, LANES), F32)
    for col, val in enumerate((i1, i2, g1, g2, r1, r2)):
        out = jnp.where(lane == col, val, out)
    route_ref[...] = out


def _router(h, wr):
    T, D = h.shape
    tm = TOKEN_TILE
    return pl.pallas_call(
        _router_kernel, grid=(T // tm,),
        in_specs=[pl.BlockSpec((tm, D), lambda i: (i, 0)), pl.BlockSpec(wr.shape, lambda i: (0, 0))],
        out_specs=[pl.BlockSpec((tm, LANES), lambda i: (i, 0)), pl.BlockSpec((8, LANES), lambda i: (0, 0))],
        out_shape=[jax.ShapeDtypeStruct((T, LANES), F32), jax.ShapeDtypeStruct((8, LANES), F32)],
        scratch_shapes=[pltpu.VMEM((8, LANES), F32)],
        name="router")(h, wr)


def _row_copy(src, dst, sem):
    return pltpu.make_async_copy(src, dst, sem)


def _moe_ffn_kernel(be_ref, nu_ref, src_ref, dst_ref, h_ref, w1_ref, w3_ref, w2_ref, y_ref,
                    xin_ref, xb_ref, acc_ref, yout_ref, sem_in, sem_out, *, nj):
    i = pl.program_id(0)
    j = pl.program_id(1)
    nblk = src_ref.shape[0]
    per_step = MOE_BLOCK // nj
    used = i < nu_ref[0]
    slot = i % 2
    nxt = jnp.minimum(i + 1, nblk - 1)
    prv = jnp.maximum(i - 1, 0)

    def gather(blk, r, buf):
        return _row_copy(h_ref.at[pl.ds(src_ref[blk, r], 1), :], xin_ref.at[buf, pl.ds(r, 1), :],
                         sem_in.at[buf])

    def scatter(blk, r):
        return _row_copy(yout_ref.at[pl.ds(r, 1), :], y_ref.at[pl.ds(dst_ref[blk, r], 1), :], sem_out)

    def issue(r0, n):
        for u in range(n):
            gather(nxt, r0 + u, 1 - slot).start()
            scatter(prv, r0 + u).start()

    def wait_gathered(buf):
        _row_copy(h_ref.at[pl.ds(0, MOE_BLOCK), :], xin_ref.at[buf], sem_in.at[buf]).wait()

    def wait_scattered():
        _row_copy(yout_ref, y_ref.at[pl.ds(0, MOE_BLOCK), :], sem_out).wait()

    @pl.when((i == 0) & (j == 0))
    def _():
        yout_ref[...] = jnp.zeros_like(yout_ref)

        def first(r, carry):
            gather(0, r, 0).start()
            return carry
        lax.fori_loop(0, MOE_BLOCK, first, 0)

    @pl.when(j == 0)
    def _():
        wait_gathered(slot)
        xb_ref[...] = xin_ref[slot].astype(BF16)
        acc_ref[...] = jnp.zeros_like(acc_ref)

    @pl.when(used)
    def _():
        issue(j * per_step, per_step)
        xb = xb_ref[...]
        a = _dot(xb, w1_ref[0])
        b = _dot(xb, w3_ref[0])
        acc_ref[...] += _dot((a * _sigmoid(a) * b).astype(BF16), w2_ref[0])

    @pl.when(jnp.logical_not(used))
    def _():
        issue(j * per_step, per_step)

    @pl.when(j == nj - 1)
    def _():
        def rest(r, carry):
            gather(nxt, r, 1 - slot).start()
            scatter(prv, r).start()
            return carry
        lax.fori_loop(nj * per_step, MOE_BLOCK, rest, 0)
        wait_scattered()
        yout_ref[...] = acc_ref[...]

    @pl.when((i == pl.num_programs(0) - 1) & (j == nj - 1))
    def _():
        wait_gathered(1 - slot)


def _moe_ffn(block_e, n_used, slot_src, slot_dst, h, w1, w3, w2, tn):
    T, D = h.shape
    nblk = slot_src.shape[0]
    E, _, F = w1.shape
    nj = F // tn
    bi = lambda i: jnp.minimum(i, nblk - 1)
    jj = lambda i, j, nu: jnp.where(i < nu[0], j, nj - 1)
    return pl.pallas_call(
        functools.partial(_moe_ffn_kernel, nj=nj),
        grid_spec=pltpu.PrefetchScalarGridSpec(
            num_scalar_prefetch=4, grid=(nblk + 1, nj),
            in_specs=[pl.BlockSpec(memory_space=pl.ANY),
                      pl.BlockSpec((1, D, tn), lambda i, j, be, nu, s, d: (be[bi(i)], 0, jj(i, j, nu))),
                      pl.BlockSpec((1, D, tn), lambda i, j, be, nu, s, d: (be[bi(i)], 0, jj(i, j, nu))),
                      pl.BlockSpec((1, tn, D), lambda i, j, be, nu, s, d: (be[bi(i)], jj(i, j, nu), 0))],
            out_specs=pl.BlockSpec(memory_space=pl.ANY),
            scratch_shapes=[pltpu.VMEM((2, MOE_BLOCK, D), F32), pltpu.VMEM((MOE_BLOCK, D), BF16),
                            pltpu.VMEM((MOE_BLOCK, D), F32), pltpu.VMEM((MOE_BLOCK, D), F32),
                            pltpu.SemaphoreType.DMA((2,)), pltpu.SemaphoreType.DMA(())]),
        out_shape=jax.ShapeDtypeStruct((2 * T + MOE_BLOCK, D), F32),
        name="moe_ffn")(block_e, n_used, slot_src, slot_dst, h, w1, w3, w2)


def _moe_combine_kernel(y0_ref, y1_ref, x_ref, gate_ref, route_ref, fn_ref, o_ref):
    route = route_ref[...]
    y = route[:, 2:3] * y0_ref[...] + route[:, 3:4] * y1_ref[...]
    o_ref[...] = _rms(x_ref[...] + gate_ref[0] * y, fn_ref[...])


def _moe_combine(y, xf, gate, route, fn, S):
    T, D = xf.shape
    tm = TOKEN_TILE
    nps = S // tm
    nt = T // tm
    return pl.pallas_call(
        _moe_combine_kernel, grid=(nt,),
        in_specs=[pl.BlockSpec((tm, D), lambda i: (i, 0)),
                  pl.BlockSpec((tm, D), lambda i: (nt + i, 0)),
                  pl.BlockSpec((tm, D), lambda i: (i, 0)),
                  pl.BlockSpec((1, 1, D), lambda i: (i // nps, 0, 0)),
                  pl.BlockSpec((tm, LANES), lambda i: (i, 0)),
                  pl.BlockSpec((1, D), lambda i: (0, 0))],
        out_specs=pl.BlockSpec((tm, D), lambda i: (i, 0)),
        out_shape=jax.ShapeDtypeStruct((T, D), F32), name="moe_combine")(y, y, xf, gate, route, fn)


def _rope_tables(S):
    inv = ROPE_THETA ** (-jnp.arange(0, ROPE_DIM, 2, dtype=F32) / ROPE_DIM)
    ang = jnp.arange(S, dtype=F32)[:, None] * inv[None, :]
    cos, sin = jnp.cos(ang), jnp.sin(ang)
    pm = np.arange(LANES) % HEAD_DIM
    col = pm % (ROPE_DIM // 2)
    rc = jnp.where((pm < ROPE_DIM)[None, :], cos[:, col], 1.0)
    ra = jnp.where((pm < ROPE_DIM // 2)[None, :], -sin[:, col], 0.0)
    rb = jnp.where(((pm >= ROPE_DIM // 2) & (pm < ROPE_DIM))[None, :], sin[:, col], 0.0)
    return rc, ra, rb


def _dsa_weights(w_in, w_uk, w_iq):
    D = w_in.shape[0]
    a, b, c, d = DSA_Q_LORA, DSA_Q_LORA + DSA_KV_LORA, DSA_Q_LORA + DSA_KV_LORA + ROPE_DIM, \
        DSA_Q_LORA + DSA_KV_LORA + ROPE_DIM + IDX_DIM
    win = jnp.concatenate([w_in[:, :b], w_in[:, c:d], w_in[:, b:c], w_in[:, d:],
                           jnp.zeros((D, 512 - w_in.shape[1]), F32)], axis=1).astype(BF16)
    H = DSA_HEADS
    blk = jnp.zeros((H, HEAD_DIM, 256), F32)
    blk = blk.at[:, ROPE_DIM:, :DSA_KV_LORA].set(jnp.transpose(w_uk, (0, 2, 1)))
    blk = blk.at[:, :ROPE_DIM, 192:192 + ROPE_DIM].set(jnp.eye(ROPE_DIM, dtype=F32))
    z = jnp.zeros((H // 2, HEAD_DIM, 256), F32)
    wcat = jnp.concatenate([jnp.concatenate([blk[0::2], z], axis=2),
                            jnp.concatenate([z, blk[1::2]], axis=2)], axis=1).astype(BF16)
    wiq = w_iq.reshape(DSA_Q_LORA, IDX_HEADS, IDX_DIM)
    wiq = jnp.concatenate([wiq, jnp.zeros_like(wiq)], axis=2).reshape(DSA_Q_LORA, IDX_HEADS * 128)
    return win, wcat, wiq.astype(BF16)


def _nsa_weights(w_in):
    D = w_in.shape[0]
    nq = NSA_HEADS * HEAD_DIM
    wq = w_in[:, :nq].reshape(D, NSA_HEADS, HEAD_DIM)
    z = jnp.zeros_like(wq)
    odd = ((np.arange(NSA_HEADS) // NSA_HPG) % 2 == 1)[None, :, None]
    wq = jnp.concatenate([jnp.where(odd, z, wq), jnp.where(odd, wq, z)], axis=2).reshape(D, NSA_HEADS * 128)
    wkv = w_in[:, nq:nq + 6 * 256]
    wg = jnp.concatenate([w_in[:, nq + 6 * 256:], jnp.zeros((D, LANES - 3 * NSA_HEADS), F32)], axis=1)
    return wq.astype(BF16), wkv.astype(BF16), wg.astype(BF16)


def _nsa_tables(S):
    nch = S // CMP_STRIDE
    nc = (S - CMP_LEN) // CMP_STRIDE + 1
    nb = S // SEL_LEN
    cstart = np.arange(nch) * CMP_STRIDE
    bstart = np.arange(nb) * SEL_LEN
    cov = ((cstart[None, :] < bstart[:, None] + SEL_LEN) & (cstart[None, :] + CMP_LEN > bstart[:, None])
           & (np.arange(nch)[None, :] < nc)).astype(np.float32)
    kpos = np.arange(S).reshape(S // NSA_KC, 1, NSA_KC)
    expand = (kpos // SEL_LEN == np.arange(nb)[None, :, None]).astype(np.float32)
    return jnp.asarray(cov), jnp.asarray(expand, dtype=BF16)


def _slot_tables(dest, T, n_blocks):
    ns = n_blocks * MOE_BLOCK
    asg = jnp.full((ns,), -1, jnp.int32).at[dest].set(jnp.arange(2 * T, dtype=jnp.int32))
    real = asg >= 0
    src = jnp.where(real, asg // 2, 0)
    spare = 2 * T + jnp.arange(ns, dtype=jnp.int32) % MOE_BLOCK
    dst = jnp.where(real, (asg % 2) * T + asg // 2, spare)
    return src.reshape(n_blocks, MOE_BLOCK), dst.reshape(n_blocks, MOE_BLOCK)


def _chunk_tokens(a, B, S):
    a = a.reshape(B, S // CMP_STRIDE, CMP_STRIDE, NSA_GROUPS, HEAD_DIM)
    return jnp.transpose(a, (0, 3, 1, 2, 4)).reshape(B, NSA_GROUPS, S // CMP_STRIDE, CMP_STRIDE * HEAD_DIM)


def kernel(x, c, norm_mix, norm_ffn, ada_w, ada_b, final_norm, dsa_w_in, dsa_g_q, dsa_w_uq, dsa_g_kv,
           dsa_w_uk, dsa_w_uv, dsa_w_iq, dsa_w_o, ffn_w1, ffn_w3, ffn_w2, nsa_w_in, nsa_cmp_pe,
           nsa_cmp_k1, nsa_cmp_k2, nsa_cmp_v1, nsa_cmp_v2, nsa_w_o, moe_router, moe_w1, moe_w3, moe_w2):
    B, S, D = x.shape
    T = B * S
    xf = x.reshape(T, D)
    mods = _ada(c, ada_w, ada_b).reshape(4, B, 3, 1, D)
    shift = lambda s: mods[s, :, 0]
    scale = lambda s: mods[s, :, 1]
    gate = lambda s: mods[s, :, 2]
    rope = _rope_tables(S)

    win, wcat, wiq = _dsa_weights(dsa_w_in[0], dsa_w_uk[0], dsa_w_iq[0])
    qcat, iq, kcat, ik, iw = _dsa_proj(
        xf, shift(0), scale(0), norm_mix[0:1], win, dsa_g_q[0:1], dsa_g_kv[0:1],
        dsa_w_uq[0].astype(BF16), wiq, wcat, rope, S)
    olat = _dsa_attn(qcat, iq, iw, kcat, ik, B, S)
    wuvo = _wuvo(dsa_w_uv[0], dsa_w_o[0])
    x1, h1 = _post(olat, wuvo, xf, gate(0), norm_ffn[0:1], scale(1), shift(1), S, BF16)
    x2 = _ffn(h1, x1, gate(1), ffn_w1[0].astype(BF16), ffn_w3[0].astype(BF16), ffn_w2[0].astype(BF16),
              S, ffn_w1.shape[2] // 2)

    wq, wkv, wg = _nsa_weights(nsa_w_in[0])
    q, kc, vc, ks, vs, kw, vw, gates = _nsa_proj(xf=x2, shift=shift(2), scale=scale(2), gn=norm_mix[1:2],
                                                 wq=wq, wkv=wkv, wg=wg, rope=rope, S=S)
    zpad = jnp.zeros((CMP_HIDDEN, HEAD_DIM), F32)
    pad2 = lambda w2: jnp.stack([jnp.concatenate([w2, zpad], axis=1),
                                 jnp.concatenate([zpad, w2], axis=1)]).astype(BF16)
    kcmp, vcmp = _compress(_chunk_tokens(kc, B, S), _chunk_tokens(vc, B, S),
                           nsa_cmp_pe[0].reshape(1, CMP_LEN * HEAD_DIM),
                           nsa_cmp_k1[0].astype(BF16), pad2(nsa_cmp_k2[0]),
                           nsa_cmp_v1[0].astype(BF16), pad2(nsa_cmp_v2[0]))
    covt, expand = _nsa_tables(S)
    o = _nsa_attn(q, gates, kcmp, vcmp, ks, vs, kw, vw, covt, expand, B, S)
    x3, h3 = _post(o, nsa_w_o[0].astype(BF16), x2, gate(2), norm_ffn[1:2], scale(3), shift(3), S, F32)

    wr = jnp.concatenate([moe_router[0], jnp.zeros((D, LANES - N_EXPERTS), F32)], axis=1)
    route, cnt = _router(h3, wr)
    counts = cnt[0, :N_EXPERTS].astype(jnp.int32)
    padded = (counts + MOE_BLOCK - 1) // MOE_BLOCK * MOE_BLOCK
    ends = jnp.cumsum(padded)
    pstart = ends - padded
    eidx = route[:, 0:2].astype(jnp.int32)
    dest = (pstart[eidx] + route[:, 4:6].astype(jnp.int32)).reshape(-1)
    n_blocks = -(-(T * 2) // MOE_BLOCK) + N_EXPERTS
    block_start = jnp.arange(n_blocks, dtype=jnp.int32) * MOE_BLOCK
    block_e = jnp.minimum(jnp.sum((ends[None, :] <= block_start[:, None]).astype(jnp.int32), axis=1),
                          N_EXPERTS - 1)
    n_used = (ends[-1:] // MOE_BLOCK).astype(jnp.int32)
    slot_src, slot_dst = _slot_tables(dest, T, n_blocks)
    y = _moe_ffn(block_e, n_used, slot_src, slot_dst, h3, moe_w1[0].astype(BF16), moe_w3[0].astype(BF16),
                 moe_w2[0].astype(BF16), 512)
    out = _moe_combine(y, x3, gate(3), route, final_norm.reshape(1, D), S)
    return out.reshape(B, S, D)
```

```python
import functools

import numpy as np
import jax
import jax.numpy as jnp
from jax import lax
from jax.experimental import pallas as pl
from jax.experimental.pallas import tpu as pltpu

F32 = jnp.float32
BF16 = jnp.bfloat16
HIGHEST = lax.Precision.HIGHEST
INT_MIN = -2147483648

HEAD_DIM = 64
ROPE_DIM = 16
ROPE_THETA = 500000.0
Q_BLOCK = 128
NORM_EPS = 1e-6

DSA_HEADS = 16
DSA_NOPE = 48
DSA_Q_LORA = 256
DSA_KV_LORA = 128
IDX_HEADS = 8
IDX_DIM = 64
DSA_TOPK = 256
DSA_KC = 512

NSA_HEADS = 16
NSA_GROUPS = 4
NSA_HPG = 4
CMP_LEN = 32
CMP_STRIDE = 16
CMP_HIDDEN = 256
SEL_LEN = 64
SEL_BLOCKS = 16
WINDOW = 512
FORCE_SCORE = 1e4
NSA_KC = 512

N_EXPERTS = 8
MOE_BLOCK = 512
TOKEN_TILE = 512
ROW_TILE = 256
LANES = 128

LOG2E = 1.4426950408889634
ATT_SCALE = HEAD_DIM ** -0.5 * LOG2E


def _dot(a, b):
    return jnp.dot(a, b, preferred_element_type=F32)


def _dot_nt(a, b):
    return lax.dot_general(a, b, (((1,), (1,)), ((), ())), preferred_element_type=F32)


def _sigmoid(v):
    return 1.0 / (1.0 + jnp.exp(-v))


def _rms(v, g):
    return v * lax.rsqrt(jnp.mean(v * v, axis=-1, keepdims=True) + NORM_EPS) * g


def _modulate(v, g, scale, shift):
    return _rms(v, g) * (1.0 + scale) + shift


def _rope(v, c, sa, sb):
    return v * c + pltpu.roll(v, LANES - 8, 1) * sa + pltpu.roll(v, 8, 1) * sb


def _tile_rows(a, n):
    return jnp.concatenate([a] * n, axis=0) if n > 1 else a


def _tile_lanes(a, n):
    return jnp.concatenate([a] * n, axis=1) if n > 1 else a


def _paired_loop(n, body):
    def two(i, carry):
        return body(2 * i + 1, body(2 * i, carry))
    lax.fori_loop(0, n // 2, two, 0)

    @pl.when(n % 2 == 1)
    def _():
        body(n - 1, 0)


def _flash_init(m_ref, l_ref, acc_ref):
    m_ref[...] = jnp.full(m_ref.shape, -1e30, F32)
    l_ref[...] = jnp.zeros(l_ref.shape, F32)
    acc_ref[...] = jnp.zeros(acc_ref.shape, F32)


def _flash_update(s, v, m_ref, l_ref, acc_ref, row0):
    n = s.shape[0]
    ps = []
    for r in range(0, n, Q_BLOCK):
        rs = slice(row0 + r, row0 + r + Q_BLOCK)
        sl = s[r:r + Q_BLOCK]
        m_old = m_ref[rs, :]
        m_new = jnp.maximum(m_old, jnp.max(sl, axis=1, keepdims=True))
        p = jnp.exp2(sl - _tile_lanes(m_new, sl.shape[1] // LANES))
        alpha = jnp.exp2(m_old - m_new)
        l_ref[rs, :] = alpha * l_ref[rs, :] + jnp.sum(p, axis=1, keepdims=True)
        acc_ref[rs, :] = alpha * acc_ref[rs, :]
        m_ref[rs, :] = m_new
        ps.append(p.astype(BF16))
    pv = _dot(jnp.concatenate(ps, axis=0), v)
    acc_ref[row0:row0 + n, :] += pv


def _ada_kernel(c_ref, w_ref, b_ref, o_ref):
    cv = c_ref[...]
    sc = cv * _sigmoid(cv)
    o_ref[0] = jnp.dot(sc, w_ref[0], precision=HIGHEST, preferred_element_type=F32) + b_ref[0]


def _ada(c, ada_w, ada_b):
    B, D = c.shape
    w = ada_w.reshape(4, D, 3 * D)
    b = ada_b.reshape(4, 1, 3 * D)
    return pl.pallas_call(
        _ada_kernel, grid=(4, 3),
        in_specs=[pl.BlockSpec((B, D), lambda l, j: (0, 0)),
                  pl.BlockSpec((1, D, D), lambda l, j: (l, 0, j)),
                  pl.BlockSpec((1, 1, D), lambda l, j: (l, 0, j))],
        out_specs=pl.BlockSpec((1, B, D), lambda l, j: (l, 0, j)),
        out_shape=jax.ShapeDtypeStruct((4, B, 3 * D), F32), name="ada")(c, w, b)


def _dsa_proj_kernel(x_ref, sh_ref, sc_ref, gn_ref, win_ref, gq_ref, gkv_ref, wuq_ref, wiq_ref,
                     wcat_ref, rc_ref, ra_ref, rb_ref, qcat_ref, iq_ref, kcat_ref, ik_ref, iw_ref):
    h = _modulate(x_ref[...], gn_ref[...], sc_ref[0], sh_ref[0])
    proj = _dot(h.astype(BF16), win_ref[...])
    q_lat = _rms(proj[:, :256], gq_ref[...]).astype(BF16)
    c_kv = _rms(proj[:, 256:384], gkv_ref[...])
    rc, ra, rb = rc_ref[...], ra_ref[...], rb_ref[...]
    rest = _rope(proj[:, 384:512], rc, ra, rb)
    lane = lax.broadcasted_iota(jnp.int32, rest.shape, 1)
    kcat_ref[:, :128] = c_kv.astype(BF16)
    kcat_ref[:, 128:] = jnp.where((lane >= 64) & (lane < 80), rest, 0.0).astype(BF16)
    ik_ref[...] = jnp.where(lane < 64, rest, 0.0).astype(BF16)
    iw_ref[...] = rest * (IDX_HEADS ** -0.5)
    q = _dot(q_lat, wuq_ref[...])
    for p in range(DSA_HEADS // 2):
        qp = (_rope(q[:, p * 128:(p + 1) * 128], rc, ra, rb) * ATT_SCALE).astype(BF16)
        res = _dot(qp, wcat_ref[p]).astype(BF16)
        qcat_ref[2 * p] = res[:, :256]
        qcat_ref[2 * p + 1] = res[:, 256:]
    iqv = _dot(q_lat, wiq_ref[...])
    for hh in range(IDX_HEADS):
        iq_ref[hh] = (_rope(iqv[:, hh * 128:(hh + 1) * 128], rc, ra, rb) * (IDX_DIM ** -0.5)).astype(BF16)


def _dsa_proj(xf, shift, scale, gn, win, gq, gkv, wuq, wiq, wcat, rope, S):
    T, D = xf.shape
    tm = TOKEN_TILE
    nps = S // tm
    row = lambda i: (i, 0)
    bat = lambda i: (i // nps, 0, 0)
    pos = lambda i: (i % nps, 0)
    cst2 = lambda i: (0, 0)
    cst3 = lambda i: (0, 0, 0)
    return pl.pallas_call(
        _dsa_proj_kernel, grid=(T // tm,),
        in_specs=[pl.BlockSpec((tm, D), row),
                  pl.BlockSpec((1, 1, D), bat), pl.BlockSpec((1, 1, D), bat),
                  pl.BlockSpec((1, D), cst2),
                  pl.BlockSpec(win.shape, cst2),
                  pl.BlockSpec((1, DSA_Q_LORA), cst2), pl.BlockSpec((1, DSA_KV_LORA), cst2),
                  pl.BlockSpec(wuq.shape, cst2), pl.BlockSpec(wiq.shape, cst2),
                  pl.BlockSpec(wcat.shape, cst3),
                  pl.BlockSpec((tm, LANES), pos), pl.BlockSpec((tm, LANES), pos),
                  pl.BlockSpec((tm, LANES), pos)],
        out_specs=[pl.BlockSpec((DSA_HEADS, tm, 256), lambda i: (0, i, 0)),
                   pl.BlockSpec((IDX_HEADS, tm, LANES), lambda i: (0, i, 0)),
                   pl.BlockSpec((tm, 256), row),
                   pl.BlockSpec((tm, LANES), row),
                   pl.BlockSpec((tm, LANES), row)],
        out_shape=[jax.ShapeDtypeStruct((DSA_HEADS, T, 256), BF16),
                   jax.ShapeDtypeStruct((IDX_HEADS, T, LANES), BF16),
                   jax.ShapeDtypeStruct((T, 256), BF16),
                   jax.ShapeDtypeStruct((T, LANES), BF16),
                   jax.ShapeDtypeStruct((T, LANES), F32)],
        name="dsa_proj")(xf, shift, scale, gn, win, gq, gkv, wuq, wiq, wcat, *rope)


def _dsa_attn_kernel(q_ref, iq_ref, iw_ref, kcat_ref, ik_ref, o_ref, keys_ref, w_ref, m_ref, l_ref, acc_ref,
                     thr_ref, cnt_ref, *, k_sel, idx_bits):
    QB, KC, H = Q_BLOCK, DSA_KC, DSA_HEADS
    qi = pl.program_id(1)
    n_ch = (qi * QB) // KC + 1
    row_tl = qi * QB + lax.broadcasted_iota(jnp.int32, (QB, LANES), 0)
    iw = iw_ref[...]
    for hh in range(IDX_HEADS):
        w_ref[hh * QB:(hh + 1) * QB, :] = jnp.broadcast_to(iw[:, 80 + hh:81 + hh], (QB, KC))
    iq_all = iq_ref[...].reshape(IDX_HEADS * QB, LANES)

    def score_chunk(c, carry):
        k0 = pl.multiple_of(c * KC, KC)
        ikc = ik_ref[0, pl.ds(k0, KC), :]
        r = jnp.maximum(_dot_nt(iq_all, ikc), 0.0) * w_ref[...]
        sc = r[0:QB]
        for hh in range(1, IDX_HEADS):
            sc = sc + r[hh * QB:(hh + 1) * QB]
        sc = sc + 0.0
        bits = pltpu.bitcast(sc, jnp.int32)
        key = jnp.where(bits < 0, bits ^ 0x7FFFFFFF, bits)
        pos = k0 + lax.broadcasted_iota(jnp.int32, (QB, KC), 1)
        keys_ref[c] = jnp.where(pos <= _tile_lanes(row_tl, KC // LANES), key, INT_MIN)
        return carry
    _paired_loop(n_ch, score_chunk)

    lane_l = lax.broadcasted_iota(jnp.int32, (QB, LANES), 1)

    def count(pred):
        def body(c, a):
            for j in range(KC // LANES):
                kk = keys_ref[c, :, j * LANES:(j + 1) * LANES]
                a = a + jnp.where(pred(kk, c * KC + j * LANES + lane_l), 1.0, 0.0)
            return a
        a = lax.fori_loop(0, n_ch, body, jnp.zeros((QB, LANES), F32))
        return jnp.sum(a, axis=1, keepdims=True)

    def count3(c1, c2, c3):
        def body(c, a):
            for j in range(KC // LANES):
                kk = keys_ref[c, :, j * LANES:(j + 1) * LANES]
                a = a + jnp.where(kk >= c3, 4161, jnp.where(kk >= c2, 65, jnp.where(kk >= c1, 1, 0)))
            return a
        a = lax.fori_loop(0, n_ch, body, jnp.zeros((QB, LANES), jnp.int32))
        lane_sum = lambda x: jnp.sum(x.astype(F32), axis=1, keepdims=True)
        return lane_sum(a & 63), lane_sum((a >> 6) & 63), lane_sum(a >> 12)

    def bit_body(it, carry):
        thr, cnt = carry
        b1 = jnp.left_shift(jnp.int32(1), 31 - 2 * it)
        b0 = jnp.left_shift(jnp.int32(1), 30 - 2 * it)
        c1, c2 = thr ^ b0, thr ^ b1
        c3 = c2 ^ b0
        n1, n2, n3 = count3(c1, c2, c3)
        pick = lambda x3, x2, x1, x0: jnp.where(n3 >= k_sel, x3, jnp.where(n2 >= k_sel, x2,
                                                                           jnp.where(n1 >= k_sel, x1, x0)))
        return pick(c3, c2, c1, thr), pick(n3, n2, n1, cnt)

    def sweeps(first, n):
        thr, cnt = lax.fori_loop(first, first + n, bit_body, (thr_ref[...], cnt_ref[...]))
        thr_ref[...] = thr
        cnt_ref[...] = cnt
    thr_ref[...] = jnp.full((QB, LANES), INT_MIN, jnp.int32)
    cnt_ref[...] = jnp.full((QB, LANES), -1.0, F32)
    sweeps(0, 10)
    few = (row_tl + 1) < k_sel
    for stage in range(3):
        unsettled = jnp.where((cnt_ref[...] == k_sel) | few, 0.0, 1.0)

        @pl.when(jnp.max(unsettled) > 0.0)
        def _():
            sweeps(10 + 2 * stage, 2)
    thr = thr_ref[...]

    n_gt = count(lambda kk, pos: kk > thr)
    n_ge = count(lambda kk, pos: kk >= thr)
    tie = jnp.where((n_ge > k_sel) & (thr > INT_MIN), 1.0, 0.0)

    @pl.when(jnp.max(tie) > 0.0)
    def _():
        need = k_sel - n_gt
        def jbit(it, jcut):
            cand = jcut | jnp.left_shift(jnp.int32(1), idx_bits - 1 - it)
            f = count(lambda kk, pos: (kk == thr) & (pos < cand))
            return jnp.where(f <= need, cand, jcut)
        jcut = lax.fori_loop(0, idx_bits, jbit, jnp.zeros((QB, LANES), jnp.int32))
        def drop(c, carry):
            for j in range(KC // LANES):
                sl = slice(j * LANES, (j + 1) * LANES)
                kk = keys_ref[c, :, sl]
                pos = c * KC + j * LANES + lane_l
                keys_ref[c, :, sl] = jnp.where((kk == thr) & (pos >= jcut), INT_MIN, kk)
            return carry
        lax.fori_loop(0, n_ch, drop, 0)

    thr_eff = _tile_lanes(jnp.maximum(thr, INT_MIN + 1), KC // LANES)
    _flash_init(m_ref, l_ref, acc_ref)
    q_all = q_ref[...].reshape(H * QB, 256)

    def chunk(c, carry):
        k0 = pl.multiple_of(c * KC, KC)
        kc = kcat_ref[0, pl.ds(k0, KC), :]
        bias = jnp.where(keys_ref[c] >= thr_eff, 0.0, -jnp.inf)
        s = _dot_nt(q_all, kc) + _tile_rows(bias, H)
        _flash_update(s, kc[:, :DSA_KV_LORA], m_ref, l_ref, acc_ref, 0)
        return carry
    _paired_loop(n_ch, chunk)
    o = acc_ref[...] / jnp.maximum(l_ref[...], 1e-30)
    for h in range(H):
        o_ref[:, h * 128:(h + 1) * 128] = o[h * QB:(h + 1) * QB].astype(BF16)


def _dsa_attn(qcat, iq, iw, kcat, ik, B, S):
    T = B * S
    nq = S // Q_BLOCK
    row = lambda b, q: (b * nq + q, 0)
    bat = lambda b, q: (b, 0, 0)
    k_sel = min(DSA_TOPK, S // 4)
    assert S // LANES < 64, "per-lane key counts are packed 6 bits each"
    kern = functools.partial(_dsa_attn_kernel, k_sel=k_sel, idx_bits=int(S).bit_length())
    return pl.pallas_call(
        kern, grid=(B, nq),
        in_specs=[pl.BlockSpec((DSA_HEADS, Q_BLOCK, 256), lambda b, q: (0, b * nq + q, 0)),
                  pl.BlockSpec((IDX_HEADS, Q_BLOCK, LANES), lambda b, q: (0, b * nq + q, 0)),
                  pl.BlockSpec((Q_BLOCK, LANES), row),
                  pl.BlockSpec((1, S, 256), bat),
                  pl.BlockSpec((1, S, LANES), bat)],
        out_specs=pl.BlockSpec((Q_BLOCK, DSA_HEADS * DSA_KV_LORA), row),
        out_shape=jax.ShapeDtypeStruct((T, DSA_HEADS * DSA_KV_LORA), BF16),
        scratch_shapes=[pltpu.VMEM((S // DSA_KC, Q_BLOCK, DSA_KC), jnp.int32),
                        pltpu.VMEM((IDX_HEADS * Q_BLOCK, DSA_KC), F32),
                        pltpu.VMEM((DSA_HEADS * Q_BLOCK, LANES), F32),
                        pltpu.VMEM((DSA_HEADS * Q_BLOCK, LANES), F32),
                        pltpu.VMEM((DSA_HEADS * Q_BLOCK, DSA_KV_LORA), F32),
                        pltpu.VMEM((Q_BLOCK, LANES), jnp.int32),
                        pltpu.VMEM((Q_BLOCK, LANES), F32)],
        name="dsa_attn")(qcat, iq, iw, kcat.reshape(B, S, 256), ik.reshape(B, S, LANES))


def _wuvo_kernel(uv_ref, wo_ref, o_ref):
    o_ref[0] = jnp.dot(uv_ref[0], wo_ref[...], precision=HIGHEST,
                       preferred_element_type=F32).astype(BF16)


def _wuvo(w_uv, w_o):
    H, C, V = w_uv.shape
    D = w_o.shape[1]
    out = pl.pallas_call(
        _wuvo_kernel, grid=(H,),
        in_specs=[pl.BlockSpec((1, C, V), lambda h: (h, 0, 0)),
                  pl.BlockSpec((V, D), lambda h: (h, 0))],
        out_specs=pl.BlockSpec((1, C, D), lambda h: (h, 0, 0)),
        out_shape=jax.ShapeDtypeStruct((H, C, D), BF16), name="wuvo")(w_uv, w_o)
    return out.reshape(H * C, D)


def _post_kernel(a_ref, w_ref, x_ref, gate_ref, gn_ref, sc_ref, sh_ref, xo_ref, ho_ref):
    x1 = x_ref[...] + gate_ref[0] * _dot(a_ref[...], w_ref[...])
    xo_ref[...] = x1
    ho_ref[...] = _modulate(x1, gn_ref[...], sc_ref[0], sh_ref[0]).astype(ho_ref.dtype)


def _post(a, w, xf, gate, gn, scale, shift, S, h_dtype):
    T, D = xf.shape
    tm = TOKEN_TILE
    nps = S // tm
    row = lambda i: (i, 0)
    bat = lambda i: (i // nps, 0, 0)
    cst2 = lambda i: (0, 0)
    return pl.pallas_call(
        _post_kernel, grid=(T // tm,),
        in_specs=[pl.BlockSpec((tm, a.shape[1]), row), pl.BlockSpec(w.shape, cst2),
                  pl.BlockSpec((tm, D), row), pl.BlockSpec((1, 1, D), bat),
                  pl.BlockSpec((1, D), cst2), pl.BlockSpec((1, 1, D), bat), pl.BlockSpec((1, 1, D), bat)],
        out_specs=[pl.BlockSpec((tm, D), row), pl.BlockSpec((tm, D), row)],
        out_shape=[jax.ShapeDtypeStruct((T, D), F32), jax.ShapeDtypeStruct((T, D), h_dtype)],
        name="post")(a, w, xf, gate, gn, scale, shift)


def _ffn_kernel(h_ref, x_ref, gate_ref, w1_ref, w3_ref, w2_ref, xo_ref, acc_ref):
    j = pl.program_id(1)

    @pl.when(j == 0)
    def _():
        acc_ref[...] = jnp.zeros_like(acc_ref)
    hb = h_ref[...]
    a = _dot(hb, w1_ref[...])
    b = _dot(hb, w3_ref[...])
    acc_ref[...] += _dot((a * _sigmoid(a) * b).astype(BF16), w2_ref[...])

    @pl.when(j == pl.num_programs(1) - 1)
    def _():
        xo_ref[...] = x_ref[...] + gate_ref[0] * acc_ref[...]


def _ffn(h, xf, gate, w1, w3, w2, S, tn):
    T, D = xf.shape
    F = w1.shape[1]
    tm = TOKEN_TILE
    nps = S // tm
    return pl.pallas_call(
        _ffn_kernel, grid=(T // tm, F // tn),
        in_specs=[pl.BlockSpec((tm, D), lambda i, j: (i, 0)),
                  pl.BlockSpec((tm, D), lambda i, j: (i, 0)),
                  pl.BlockSpec((1, 1, D), lambda i, j: (i // nps, 0, 0)),
                  pl.BlockSpec((D, tn), lambda i, j: (0, j)),
                  pl.BlockSpec((D, tn), lambda i, j: (0, j)),
                  pl.BlockSpec((tn, D), lambda i, j: (j, 0))],
        out_specs=pl.BlockSpec((tm, D), lambda i, j: (i, 0)),
        out_shape=jax.ShapeDtypeStruct((T, D), F32),
        scratch_shapes=[pltpu.VMEM((tm, D), F32)],
        name="ffn")(h, xf, gate, w1, w3, w2)


def _nsa_proj_kernel(x_ref, sh_ref, sc_ref, gn_ref, wq_ref, wkv_ref, wg_ref, rc_ref, ra_ref, rb_ref,
                     q_ref, kc_ref, vc_ref, ks_ref, vs_ref, kw_ref, vw_ref, g_ref):
    hb = _modulate(x_ref[...], gn_ref[...], sc_ref[0], sh_ref[0]).astype(BF16)
    rc, ra, rb = rc_ref[...], ra_ref[...], rb_ref[...]

    def roped(v):
        return jnp.concatenate([_rope(v[:, :128], rc, ra, rb), _rope(v[:, 128:], rc, ra, rb)], axis=1)

    for p in range(NSA_HEADS // 2):
        qv = (roped(_dot(hb, wq_ref[:, p * 256:(p + 1) * 256])) * ATT_SCALE).astype(BF16)
        q_ref[2 * p] = qv[:, :128]
        q_ref[2 * p + 1] = qv[:, 128:]
    outs = (kc_ref, vc_ref, ks_ref, vs_ref, kw_ref, vw_ref)
    for n, o_ref in enumerate(outs):
        v = _dot(hb, wkv_ref[:, n * 256:(n + 1) * 256])
        o_ref[...] = (roped(v) if n % 2 == 0 else v).astype(BF16)
    g_ref[...] = _sigmoid(_dot(hb, wg_ref[...]))


def _nsa_proj(xf, shift, scale, gn, wq, wkv, wg, rope, S):
    T, D = xf.shape
    tm = TOKEN_TILE
    nps = S // tm
    row = lambda i: (i, 0)
    bat = lambda i: (i // nps, 0, 0)
    pos = lambda i: (i % nps, 0)
    cst2 = lambda i: (0, 0)
    kv_spec = pl.BlockSpec((tm, 256), row)
    kv_shape = jax.ShapeDtypeStruct((T, 256), BF16)
    return pl.pallas_call(
        _nsa_proj_kernel, grid=(T // tm,),
        in_specs=[pl.BlockSpec((tm, D), row),
                  pl.BlockSpec((1, 1, D), bat), pl.BlockSpec((1, 1, D), bat),
                  pl.BlockSpec((1, D), cst2),
                  pl.BlockSpec(wq.shape, cst2), pl.BlockSpec(wkv.shape, cst2), pl.BlockSpec(wg.shape, cst2),
                  pl.BlockSpec((tm, LANES), pos), pl.BlockSpec((tm, LANES), pos),
                  pl.BlockSpec((tm, LANES), pos)],
        out_specs=[pl.BlockSpec((NSA_HEADS, tm, LANES), lambda i: (0, i, 0))] + [kv_spec] * 6
                  + [pl.BlockSpec((tm, LANES), row)],
        out_shape=[jax.ShapeDtypeStruct((NSA_HEADS, T, LANES), BF16)] + [kv_shape] * 6
                  + [jax.ShapeDtypeStruct((T, LANES), F32)],
        name="nsa_proj")(xf, shift, scale, gn, wq, wkv, wg, *rope)


def _compress_kernel(uk_ref, uv_ref, pe_ref, k1_ref, k2_ref, v1_ref, v2_ref, ko_ref, vo_ref):
    half = CMP_STRIDE * HEAD_DIM
    pe = jnp.broadcast_to(pe_ref[...], (8, 2 * half)).astype(BF16)

    def comp(u_ref, w1_ref, w2_ref):
        bias = _dot(pe, w1_ref[...])[0:1]
        out = None
        for gg in range(2):
            u = u_ref[0, gg]
            a = _dot(u, w1_ref[:half, :])
            b = _dot(u, w1_ref[half:, :])
            nrow = b.shape[0]
            hid = a + pltpu.roll(b, nrow - 1, 0) + bias
            hid = (hid * _sigmoid(hid)).astype(BF16)
            o = _dot(hid, w2_ref[gg])
            out = o if out is None else out + o
        return out
    ko_ref[0] = comp(uk_ref, k1_ref, k2_ref).astype(BF16)
    vo_ref[0] = comp(uv_ref, v1_ref, v2_ref).astype(BF16)


def _compress(uk, uv, pe, k1, k2, v1, v2):
    B, G, nch, W = uk.shape
    u_spec = pl.BlockSpec((1, 2, nch, W), lambda b, p: (b, p, 0, 0))
    w1_spec = pl.BlockSpec(k1.shape, lambda b, p: (0, 0))
    w2_spec = pl.BlockSpec(k2.shape, lambda b, p: (0, 0, 0))
    o_spec = pl.BlockSpec((1, nch, LANES), lambda b, p: (b, 0, p))
    o_shape = jax.ShapeDtypeStruct((B, nch, G * HEAD_DIM), BF16)
    return pl.pallas_call(
        _compress_kernel, grid=(B, G // 2),
        in_specs=[u_spec, u_spec, pl.BlockSpec(pe.shape, lambda b, p: (0, 0)),
                  w1_spec, w2_spec, w1_spec, w2_spec],
        out_specs=[o_spec, o_spec], out_shape=[o_shape, o_shape],
        name="compress")(uk, uv, pe, k1, k2, v1, v2)


def _nsa_attn_kernel(q_ref, g_ref, kcmp_ref, vcmp_ref, ks_ref, vs_ref, kw_ref, vw_ref,
                     covt_ref, exp_ref, o_ref, oc_ref, psum_ref, imp_ref, sel_ref, m_ref, l_ref, acc_ref,
                     osel_ref, *, n_sel):
    QB, KC, HPG = Q_BLOCK, NSA_KC, NSA_HPG
    GR = HPG * QB
    qi = pl.program_id(1)
    qs = qi * QB
    row_t = qs + lax.broadcasted_iota(jnp.int32, (QB, 1), 0)
    nch = kcmp_ref.shape[1]
    nb = covt_ref.shape[0]
    n_ch = qs // KC + 1
    gates = g_ref[...]
    pair = lambda g: slice((g // 2) * 128, (g // 2) * 128 + 128)
    rows = lambda g: slice(g * GR, (g + 1) * GR)
    q_grp = lambda g: q_ref[g * HPG:(g + 1) * HPG].reshape(GR, LANES)

    cmp_end = lax.broadcasted_iota(jnp.int32, (QB, nch), 1) * CMP_STRIDE + (CMP_LEN - 1)
    cbias = _tile_rows(jnp.where(cmp_end <= row_t, 0.0, -jnp.inf), HPG)
    for g in range(NSA_GROUPS):
        s = _dot_nt(q_grp(g), kcmp_ref[0, :, pair(g)]) + cbias
        m = jnp.max(s, axis=1, keepdims=True)
        m = jnp.where(m == -jnp.inf, 0.0, m)
        e = jnp.exp2(s - m)
        p = e / jnp.maximum(jnp.sum(e, axis=1, keepdims=True), 1e-30)
        oc_ref[rows(g), :] = _dot(p.astype(BF16), vcmp_ref[0, :, pair(g)])
        psum_ref[g] = p[0:QB] + p[QB:2 * QB] + p[2 * QB:3 * QB] + p[3 * QB:4 * QB]

    jb = lax.broadcasted_iota(jnp.int32, (nb, QB), 0)
    cur = (qs + lax.broadcasted_iota(jnp.int32, (nb, QB), 1)) // SEL_LEN
    forced = (jb == 0) | (jb == cur) | (jb == cur - 1)
    ri = lax.broadcasted_iota(jnp.int32, (QB, QB), 0)
    ci = lax.broadcasted_iota(jnp.int32, (QB, QB), 1)
    eye = jnp.where(ri == ci, 1.0, 0.0).astype(BF16)
    imps = []
    for g in range(NSA_GROUPS):
        imp = lax.dot_general(covt_ref[...], psum_ref[g], (((1,), (1,)), ((), ())),
                              precision=HIGHEST, preferred_element_type=F32)
        imp = jnp.where(forced, FORCE_SCORE, imp)
        imp = jnp.where(jb <= cur, imp, -jnp.inf)
        imp_ref[g] = imp
        imps.append(imp)

    def rank_body(i, ranks):
        before = jnp.where(i < jb, 1.0, 0.0)
        out = []
        for g in range(NSA_GROUPS):
            ri_ = imp_ref[g, pl.ds(i, 1), :]
            out.append(ranks[g] + jnp.where(ri_ > imps[g], 1.0, 0.0) + jnp.where(ri_ == imps[g], before, 0.0))
        return tuple(out)
    n_live = jnp.minimum((qs + QB - 1) // SEL_LEN + 1, nb)
    ranks = lax.fori_loop(0, n_live, rank_body, (jnp.zeros((nb, QB), F32),) * NSA_GROUPS)
    for g in range(NSA_GROUPS):
        selt = jnp.where(ranks[g] < n_sel, 1.0, 0.0).astype(BF16)
        sel_ref[g] = _dot_nt(eye, selt).astype(BF16)

    _flash_init(m_ref, l_ref, acc_ref)

    def sel_chunk(c, carry):
        k0 = pl.multiple_of(c * KC, KC)
        causal = (k0 + lax.broadcasted_iota(jnp.int32, (QB, KC), 1)) <= row_t
        for g in range(NSA_GROUPS):
            keep = (_dot(sel_ref[g], exp_ref[c]) > 0.5) & causal
            bias = _tile_rows(jnp.where(keep, 0.0, -jnp.inf), HPG)
            s = _dot_nt(q_grp(g), ks_ref[0, pl.ds(k0, KC), pair(g)]) + bias
            _flash_update(s, vs_ref[0, pl.ds(k0, KC), pair(g)], m_ref, l_ref, acc_ref, g * GR)
        return carry
    _paired_loop(n_ch, sel_chunk)
    osel_ref[...] = acc_ref[...] / jnp.maximum(l_ref[...], 1e-30)

    wl = WINDOW + QB
    w0 = pl.multiple_of(jnp.maximum(qs - WINDOW, 0), QB)
    wpos = w0 + lax.broadcasted_iota(jnp.int32, (QB, wl), 1)
    wbias = _tile_rows(jnp.where((wpos <= row_t) & (wpos > row_t - WINDOW), 0.0, -jnp.inf), HPG)
    for g in range(NSA_GROUPS):
        s = _dot_nt(q_grp(g), kw_ref[0, pl.ds(w0, wl), pair(g)]) + wbias
        m = jnp.max(s, axis=1, keepdims=True)
        m = jnp.where(m == -jnp.inf, 0.0, m)
        e = jnp.exp2(s - m)
        den = jnp.maximum(jnp.sum(e, axis=1, keepdims=True), 1e-30)
        acc_ref[rows(g), :] = _dot(e.astype(BF16), vw_ref[0, pl.ds(w0, wl), pair(g)]) / den

    lane = lax.broadcasted_iota(jnp.int32, (QB, LANES), 1)
    pair_out = [None, None]
    for h in range(NSA_HEADS):
        g = h // HPG
        hr = slice(h * QB, (h + 1) * QB)
        o = (gates[:, 3 * h:3 * h + 1] * oc_ref[hr, :] + gates[:, 3 * h + 1:3 * h + 2] * osel_ref[hr, :]
             + gates[:, 3 * h + 2:3 * h + 3] * acc_ref[hr, :])
        pair_out[h % 2] = o
        if h % 2 == 1:
            if g % 2 == 0:
                both = jnp.where(lane < 64, pair_out[0], pltpu.roll(pair_out[1], 64, 1))
            else:
                both = jnp.where(lane < 64, pltpu.roll(pair_out[0], 64, 1), pair_out[1])
            o_ref[:, (h // 2) * 128:(h // 2) * 128 + 128] = both.astype(BF16)


def _nsa_attn(q, gates, kcmp, vcmp, ks, vs, kw, vw, covt, expand, B, S):
    T = B * S
    nq = S // Q_BLOCK
    nch = kcmp.shape[1]
    nb = S // SEL_LEN
    row = lambda b, i: (b * nq + i, 0)
    bat = lambda b, i: (b, 0, 0)
    kv = lambda a: a.reshape(B, S, 256)
    kv_spec = pl.BlockSpec((1, S, 256), bat)
    cmp_spec = pl.BlockSpec((1, nch, 256), bat)
    kern = functools.partial(_nsa_attn_kernel, n_sel=min(SEL_BLOCKS, nb))
    return pl.pallas_call(
        kern, grid=(B, nq),
        in_specs=[pl.BlockSpec((NSA_HEADS, Q_BLOCK, LANES), lambda b, i: (0, b * nq + i, 0)),
                  pl.BlockSpec((Q_BLOCK, LANES), row),
                  cmp_spec, cmp_spec, kv_spec, kv_spec, kv_spec, kv_spec,
                  pl.BlockSpec(covt.shape, lambda b, i: (0, 0)),
                  pl.BlockSpec(expand.shape, lambda b, i: (0, 0, 0))],
        out_specs=pl.BlockSpec((Q_BLOCK, NSA_HEADS * HEAD_DIM), row),
        out_shape=jax.ShapeDtypeStruct((T, NSA_HEADS * HEAD_DIM), BF16),
        scratch_shapes=[pltpu.VMEM((NSA_HEADS * Q_BLOCK, LANES), F32),
                        pltpu.VMEM((NSA_GROUPS, Q_BLOCK, nch), F32),
                        pltpu.VMEM((NSA_GROUPS, nb, Q_BLOCK), F32),
                        pltpu.VMEM((NSA_GROUPS, Q_BLOCK, nb), BF16),
                        pltpu.VMEM((NSA_HEADS * Q_BLOCK, LANES), F32),
                        pltpu.VMEM((NSA_HEADS * Q_BLOCK, LANES), F32),
                        pltpu.VMEM((NSA_HEADS * Q_BLOCK, LANES), F32),
                        pltpu.VMEM((NSA_HEADS * Q_BLOCK, LANES), F32)],
        name="nsa_attn")(q, gates, kcmp, vcmp, kv(ks), kv(vs), kv(kw), kv(vw), covt, expand)


def _router_kernel(h_ref, wr_ref, route_ref, cnt_ref, carry_ref):
    i = pl.program_id(0)
    tm = h_ref.shape[0]

    @pl.when(i == 0)
    def _():
        carry_ref[...] = jnp.zeros_like(carry_ref)
    lane = lax.broadcasted_iota(jnp.int32, (tm, LANES), 1).astype(F32)
    lg = jnp.dot(h_ref[...], wr_ref[...], precision=HIGHEST, preferred_element_type=F32)
    lg = jnp.where(lane < N_EXPERTS, lg, -jnp.inf)
    v1 = jnp.max(lg, axis=1, keepdims=True)
    i1 = jnp.min(jnp.where(lg == v1, lane, float(LANES)), axis=1, keepdims=True)
    lg2 = jnp.where(lane == i1, -jnp.inf, lg)
    v2 = jnp.max(lg2, axis=1, keepdims=True)
    i2 = jnp.min(jnp.where(lg2 == v2, lane, float(LANES)), axis=1, keepdims=True)
    e2 = jnp.exp(v2 - v1)
    g1 = 1.0 / (1.0 + e2)
    g2 = e2 / (1.0 + e2)
    oh1 = jnp.where(lane == i1, 1.0, 0.0)
    oh2 = jnp.where(lane == i2, 1.0, 0.0)
    both = oh1 + oh2
    ri = lax.broadcasted_iota(jnp.int32, (tm, tm), 0)
    ci = lax.broadcasted_iota(jnp.int32, (tm, tm), 1)
    lower = jnp.where(ri > ci, 1.0, 0.0).astype(BF16)
    tot = carry_ref[0:1, :] + _dot(lower, both.astype(BF16))
    r1 = jnp.sum(oh1 * tot, axis=1, keepdims=True)
    r2 = jnp.sum(oh2 * tot, axis=1, keepdims=True)
    new_carry = carry_ref[...] + jnp.sum(both, axis=0, keepdims=True)
    carry_ref[...] = new_carry
    cnt_ref[...] = new_carry
    out = jnp.zeros((tm, LANES), F32)
    for col, val in enumerate((i1, i2, g1, g2, r1, r2)):
        out = jnp.where(lane == col, val, out)
    route_ref[...] = out


def _router(h, wr):
    T, D = h.shape
    tm = TOKEN_TILE
    return pl.pallas_call(
        _router_kernel, grid=(T // tm,),
        in_specs=[pl.BlockSpec((tm, D), lambda i: (i, 0)), pl.BlockSpec(wr.shape, lambda i: (0, 0))],
        out_specs=[pl.BlockSpec((tm, LANES), lambda i: (i, 0)), pl.BlockSpec((8, LANES), lambda i: (0, 0))],
        out_shape=[jax.ShapeDtypeStruct((T, LANES), F32), jax.ShapeDtypeStruct((8, LANES), F32)],
        scratch_shapes=[pltpu.VMEM((8, LANES), F32)],
        name="router")(h, wr)


def _row_copy(src, dst, sem):
    return pltpu.make_async_copy(src, dst, sem)


def _moe_ffn_kernel(be_ref, nu_ref, src_ref, dst_ref, h_ref, w1_ref, w3_ref, w2_ref, y_ref,
                    xin_ref, xb_ref, acc_ref, yout_ref, sem_in, sem_out, *, nj):
    i = pl.program_id(0)
    j = pl.program_id(1)
    nblk = src_ref.shape[0]
    issue_steps = 4
    per_step = MOE_BLOCK // issue_steps
    assert nj > issue_steps and per_step * issue_steps == MOE_BLOCK
    used = i < nu_ref[0]
    slot = i % 2
    nxt = jnp.minimum(i + 1, nblk - 1)
    prv = jnp.maximum(i - 1, 0)

    def gather(blk, r, buf):
        return _row_copy(h_ref.at[pl.ds(src_ref[blk, r], 1), :], xin_ref.at[buf, pl.ds(r, 1), :],
                         sem_in.at[buf])

    def scatter(blk, r):
        return _row_copy(yout_ref.at[pl.ds(r, 1), :], y_ref.at[pl.ds(dst_ref[blk, r], 1), :], sem_out)

    def issue(r0, n):
        for u in range(n):
            gather(nxt, r0 + u, 1 - slot).start()
            scatter(prv, r0 + u).start()

    def wait_gathered(buf):
        _row_copy(h_ref.at[pl.ds(0, MOE_BLOCK), :], xin_ref.at[buf], sem_in.at[buf]).wait()

    def wait_scattered():
        _row_copy(yout_ref, y_ref.at[pl.ds(0, MOE_BLOCK), :], sem_out).wait()

    @pl.when((i == 0) & (j == 0))
    def _():
        yout_ref[...] = jnp.zeros_like(yout_ref)

        def first(r, carry):
            gather(0, r, 0).start()
            return carry
        lax.fori_loop(0, MOE_BLOCK, first, 0)

    @pl.when(j == 0)
    def _():
        wait_gathered(slot)
        xb_ref[...] = xin_ref[slot].astype(BF16)
        acc_ref[...] = jnp.zeros_like(acc_ref)

    def expert_step():
        xb = xb_ref[...]
        a = _dot(xb, w1_ref[0])
        b = _dot(xb, w3_ref[0])
        acc_ref[...] += _dot((a * _sigmoid(a) * b).astype(BF16), w2_ref[0])

    issuing = j < issue_steps

    @pl.when(used & issuing)
    def _():
        issue(j * per_step, per_step)
        expert_step()

    @pl.when(used & jnp.logical_not(issuing))
    def _():
        expert_step()

    @pl.when(jnp.logical_not(used) & issuing)
    def _():
        issue(j * per_step, per_step)

    @pl.when(j == nj - 1)
    def _():
        wait_scattered()
        yout_ref[...] = acc_ref[...]

    @pl.when((i == pl.num_programs(0) - 1) & (j == nj - 1))
    def _():
        wait_gathered(1 - slot)


def _moe_ffn(block_e, n_used, slot_src, slot_dst, h, w1, w3, w2, tn):
    T, D = h.shape
    nblk = slot_src.shape[0]
    E, _, F = w1.shape
    nj = F // tn
    bi = lambda i: jnp.minimum(i, nblk - 1)
    jj = lambda i, j, nu: jnp.where(i < nu[0], j, nj - 1)
    return pl.pallas_call(
        functools.partial(_moe_ffn_kernel, nj=nj),
        grid_spec=pltpu.PrefetchScalarGridSpec(
            num_scalar_prefetch=4, grid=(nblk + 1, nj),
            in_specs=[pl.BlockSpec(memory_space=pl.ANY),
                      pl.BlockSpec((1, D, tn), lambda i, j, be, nu, s, d: (be[bi(i)], 0, jj(i, j, nu))),
                      pl.BlockSpec((1, D, tn), lambda i, j, be, nu, s, d: (be[bi(i)], 0, jj(i, j, nu))),
                      pl.BlockSpec((1, tn, D), lambda i, j, be, nu, s, d: (be[bi(i)], jj(i, j, nu), 0))],
            out_specs=pl.BlockSpec(memory_space=pl.ANY),
            scratch_shapes=[pltpu.VMEM((2, MOE_BLOCK, D), F32), pltpu.VMEM((MOE_BLOCK, D), BF16),
                            pltpu.VMEM((MOE_BLOCK, D), F32), pltpu.VMEM((MOE_BLOCK, D), F32),
                            pltpu.SemaphoreType.DMA((2,)), pltpu.SemaphoreType.DMA(())]),
        out_shape=jax.ShapeDtypeStruct((2 * T + MOE_BLOCK, D), F32),
        name="moe_ffn")(block_e, n_used, slot_src, slot_dst, h, w1, w3, w2)


def _moe_combine_kernel(y0_ref, y1_ref, x_ref, gate_ref, route_ref, fn_ref, o_ref):
    route = route_ref[...]
    y = route[:, 2:3] * y0_ref[...] + route[:, 3:4] * y1_ref[...]
    o_ref[...] = _rms(x_ref[...] + gate_ref[0] * y, fn_ref[...])


def _moe_combine(y, xf, gate, route, fn, S):
    T, D = xf.shape
    tm = TOKEN_TILE
    nps = S // tm
    nt = T // tm
    return pl.pallas_call(
        _moe_combine_kernel, grid=(nt,),
        in_specs=[pl.BlockSpec((tm, D), lambda i: (i, 0)),
                  pl.BlockSpec((tm, D), lambda i: (nt + i, 0)),
                  pl.BlockSpec((tm, D), lambda i: (i, 0)),
                  pl.BlockSpec((1, 1, D), lambda i: (i // nps, 0, 0)),
                  pl.BlockSpec((tm, LANES), lambda i: (i, 0)),
                  pl.BlockSpec((1, D), lambda i: (0, 0))],
        out_specs=pl.BlockSpec((tm, D), lambda i: (i, 0)),
        out_shape=jax.ShapeDtypeStruct((T, D), F32), name="moe_combine")(y, y, xf, gate, route, fn)


def _rope_tables(S):
    inv = ROPE_THETA ** (-jnp.arange(0, ROPE_DIM, 2, dtype=F32) / ROPE_DIM)
    ang = jnp.arange(S, dtype=F32)[:, None] * inv[None, :]
    cos, sin = jnp.cos(ang), jnp.sin(ang)
    pm = np.arange(LANES) % HEAD_DIM
    col = pm % (ROPE_DIM // 2)
    rc = jnp.where((pm < ROPE_DIM)[None, :], cos[:, col], 1.0)
    ra = jnp.where((pm < ROPE_DIM // 2)[None, :], -sin[:, col], 0.0)
    rb = jnp.where(((pm >= ROPE_DIM // 2) & (pm < ROPE_DIM))[None, :], sin[:, col], 0.0)
    return rc, ra, rb


def _dsa_weights(w_in, w_uk, w_iq):
    D = w_in.shape[0]
    a, b, c, d = DSA_Q_LORA, DSA_Q_LORA + DSA_KV_LORA, DSA_Q_LORA + DSA_KV_LORA + ROPE_DIM, \
        DSA_Q_LORA + DSA_KV_LORA + ROPE_DIM + IDX_DIM
    win = jnp.concatenate([w_in[:, :b], w_in[:, c:d], w_in[:, b:c], w_in[:, d:],
                           jnp.zeros((D, 512 - w_in.shape[1]), F32)], axis=1).astype(BF16)
    H = DSA_HEADS
    blk = jnp.zeros((H, HEAD_DIM, 256), F32)
    blk = blk.at[:, ROPE_DIM:, :DSA_KV_LORA].set(jnp.transpose(w_uk, (0, 2, 1)))
    blk = blk.at[:, :ROPE_DIM, 192:192 + ROPE_DIM].set(jnp.eye(ROPE_DIM, dtype=F32))
    z = jnp.zeros((H // 2, HEAD_DIM, 256), F32)
    wcat = jnp.concatenate([jnp.concatenate([blk[0::2], z], axis=2),
                            jnp.concatenate([z, blk[1::2]], axis=2)], axis=1).astype(BF16)
    wiq = w_iq.reshape(DSA_Q_LORA, IDX_HEADS, IDX_DIM)
    wiq = jnp.concatenate([wiq, jnp.zeros_like(wiq)], axis=2).reshape(DSA_Q_LORA, IDX_HEADS * 128)
    return win, wcat, wiq.astype(BF16)


def _nsa_weights(w_in):
    D = w_in.shape[0]
    nq = NSA_HEADS * HEAD_DIM
    wq = w_in[:, :nq].reshape(D, NSA_HEADS, HEAD_DIM)
    z = jnp.zeros_like(wq)
    odd = ((np.arange(NSA_HEADS) // NSA_HPG) % 2 == 1)[None, :, None]
    wq = jnp.concatenate([jnp.where(odd, z, wq), jnp.where(odd, wq, z)], axis=2).reshape(D, NSA_HEADS * 128)
    wkv = w_in[:, nq:nq + 6 * 256]
    wg = jnp.concatenate([w_in[:, nq + 6 * 256:], jnp.zeros((D, LANES - 3 * NSA_HEADS), F32)], axis=1)
    return wq.astype(BF16), wkv.astype(BF16), wg.astype(BF16)


def _nsa_tables(S):
    nch = S // CMP_STRIDE
    nc = (S - CMP_LEN) // CMP_STRIDE + 1
    nb = S // SEL_LEN
    cstart = np.arange(nch) * CMP_STRIDE
    bstart = np.arange(nb) * SEL_LEN
    cov = ((cstart[None, :] < bstart[:, None] + SEL_LEN) & (cstart[None, :] + CMP_LEN > bstart[:, None])
           & (np.arange(nch)[None, :] < nc)).astype(np.float32)
    kpos = np.arange(S).reshape(S // NSA_KC, 1, NSA_KC)
    expand = (kpos // SEL_LEN == np.arange(nb)[None, :, None]).astype(np.float32)
    return jnp.asarray(cov), jnp.asarray(expand, dtype=BF16)


def _slot_tables(dest, T, n_blocks):
    ns = n_blocks * MOE_BLOCK
    asg = jnp.full((ns,), -1, jnp.int32).at[dest].set(jnp.arange(2 * T, dtype=jnp.int32))
    real = asg >= 0
    src = jnp.where(real, asg // 2, 0)
    spare = 2 * T + jnp.arange(ns, dtype=jnp.int32) % MOE_BLOCK
    dst = jnp.where(real, (asg % 2) * T + asg // 2, spare)
    return src.reshape(n_blocks, MOE_BLOCK), dst.reshape(n_blocks, MOE_BLOCK)


def _chunk_tokens(a, B, S):
    a = a.reshape(B, S // CMP_STRIDE, CMP_STRIDE, NSA_GROUPS, HEAD_DIM)
    return jnp.transpose(a, (0, 3, 1, 2, 4)).reshape(B, NSA_GROUPS, S // CMP_STRIDE, CMP_STRIDE * HEAD_DIM)


def kernel(x, c, norm_mix, norm_ffn, ada_w, ada_b, final_norm, dsa_w_in, dsa_g_q, dsa_w_uq, dsa_g_kv,
           dsa_w_uk, dsa_w_uv, dsa_w_iq, dsa_w_o, ffn_w1, ffn_w3, ffn_w2, nsa_w_in, nsa_cmp_pe,
           nsa_cmp_k1, nsa_cmp_k2, nsa_cmp_v1, nsa_cmp_v2, nsa_w_o, moe_router, moe_w1, moe_w3, moe_w2):
    B, S, D = x.shape
    T = B * S
    xf = x.reshape(T, D)
    mods = _ada(c, ada_w, ada_b).reshape(4, B, 3, 1, D)
    shift = lambda s: mods[s, :, 0]
    scale = lambda s: mods[s, :, 1]
    gate = lambda s: mods[s, :, 2]
    rope = _rope_tables(S)

    win, wcat, wiq = _dsa_weights(dsa_w_in[0], dsa_w_uk[0], dsa_w_iq[0])
    qcat, iq, kcat, ik, iw = _dsa_proj(
        xf, shift(0), scale(0), norm_mix[0:1], win, dsa_g_q[0:1], dsa_g_kv[0:1],
        dsa_w_uq[0].astype(BF16), wiq, wcat, rope, S)
    olat = _dsa_attn(qcat, iq, iw, kcat, ik, B, S)
    wuvo = _wuvo(dsa_w_uv[0], dsa_w_o[0])
    x1, h1 = _post(olat, wuvo, xf, gate(0), norm_ffn[0:1], scale(1), shift(1), S, BF16)
    x2 = _ffn(h1, x1, gate(1), ffn_w1[0].astype(BF16), ffn_w3[0].astype(BF16), ffn_w2[0].astype(BF16),
              S, ffn_w1.shape[2] // 2)

    wq, wkv, wg = _nsa_weights(nsa_w_in[0])
    q, kc, vc, ks, vs, kw, vw, gates = _nsa_proj(xf=x2, shift=shift(2), scale=scale(2), gn=norm_mix[1:2],
                                                 wq=wq, wkv=wkv, wg=wg, rope=rope, S=S)
    zpad = jnp.zeros((CMP_HIDDEN, HEAD_DIM), F32)
    pad2 = lambda w2: jnp.stack([jnp.concatenate([w2, zpad], axis=1),
                                 jnp.concatenate([zpad, w2], axis=1)]).astype(BF16)
    kcmp, vcmp = _compress(_chunk_tokens(kc, B, S), _chunk_tokens(vc, B, S),
                           nsa_cmp_pe[0].reshape(1, CMP_LEN * HEAD_DIM),
                           nsa_cmp_k1[0].astype(BF16), pad2(nsa_cmp_k2[0]),
                           nsa_cmp_v1[0].astype(BF16), pad2(nsa_cmp_v2[0]))
    covt, expand = _nsa_tables(S)
    o = _nsa_attn(q, gates, kcmp, vcmp, ks, vs, kw, vw, covt, expand, B, S)
    x3, h3 = _post(o, nsa_w_o[0].astype(BF16), x2, gate(2), norm_ffn[1:2], scale(3), shift(3), S, F32)

    wr = jnp.concatenate([moe_router[0], jnp.zeros((D, LANES - N_EXPERTS), F32)], axis=1)
    route, cnt = _router(h3, wr)
    counts = cnt[0, :N_EXPERTS].astype(jnp.int32)
    padded = (counts + MOE_BLOCK - 1) // MOE_BLOCK * MOE_BLOCK
    ends = jnp.cumsum(padded)
    pstart = ends - padded
    eidx = route[:, 0:2].astype(jnp.int32)
    dest = (pstart[eidx] + route[:, 4:6].astype(jnp.int32)).reshape(-1)
    n_blocks = -(-(T * 2) // MOE_BLOCK) + N_EXPERTS
    block_start = jnp.arange(n_blocks, dtype=jnp.int32) * MOE_BLOCK
    block_e = jnp.minimum(jnp.sum((ends[None, :] <= block_start[:, None]).astype(jnp.int32), axis=1),
                          N_EXPERTS - 1)
    n_used = (ends[-1:] // MOE_BLOCK).astype(jnp.int32)
    slot_src, slot_dst = _slot_tables(dest, T, n_blocks)
    y = _moe_ffn(block_e, n_used, slot_src, slot_dst, h3, moe_w1[0].astype(BF16), moe_w3[0].astype(BF16),
                 moe_w2[0].astype(BF16), 512)
    out = _moe_combine(y, x3, gate(3), route, final_norm.reshape(1, D), S)
    return out.reshape(B, S, D)
```

```python
import functools

import numpy as np
import jax
import jax.numpy as jnp
from jax import lax
from jax.experimental import pallas as pl
from jax.experimental.pallas import tpu as pltpu

F32 = jnp.float32
BF16 = jnp.bfloat16
HIGHEST = lax.Precision.HIGHEST
INT_MIN = -2147483648

HEAD_DIM = 64
ROPE_DIM = 16
ROPE_THETA = 500000.0
Q_BLOCK = 128
NORM_EPS = 1e-6

DSA_HEADS = 16
DSA_NOPE = 48
DSA_Q_LORA = 256
DSA_KV_LORA = 128
IDX_HEADS = 8
IDX_DIM = 64
DSA_TOPK = 256
DSA_KC = 512

NSA_HEADS = 16
NSA_GROUPS = 4
NSA_HPG = 4
CMP_LEN = 32
CMP_STRIDE = 16
CMP_HIDDEN = 256
SEL_LEN = 64
SEL_BLOCKS = 16
WINDOW = 512
FORCE_SCORE = 1e4
NSA_KC = 512

N_EXPERTS = 8
MOE_BLOCK = 512
TOKEN_TILE = 512
ROW_TILE = 256
LANES = 128

LOG2E = 1.4426950408889634
ATT_SCALE = HEAD_DIM ** -0.5 * LOG2E


def _dot(a, b):
    return jnp.dot(a, b, preferred_element_type=F32)


def _dot_nt(a, b):
    return lax.dot_general(a, b, (((1,), (1,)), ((), ())), preferred_element_type=F32)


def _sigmoid(v):
    return 1.0 / (1.0 + jnp.exp(-v))


def _rms(v, g):
    return v * lax.rsqrt(jnp.mean(v * v, axis=-1, keepdims=True) + NORM_EPS) * g


def _modulate(v, g, scale, shift):
    return _rms(v, g) * (1.0 + scale) + shift


def _rope(v, c, sa, sb):
    return v * c + pltpu.roll(v, LANES - 8, 1) * sa + pltpu.roll(v, 8, 1) * sb


def _tile_rows(a, n):
    return jnp.concatenate([a] * n, axis=0) if n > 1 else a


def _tile_lanes(a, n):
    return jnp.concatenate([a] * n, axis=1) if n > 1 else a


def _paired_loop(n, body):
    def two(i, carry):
        return body(2 * i + 1, body(2 * i, carry))
    lax.fori_loop(0, n // 2, two, 0)

    @pl.when(n % 2 == 1)
    def _():
        body(n - 1, 0)


def _flash_init(m_ref, l_ref, acc_ref):
    m_ref[...] = jnp.full(m_ref.shape, -1e30, F32)
    l_ref[...] = jnp.zeros(l_ref.shape, F32)
    acc_ref[...] = jnp.zeros(acc_ref.shape, F32)


def _flash_update(s, v, m_ref, l_ref, acc_ref, row0):
    n = s.shape[0]
    ps = []
    for r in range(0, n, Q_BLOCK):
        rs = slice(row0 + r, row0 + r + Q_BLOCK)
        sl = s[r:r + Q_BLOCK]
        m_old = m_ref[rs, :]
        m_new = jnp.maximum(m_old, jnp.max(sl, axis=1, keepdims=True))
        p = jnp.exp2(sl - _tile_lanes(m_new, sl.shape[1] // LANES))
        alpha = jnp.exp2(m_old - m_new)
        l_ref[rs, :] = alpha * l_ref[rs, :] + jnp.sum(p, axis=1, keepdims=True)
        acc_ref[rs, :] = alpha * acc_ref[rs, :]
        m_ref[rs, :] = m_new
        ps.append(p.astype(BF16))
    pv = _dot(jnp.concatenate(ps, axis=0), v)
    acc_ref[row0:row0 + n, :] += pv


def _ada_kernel(c_ref, w_ref, b_ref, o_ref):
    cv = c_ref[...]
    sc = cv * _sigmoid(cv)
    o_ref[0] = jnp.dot(sc, w_ref[0], precision=HIGHEST, preferred_element_type=F32) + b_ref[0]


def _ada(c, ada_w, ada_b):
    B, D = c.shape
    w = ada_w.reshape(4, D, 3 * D)
    b = ada_b.reshape(4, 1, 3 * D)
    return pl.pallas_call(
        _ada_kernel, grid=(4, 3),
        in_specs=[pl.BlockSpec((B, D), lambda l, j: (0, 0)),
                  pl.BlockSpec((1, D, D), lambda l, j: (l, 0, j)),
                  pl.BlockSpec((1, 1, D), lambda l, j: (l, 0, j))],
        out_specs=pl.BlockSpec((1, B, D), lambda l, j: (l, 0, j)),
        out_shape=jax.ShapeDtypeStruct((4, B, 3 * D), F32), name="ada")(c, w, b)


def _dsa_proj_kernel(x_ref, sh_ref, sc_ref, gn_ref, win_ref, gq_ref, gkv_ref, wuq_ref, wiq_ref,
                     wcat_ref, rc_ref, ra_ref, rb_ref, qcat_ref, iq_ref, kcat_ref, ik_ref, iw_ref):
    h = _modulate(x_ref[...], gn_ref[...], sc_ref[0], sh_ref[0])
    proj = _dot(h.astype(BF16), win_ref[...])
    q_lat = _rms(proj[:, :256], gq_ref[...]).astype(BF16)
    c_kv = _rms(proj[:, 256:384], gkv_ref[...])
    rc, ra, rb = rc_ref[...], ra_ref[...], rb_ref[...]
    rest = _rope(proj[:, 384:512], rc, ra, rb)
    lane = lax.broadcasted_iota(jnp.int32, rest.shape, 1)
    kcat_ref[:, :128] = c_kv.astype(BF16)
    kcat_ref[:, 128:] = jnp.where((lane >= 64) & (lane < 80), rest, 0.0).astype(BF16)
    ik_ref[...] = jnp.where(lane < 64, rest, 0.0).astype(BF16)
    iw_ref[...] = rest * (IDX_HEADS ** -0.5)
    q = _dot(q_lat, wuq_ref[...])
    for p in range(DSA_HEADS // 2):
        qp = (_rope(q[:, p * 128:(p + 1) * 128], rc, ra, rb) * ATT_SCALE).astype(BF16)
        res = _dot(qp, wcat_ref[p]).astype(BF16)
        qcat_ref[2 * p] = res[:, :256]
        qcat_ref[2 * p + 1] = res[:, 256:]
    iqv = _dot(q_lat, wiq_ref[...])
    for hh in range(IDX_HEADS):
        iq_ref[hh] = (_rope(iqv[:, hh * 128:(hh + 1) * 128], rc, ra, rb) * (IDX_DIM ** -0.5)).astype(BF16)


def _dsa_proj(xf, shift, scale, gn, win, gq, gkv, wuq, wiq, wcat, rope, S):
    T, D = xf.shape
    tm = TOKEN_TILE
    nps = S // tm
    row = lambda i: (i, 0)
    bat = lambda i: (i // nps, 0, 0)
    pos = lambda i: (i % nps, 0)
    cst2 = lambda i: (0, 0)
    cst3 = lambda i: (0, 0, 0)
    return pl.pallas_call(
        _dsa_proj_kernel, grid=(T // tm,),
        in_specs=[pl.BlockSpec((tm, D), row),
                  pl.BlockSpec((1, 1, D), bat), pl.BlockSpec((1, 1, D), bat),
                  pl.BlockSpec((1, D), cst2),
                  pl.BlockSpec(win.shape, cst2),
                  pl.BlockSpec((1, DSA_Q_LORA), cst2), pl.BlockSpec((1, DSA_KV_LORA), cst2),
                  pl.BlockSpec(wuq.shape, cst2), pl.BlockSpec(wiq.shape, cst2),
                  pl.BlockSpec(wcat.shape, cst3),
                  pl.BlockSpec((tm, LANES), pos), pl.BlockSpec((tm, LANES), pos),
                  pl.BlockSpec((tm, LANES), pos)],
        out_specs=[pl.BlockSpec((DSA_HEADS, tm, 256), lambda i: (0, i, 0)),
                   pl.BlockSpec((IDX_HEADS, tm, LANES), lambda i: (0, i, 0)),
                   pl.BlockSpec((tm, 256), row),
                   pl.BlockSpec((tm, LANES), row),
                   pl.BlockSpec((tm, LANES), row)],
        out_shape=[jax.ShapeDtypeStruct((DSA_HEADS, T, 256), BF16),
                   jax.ShapeDtypeStruct((IDX_HEADS, T, LANES), BF16),
                   jax.ShapeDtypeStruct((T, 256), BF16),
                   jax.ShapeDtypeStruct((T, LANES), BF16),
                   jax.ShapeDtypeStruct((T, LANES), F32)],
        name="dsa_proj")(xf, shift, scale, gn, win, gq, gkv, wuq, wiq, wcat, *rope)


def _dsa_attn_kernel(q_ref, iq_ref, iw_ref, kcat_ref, ik_ref, o_ref, keys_ref, w_ref, m_ref, l_ref, acc_ref,
                     thr_ref, cnt_ref, *, k_sel, idx_bits):
    QB, KC, H = Q_BLOCK, DSA_KC, DSA_HEADS
    qi = pl.program_id(1)
    n_ch = (qi * QB) // KC + 1
    row_tl = qi * QB + lax.broadcasted_iota(jnp.int32, (QB, LANES), 0)
    iw = iw_ref[...]
    for hh in range(IDX_HEADS):
        w_ref[hh * QB:(hh + 1) * QB, :] = jnp.broadcast_to(iw[:, 80 + hh:81 + hh], (QB, KC))
    iq_all = iq_ref[...].reshape(IDX_HEADS * QB, LANES)

    def score_chunk(c, carry):
        k0 = pl.multiple_of(c * KC, KC)
        ikc = ik_ref[0, pl.ds(k0, KC), :]
        r = jnp.maximum(_dot_nt(iq_all, ikc), 0.0) * w_ref[...]
        sc = r[0:QB]
        for hh in range(1, IDX_HEADS):
            sc = sc + r[hh * QB:(hh + 1) * QB]
        sc = sc + 0.0
        bits = pltpu.bitcast(sc, jnp.int32)
        key = jnp.where(bits < 0, bits ^ 0x7FFFFFFF, bits)
        pos = k0 + lax.broadcasted_iota(jnp.int32, (QB, KC), 1)
        keys_ref[c] = jnp.where(pos <= _tile_lanes(row_tl, KC // LANES), key, INT_MIN)
        return carry
    _paired_loop(n_ch, score_chunk)

    lane_l = lax.broadcasted_iota(jnp.int32, (QB, LANES), 1)

    def count(pred):
        def body(c, a):
            for j in range(KC // LANES):
                kk = keys_ref[c, :, j * LANES:(j + 1) * LANES]
                a = a + jnp.where(pred(kk, c * KC + j * LANES + lane_l), 1.0, 0.0)
            return a
        a = lax.fori_loop(0, n_ch, body, jnp.zeros((QB, LANES), F32))
        return jnp.sum(a, axis=1, keepdims=True)

    def count3(c1, c2, c3):
        def body(c, a):
            for j in range(KC // LANES):
                kk = keys_ref[c, :, j * LANES:(j + 1) * LANES]
                a = a + jnp.where(kk >= c3, 4161, jnp.where(kk >= c2, 65, jnp.where(kk >= c1, 1, 0)))
            return a
        a = lax.fori_loop(0, n_ch, body, jnp.zeros((QB, LANES), jnp.int32))
        lane_sum = lambda x: jnp.sum(x.astype(F32), axis=1, keepdims=True)
        return lane_sum(a & 63), lane_sum((a >> 6) & 63), lane_sum(a >> 12)

    def bit_body(it, carry):
        thr, cnt = carry
        b1 = jnp.left_shift(jnp.int32(1), 31 - 2 * it)
        b0 = jnp.left_shift(jnp.int32(1), 30 - 2 * it)
        c1, c2 = thr ^ b0, thr ^ b1
        c3 = c2 ^ b0
        n1, n2, n3 = count3(c1, c2, c3)
        pick = lambda x3, x2, x1, x0: jnp.where(n3 >= k_sel, x3, jnp.where(n2 >= k_sel, x2,
                                                                           jnp.where(n1 >= k_sel, x1, x0)))
        return pick(c3, c2, c1, thr), pick(n3, n2, n1, cnt)

    def sweeps(first, n):
        thr, cnt = lax.fori_loop(first, first + n, bit_body, (thr_ref[...], cnt_ref[...]))
        thr_ref[...] = thr
        cnt_ref[...] = cnt
    thr_ref[...] = jnp.full((QB, LANES), INT_MIN, jnp.int32)
    cnt_ref[...] = jnp.full((QB, LANES), -1.0, F32)
    sweeps(0, 10)
    few = (row_tl + 1) < k_sel
    for stage in range(3):
        unsettled = jnp.where((cnt_ref[...] == k_sel) | few, 0.0, 1.0)

        @pl.when(jnp.max(unsettled) > 0.0)
        def _():
            sweeps(10 + 2 * stage, 2)
    thr = thr_ref[...]

    n_gt = count(lambda kk, pos: kk > thr)
    n_ge = count(lambda kk, pos: kk >= thr)
    tie = jnp.where((n_ge > k_sel) & (thr > INT_MIN), 1.0, 0.0)

    @pl.when(jnp.max(tie) > 0.0)
    def _():
        need = k_sel - n_gt
        def jbit(it, jcut):
            cand = jcut | jnp.left_shift(jnp.int32(1), idx_bits - 1 - it)
            f = count(lambda kk, pos: (kk == thr) & (pos < cand))
            return jnp.where(f <= need, cand, jcut)
        jcut = lax.fori_loop(0, idx_bits, jbit, jnp.zeros((QB, LANES), jnp.int32))
        def drop(c, carry):
            for j in range(KC // LANES):
                sl = slice(j * LANES, (j + 1) * LANES)
                kk = keys_ref[c, :, sl]
                pos = c * KC + j * LANES + lane_l
                keys_ref[c, :, sl] = jnp.where((kk == thr) & (pos >= jcut), INT_MIN, kk)
            return carry
        lax.fori_loop(0, n_ch, drop, 0)

    thr_eff = _tile_lanes(jnp.maximum(thr, INT_MIN + 1), KC // LANES)
    _flash_init(m_ref, l_ref, acc_ref)
    q_all = q_ref[...].reshape(H * QB, 256)

    def chunk(c, carry):
        k0 = pl.multiple_of(c * KC, KC)
        kc = kcat_ref[0, pl.ds(k0, KC), :]
        bias = jnp.where(keys_ref[c] >= thr_eff, 0.0, -jnp.inf)
        s = _dot_nt(q_all, kc) + _tile_rows(bias, H)
        _flash_update(s, kc[:, :DSA_KV_LORA], m_ref, l_ref, acc_ref, 0)
        return carry
    _paired_loop(n_ch, chunk)
    o = acc_ref[...] / jnp.maximum(l_ref[...], 1e-30)
    for h in range(H):
        o_ref[:, h * 128:(h + 1) * 128] = o[h * QB:(h + 1) * QB].astype(BF16)


def _dsa_attn(qcat, iq, iw, kcat, ik, B, S):
    T = B * S
    nq = S // Q_BLOCK
    row = lambda b, q: (b * nq + q, 0)
    bat = lambda b, q: (b, 0, 0)
    k_sel = min(DSA_TOPK, S // 4)
    assert S // LANES < 64, "per-lane key counts are packed 6 bits each"
    kern = functools.partial(_dsa_attn_kernel, k_sel=k_sel, idx_bits=int(S).bit_length())
    return pl.pallas_call(
        kern, grid=(B, nq),
        in_specs=[pl.BlockSpec((DSA_HEADS, Q_BLOCK, 256), lambda b, q: (0, b * nq + q, 0)),
                  pl.BlockSpec((IDX_HEADS, Q_BLOCK, LANES), lambda b, q: (0, b * nq + q, 0)),
                  pl.BlockSpec((Q_BLOCK, LANES), row),
                  pl.BlockSpec((1, S, 256), bat),
                  pl.BlockSpec((1, S, LANES), bat)],
        out_specs=pl.BlockSpec((Q_BLOCK, DSA_HEADS * DSA_KV_LORA), row),
        out_shape=jax.ShapeDtypeStruct((T, DSA_HEADS * DSA_KV_LORA), BF16),
        scratch_shapes=[pltpu.VMEM((S // DSA_KC, Q_BLOCK, DSA_KC), jnp.int32),
                        pltpu.VMEM((IDX_HEADS * Q_BLOCK, DSA_KC), F32),
                        pltpu.VMEM((DSA_HEADS * Q_BLOCK, LANES), F32),
                        pltpu.VMEM((DSA_HEADS * Q_BLOCK, LANES), F32),
                        pltpu.VMEM((DSA_HEADS * Q_BLOCK, DSA_KV_LORA), F32),
                        pltpu.VMEM((Q_BLOCK, LANES), jnp.int32),
                        pltpu.VMEM((Q_BLOCK, LANES), F32)],
        name="dsa_attn")(qcat, iq, iw, kcat.reshape(B, S, 256), ik.reshape(B, S, LANES))


def _wuvo_kernel(uv_ref, wo_ref, o_ref):
    o_ref[0] = jnp.dot(uv_ref[0], wo_ref[...], precision=HIGHEST,
                       preferred_element_type=F32).astype(BF16)


def _wuvo(w_uv, w_o):
    H, C, V = w_uv.shape
    D = w_o.shape[1]
    out = pl.pallas_call(
        _wuvo_kernel, grid=(H,),
        in_specs=[pl.BlockSpec((1, C, V), lambda h: (h, 0, 0)),
                  pl.BlockSpec((V, D), lambda h: (h, 0))],
        out_specs=pl.BlockSpec((1, C, D), lambda h: (h, 0, 0)),
        out_shape=jax.ShapeDtypeStruct((H, C, D), BF16), name="wuvo")(w_uv, w_o)
    return out.reshape(H * C, D)


def _post_kernel(a_ref, w_ref, x_ref, gate_ref, gn_ref, sc_ref, sh_ref, xo_ref, ho_ref):
    x1 = x_ref[...] + gate_ref[0] * _dot(a_ref[...], w_ref[...])
    xo_ref[...] = x1
    ho_ref[...] = _modulate(x1, gn_ref[...], sc_ref[0], sh_ref[0]).astype(ho_ref.dtype)


def _post(a, w, xf, gate, gn, scale, shift, S, h_dtype):
    T, D = xf.shape
    tm = TOKEN_TILE
    nps = S // tm
    row = lambda i: (i, 0)
    bat = lambda i: (i // nps, 0, 0)
    cst2 = lambda i: (0, 0)
    return pl.pallas_call(
        _post_kernel, grid=(T // tm,),
        in_specs=[pl.BlockSpec((tm, a.shape[1]), row), pl.BlockSpec(w.shape, cst2),
                  pl.BlockSpec((tm, D), row), pl.BlockSpec((1, 1, D), bat),
                  pl.BlockSpec((1, D), cst2), pl.BlockSpec((1, 1, D), bat), pl.BlockSpec((1, 1, D), bat)],
        out_specs=[pl.BlockSpec((tm, D), row), pl.BlockSpec((tm, D), row)],
        out_shape=[jax.ShapeDtypeStruct((T, D), F32), jax.ShapeDtypeStruct((T, D), h_dtype)],
        name="post")(a, w, xf, gate, gn, scale, shift)


def _ffn_kernel(h_ref, x_ref, gate_ref, w1_ref, w3_ref, w2_ref, xo_ref, acc_ref):
    j = pl.program_id(1)

    @pl.when(j == 0)
    def _():
        acc_ref[...] = jnp.zeros_like(acc_ref)
    hb = h_ref[...]
    a = _dot(hb, w1_ref[...])
    b = _dot(hb, w3_ref[...])
    acc_ref[...] += _dot((a * _sigmoid(a) * b).astype(BF16), w2_ref[...])

    @pl.when(j == pl.num_programs(1) - 1)
    def _():
        xo_ref[...] = x_ref[...] + gate_ref[0] * acc_ref[...]


def _ffn(h, xf, gate, w1, w3, w2, S, tn):
    T, D = xf.shape
    F = w1.shape[1]
    tm = TOKEN_TILE
    nps = S // tm
    return pl.pallas_call(
        _ffn_kernel, grid=(T // tm, F // tn),
        in_specs=[pl.BlockSpec((tm, D), lambda i, j: (i, 0)),
                  pl.BlockSpec((tm, D), lambda i, j: (i, 0)),
                  pl.BlockSpec((1, 1, D), lambda i, j: (i // nps, 0, 0)),
                  pl.BlockSpec((D, tn), lambda i, j: (0, j)),
                  pl.BlockSpec((D, tn), lambda i, j: (0, j)),
                  pl.BlockSpec((tn, D), lambda i, j: (j, 0))],
        out_specs=pl.BlockSpec((tm, D), lambda i, j: (i, 0)),
        out_shape=jax.ShapeDtypeStruct((T, D), F32),
        scratch_shapes=[pltpu.VMEM((tm, D), F32)],
        name="ffn")(h, xf, gate, w1, w3, w2)


def _nsa_proj_kernel(x_ref, sh_ref, sc_ref, gn_ref, wq_ref, wkv_ref, wg_ref, rc_ref, ra_ref, rb_ref,
                     q_ref, kc_ref, vc_ref, ks_ref, vs_ref, kw_ref, vw_ref, g_ref):
    hb = _modulate(x_ref[...], gn_ref[...], sc_ref[0], sh_ref[0]).astype(BF16)
    rc, ra, rb = rc_ref[...], ra_ref[...], rb_ref[...]

    def roped(v):
        return jnp.concatenate([_rope(v[:, :128], rc, ra, rb), _rope(v[:, 128:], rc, ra, rb)], axis=1)

    for p in range(NSA_HEADS // 2):
        qv = (roped(_dot(hb, wq_ref[:, p * 256:(p + 1) * 256])) * ATT_SCALE).astype(BF16)
        q_ref[2 * p] = qv[:, :128]
        q_ref[2 * p + 1] = qv[:, 128:]
    outs = (kc_ref, vc_ref, ks_ref, vs_ref, kw_ref, vw_ref)
    for n, o_ref in enumerate(outs):
        v = _dot(hb, wkv_ref[:, n * 256:(n + 1) * 256])
        o_ref[...] = (roped(v) if n % 2 == 0 else v).astype(BF16)
    g_ref[...] = _sigmoid(_dot(hb, wg_ref[...]))


def _nsa_proj(xf, shift, scale, gn, wq, wkv, wg, rope, S):
    T, D = xf.shape
    tm = TOKEN_TILE
    nps = S // tm
    row = lambda i: (i, 0)
    bat = lambda i: (i // nps, 0, 0)
    pos = lambda i: (i % nps, 0)
    cst2 = lambda i: (0, 0)
    kv_spec = pl.BlockSpec((tm, 256), row)
    kv_shape = jax.ShapeDtypeStruct((T, 256), BF16)
    return pl.pallas_call(
        _nsa_proj_kernel, grid=(T // tm,),
        in_specs=[pl.BlockSpec((tm, D), row),
                  pl.BlockSpec((1, 1, D), bat), pl.BlockSpec((1, 1, D), bat),
                  pl.BlockSpec((1, D), cst2),
                  pl.BlockSpec(wq.shape, cst2), pl.BlockSpec(wkv.shape, cst2), pl.BlockSpec(wg.shape, cst2),
                  pl.BlockSpec((tm, LANES), pos), pl.BlockSpec((tm, LANES), pos),
                  pl.BlockSpec((tm, LANES), pos)],
        out_specs=[pl.BlockSpec((NSA_HEADS, tm, LANES), lambda i: (0, i, 0))] + [kv_spec] * 6
                  + [pl.BlockSpec((tm, LANES), row)],
        out_shape=[jax.ShapeDtypeStruct((NSA_HEADS, T, LANES), BF16)] + [kv_shape] * 6
                  + [jax.ShapeDtypeStruct((T, LANES), F32)],
        name="nsa_proj")(xf, shift, scale, gn, wq, wkv, wg, *rope)


def _compress_kernel(uk_ref, uv_ref, pe_ref, k1_ref, k2_ref, v1_ref, v2_ref, ko_ref, vo_ref):
    half = CMP_STRIDE * HEAD_DIM
    pe = jnp.broadcast_to(pe_ref[...], (8, 2 * half)).astype(BF16)

    def comp(u_ref, w1_ref, w2_ref):
        bias = _dot(pe, w1_ref[...])[0:1]
        out = None
        for gg in range(2):
            u = u_ref[0, gg]
            a = _dot(u, w1_ref[:half, :])
            b = _dot(u, w1_ref[half:, :])
            nrow = b.shape[0]
            hid = a + pltpu.roll(b, nrow - 1, 0) + bias
            hid = (hid * _sigmoid(hid)).astype(BF16)
            o = _dot(hid, w2_ref[gg])
            out = o if out is None else out + o
        return out
    ko_ref[0] = comp(uk_ref, k1_ref, k2_ref).astype(BF16)
    vo_ref[0] = comp(uv_ref, v1_ref, v2_ref).astype(BF16)


def _compress(uk, uv, pe, k1, k2, v1, v2):
    B, G, nch, W = uk.shape
    u_spec = pl.BlockSpec((1, 2, nch, W), lambda b, p: (b, p, 0, 0))
    w1_spec = pl.BlockSpec(k1.shape, lambda b, p: (0, 0))
    w2_spec = pl.BlockSpec(k2.shape, lambda b, p: (0, 0, 0))
    o_spec = pl.BlockSpec((1, nch, LANES), lambda b, p: (b, 0, p))
    o_shape = jax.ShapeDtypeStruct((B, nch, G * HEAD_DIM), BF16)
    return pl.pallas_call(
        _compress_kernel, grid=(B, G // 2),
        in_specs=[u_spec, u_spec, pl.BlockSpec(pe.shape, lambda b, p: (0, 0)),
                  w1_spec, w2_spec, w1_spec, w2_spec],
        out_specs=[o_spec, o_spec], out_shape=[o_shape, o_shape],
        name="compress")(uk, uv, pe, k1, k2, v1, v2)


def _nsa_attn_kernel(q_ref, g_ref, kcmp_ref, vcmp_ref, ks_ref, vs_ref, kw_ref, vw_ref,
                     covt_ref, exp_ref, o_ref, oc_ref, psum_ref, imp_ref, sel_ref, m_ref, l_ref, acc_ref,
                     osel_ref, *, n_sel):
    QB, KC, HPG = Q_BLOCK, NSA_KC, NSA_HPG
    GR = HPG * QB
    qi = pl.program_id(1)
    qs = qi * QB
    row_t = qs + lax.broadcasted_iota(jnp.int32, (QB, 1), 0)
    nch = kcmp_ref.shape[1]
    nb = covt_ref.shape[0]
    n_ch = qs // KC + 1
    gates = g_ref[...]
    pair = lambda g: slice((g // 2) * 128, (g // 2) * 128 + 128)
    rows = lambda g: slice(g * GR, (g + 1) * GR)
    q_grp = lambda g: q_ref[g * HPG:(g + 1) * HPG].reshape(GR, LANES)

    cmp_end = lax.broadcasted_iota(jnp.int32, (QB, nch), 1) * CMP_STRIDE + (CMP_LEN - 1)
    cbias = _tile_rows(jnp.where(cmp_end <= row_t, 0.0, -jnp.inf), HPG)
    for g in range(NSA_GROUPS):
        s = _dot_nt(q_grp(g), kcmp_ref[0, :, pair(g)]) + cbias
        m = jnp.max(s, axis=1, keepdims=True)
        m = jnp.where(m == -jnp.inf, 0.0, m)
        e = jnp.exp2(s - m)
        p = e * (1.0 / jnp.maximum(jnp.sum(e, axis=1, keepdims=True), 1e-30))
        oc_ref[rows(g), :] = _dot(p.astype(BF16), vcmp_ref[0, :, pair(g)])
        psum_ref[g] = p[0:QB] + p[QB:2 * QB] + p[2 * QB:3 * QB] + p[3 * QB:4 * QB]

    jb = lax.broadcasted_iota(jnp.int32, (nb, QB), 0)
    cur = (qs + lax.broadcasted_iota(jnp.int32, (nb, QB), 1)) // SEL_LEN
    forced = (jb == 0) | (jb == cur) | (jb == cur - 1)
    ri = lax.broadcasted_iota(jnp.int32, (QB, QB), 0)
    ci = lax.broadcasted_iota(jnp.int32, (QB, QB), 1)
    eye = jnp.where(ri == ci, 1.0, 0.0).astype(BF16)
    imps = []
    for g in range(NSA_GROUPS):
        imp = lax.dot_general(covt_ref[...], psum_ref[g], (((1,), (1,)), ((), ())),
                              precision=HIGHEST, preferred_element_type=F32)
        imp = jnp.where(forced, FORCE_SCORE, imp)
        imp = jnp.where(jb <= cur, imp, -jnp.inf)
        imp_ref[g] = imp
        imps.append(imp)

    def rank_body(i, ranks):
        before = jnp.where(i < jb, 1.0, 0.0)
        out = []
        for g in range(NSA_GROUPS):
            ri_ = imp_ref[g, pl.ds(i, 1), :]
            out.append(ranks[g] + jnp.where(ri_ > imps[g], 1.0, 0.0) + jnp.where(ri_ == imps[g], before, 0.0))
        return tuple(out)
    n_live = jnp.minimum((qs + QB - 1) // SEL_LEN + 1, nb)
    ranks = lax.fori_loop(0, n_live, rank_body, (jnp.zeros((nb, QB), F32),) * NSA_GROUPS)
    for g in range(NSA_GROUPS):
        selt = jnp.where(ranks[g] < n_sel, 1.0, 0.0).astype(BF16)
        sel_ref[g] = _dot_nt(eye, selt).astype(BF16)

    _flash_init(m_ref, l_ref, acc_ref)

    def sel_chunk(c, carry):
        k0 = pl.multiple_of(c * KC, KC)
        causal = (k0 + lax.broadcasted_iota(jnp.int32, (QB, KC), 1)) <= row_t
        for g in range(NSA_GROUPS):
            keep = (_dot(sel_ref[g], exp_ref[c]) > 0.5) & causal
            bias = _tile_rows(jnp.where(keep, 0.0, -jnp.inf), HPG)
            s = _dot_nt(q_grp(g), ks_ref[0, pl.ds(k0, KC), pair(g)]) + bias
            _flash_update(s, vs_ref[0, pl.ds(k0, KC), pair(g)], m_ref, l_ref, acc_ref, g * GR)
        return carry
    _paired_loop(n_ch, sel_chunk)
    osel_ref[...] = acc_ref[...] / jnp.maximum(l_ref[...], 1e-30)

    wl = WINDOW + QB
    w0 = pl.multiple_of(jnp.maximum(qs - WINDOW, 0), QB)
    wpos = w0 + lax.broadcasted_iota(jnp.int32, (QB, wl), 1)
    wbias = _tile_rows(jnp.where((wpos <= row_t) & (wpos > row_t - WINDOW), 0.0, -jnp.inf), HPG)
    for g in range(NSA_GROUPS):
        s = _dot_nt(q_grp(g), kw_ref[0, pl.ds(w0, wl), pair(g)]) + wbias
        m = jnp.max(s, axis=1, keepdims=True)
        m = jnp.where(m == -jnp.inf, 0.0, m)
        e = jnp.exp2(s - m)
        den = jnp.maximum(jnp.sum(e, axis=1, keepdims=True), 1e-30)
        acc_ref[rows(g), :] = _dot(e.astype(BF16), vw_ref[0, pl.ds(w0, wl), pair(g)]) / den

    lane = lax.broadcasted_iota(jnp.int32, (QB, LANES), 1)
    pair_out = [None, None]
    for h in range(NSA_HEADS):
        g = h // HPG
        hr = slice(h * QB, (h + 1) * QB)
        o = (gates[:, 3 * h:3 * h + 1] * oc_ref[hr, :] + gates[:, 3 * h + 1:3 * h + 2] * osel_ref[hr, :]
             + gates[:, 3 * h + 2:3 * h + 3] * acc_ref[hr, :])
        pair_out[h % 2] = o
        if h % 2 == 1:
            if g % 2 == 0:
                both = jnp.where(lane < 64, pair_out[0], pltpu.roll(pair_out[1], 64, 1))
            else:
                both = jnp.where(lane < 64, pltpu.roll(pair_out[0], 64, 1), pair_out[1])
            o_ref[:, (h // 2) * 128:(h // 2) * 128 + 128] = both.astype(BF16)


def _nsa_attn(q, gates, kcmp, vcmp, ks, vs, kw, vw, covt, expand, B, S):
    T = B * S
    nq = S // Q_BLOCK
    nch = kcmp.shape[1]
    nb = S // SEL_LEN
    row = lambda b, i: (b * nq + i, 0)
    bat = lambda b, i: (b, 0, 0)
    kv = lambda a: a.reshape(B, S, 256)
    kv_spec = pl.BlockSpec((1, S, 256), bat)
    cmp_spec = pl.BlockSpec((1, nch, 256), bat)
    kern = functools.partial(_nsa_attn_kernel, n_sel=min(SEL_BLOCKS, nb))
    return pl.pallas_call(
        kern, grid=(B, nq),
        in_specs=[pl.BlockSpec((NSA_HEADS, Q_BLOCK, LANES), lambda b, i: (0, b * nq + i, 0)),
                  pl.BlockSpec((Q_BLOCK, LANES), row),
                  cmp_spec, cmp_spec, kv_spec, kv_spec, kv_spec, kv_spec,
                  pl.BlockSpec(covt.shape, lambda b, i: (0, 0)),
                  pl.BlockSpec(expand.shape, lambda b, i: (0, 0, 0))],
        out_specs=pl.BlockSpec((Q_BLOCK, NSA_HEADS * HEAD_DIM), row),
        out_shape=jax.ShapeDtypeStruct((T, NSA_HEADS * HEAD_DIM), BF16),
        scratch_shapes=[pltpu.VMEM((NSA_HEADS * Q_BLOCK, LANES), F32),
                        pltpu.VMEM((NSA_GROUPS, Q_BLOCK, nch), F32),
                        pltpu.VMEM((NSA_GROUPS, nb, Q_BLOCK), F32),
                        pltpu.VMEM((NSA_GROUPS, Q_BLOCK, nb), BF16),
                        pltpu.VMEM((NSA_HEADS * Q_BLOCK, LANES), F32),
                        pltpu.VMEM((NSA_HEADS * Q_BLOCK, LANES), F32),
                        pltpu.VMEM((NSA_HEADS * Q_BLOCK, LANES), F32),
                        pltpu.VMEM((NSA_HEADS * Q_BLOCK, LANES), F32)],
        name="nsa_attn")(q, gates, kcmp, vcmp, kv(ks), kv(vs), kv(kw), kv(vw), covt, expand)


def _router_kernel(h_ref, wr_ref, route_ref, cnt_ref, carry_ref):
    i = pl.program_id(0)
    tm = h_ref.shape[0]

    @pl.when(i == 0)
    def _():
        carry_ref[...] = jnp.zeros_like(carry_ref)
    lane = lax.broadcasted_iota(jnp.int32, (tm, LANES), 1).astype(F32)
    lg = jnp.dot(h_ref[...], wr_ref[...], precision=HIGHEST, preferred_element_type=F32)
    lg = jnp.where(lane < N_EXPERTS, lg, -jnp.inf)
    v1 = jnp.max(lg, axis=1, keepdims=True)
    i1 = jnp.min(jnp.where(lg == v1, lane, float(LANES)), axis=1, keepdims=True)
    lg2 = jnp.where(lane == i1, -jnp.inf, lg)
    v2 = jnp.max(lg2, axis=1, keepdims=True)
    i2 = jnp.min(jnp.where(lg2 == v2, lane, float(LANES)), axis=1, keepdims=True)
    e2 = jnp.exp(v2 - v1)
    g1 = 1.0 / (1.0 + e2)
    g2 = e2 / (1.0 + e2)
    oh1 = jnp.where(lane == i1, 1.0, 0.0)
    oh2 = jnp.where(lane == i2, 1.0, 0.0)
    both = oh1 + oh2
    ri = lax.broadcasted_iota(jnp.int32, (tm, tm), 0)
    ci = lax.broadcasted_iota(jnp.int32, (tm, tm), 1)
    lower = jnp.where(ri > ci, 1.0, 0.0).astype(BF16)
    tot = carry_ref[0:1, :] + _dot(lower, both.astype(BF16))
    r1 = jnp.sum(oh1 * tot, axis=1, keepdims=True)
    r2 = jnp.sum(oh2 * tot, axis=1, keepdims=True)
    new_carry = carry_ref[...] + jnp.sum(both, axis=0, keepdims=True)
    carry_ref[...] = new_carry
    cnt_ref[...] = new_carry
    out = jnp.zeros((tm, LANES), F32)
    for col, val in enumerate((i1, i2, g1, g2, r1, r2)):
        out = jnp.where(lane == col, val, out)
    route_ref[...] = out


def _router(h, wr):
    T, D = h.shape
    tm = TOKEN_TILE
    return pl.pallas_call(
        _router_kernel, grid=(T // tm,),
        in_specs=[pl.BlockSpec((tm, D), lambda i: (i, 0)), pl.BlockSpec(wr.shape, lambda i: (0, 0))],
        out_specs=[pl.BlockSpec((tm, LANES), lambda i: (i, 0)), pl.BlockSpec((8, LANES), lambda i: (0, 0))],
        out_shape=[jax.ShapeDtypeStruct((T, LANES), F32), jax.ShapeDtypeStruct((8, LANES), F32)],
        scratch_shapes=[pltpu.VMEM((8, LANES), F32)],
        name="router")(h, wr)


def _row_copy(src, dst, sem):
    return pltpu.make_async_copy(src, dst, sem)


def _moe_ffn_kernel(be_ref, nu_ref, src_ref, dst_ref, h_ref, w1_ref, w3_ref, w2_ref, y_ref,
                    xin_ref, xb_ref, acc_ref, yout_ref, sem_in, sem_out, *, nj):
    i = pl.program_id(0)
    j = pl.program_id(1)
    nblk = src_ref.shape[0]
    issue_steps = 4
    per_step = MOE_BLOCK // issue_steps
    assert nj > issue_steps and per_step * issue_steps == MOE_BLOCK
    used = i < nu_ref[0]
    slot = i % 2
    nxt = jnp.minimum(i + 1, nblk - 1)
    prv = jnp.maximum(i - 1, 0)

    def gather(blk, r, buf):
        return _row_copy(h_ref.at[pl.ds(src_ref[blk, r], 1), :], xin_ref.at[buf, pl.ds(r, 1), :],
                         sem_in.at[buf])

    def scatter(blk, r):
        return _row_copy(yout_ref.at[pl.ds(r, 1), :], y_ref.at[pl.ds(dst_ref[blk, r], 1), :], sem_out)

    def issue(r0, n):
        for u in range(n):
            gather(nxt, r0 + u, 1 - slot).start(priority=1)
            scatter(prv, r0 + u).start(priority=1)

    def wait_gathered(buf):
        _row_copy(h_ref.at[pl.ds(0, MOE_BLOCK), :], xin_ref.at[buf], sem_in.at[buf]).wait()

    def wait_scattered():
        _row_copy(yout_ref, y_ref.at[pl.ds(0, MOE_BLOCK), :], sem_out).wait()

    @pl.when((i == 0) & (j == 0))
    def _():
        yout_ref[...] = jnp.zeros_like(yout_ref)

        def first(r, carry):
            gather(0, r, 0).start()
            return carry
        lax.fori_loop(0, MOE_BLOCK, first, 0)

    @pl.when(j == 0)
    def _():
        wait_gathered(slot)
        xb_ref[...] = xin_ref[slot].astype(BF16)
        acc_ref[...] = jnp.zeros_like(acc_ref)

    def expert_step():
        xb = xb_ref[...]
        a = _dot(xb, w1_ref[0])
        b = _dot(xb, w3_ref[0])
        acc_ref[...] += _dot((a * _sigmoid(a) * b).astype(BF16), w2_ref[0])

    issuing = j < issue_steps

    @pl.when(used & issuing)
    def _():
        issue(j * per_step, per_step)
        expert_step()

    @pl.when(used & jnp.logical_not(issuing))
    def _():
        expert_step()

    @pl.when(jnp.logical_not(used) & issuing)
    def _():
        issue(j * per_step, per_step)

    @pl.when(j == nj - 1)
    def _():
        wait_scattered()
        yout_ref[...] = acc_ref[...]

    @pl.when((i == pl.num_programs(0) - 1) & (j == nj - 1))
    def _():
        wait_gathered(1 - slot)


def _moe_ffn(block_e, n_used, slot_src, slot_dst, h, w1, w3, w2, tn):
    T, D = h.shape
    nblk = slot_src.shape[0]
    E, _, F = w1.shape
    nj = F // tn
    bi = lambda i: jnp.minimum(i, nblk - 1)
    jj = lambda i, j, nu: jnp.where(i < nu[0], j, nj - 1)
    return pl.pallas_call(
        functools.partial(_moe_ffn_kernel, nj=nj),
        grid_spec=pltpu.PrefetchScalarGridSpec(
            num_scalar_prefetch=4, grid=(nblk + 1, nj),
            in_specs=[pl.BlockSpec(memory_space=pl.ANY),
                      pl.BlockSpec((1, D, tn), lambda i, j, be, nu, s, d: (be[bi(i)], 0, jj(i, j, nu))),
                      pl.BlockSpec((1, D, tn), lambda i, j, be, nu, s, d: (be[bi(i)], 0, jj(i, j, nu))),
                      pl.BlockSpec((1, tn, D), lambda i, j, be, nu, s, d: (be[bi(i)], jj(i, j, nu), 0))],
            out_specs=pl.BlockSpec(memory_space=pl.ANY),
            scratch_shapes=[pltpu.VMEM((2, MOE_BLOCK, D), F32), pltpu.VMEM((MOE_BLOCK, D), BF16),
                            pltpu.VMEM((MOE_BLOCK, D), F32), pltpu.VMEM((MOE_BLOCK, D), F32),
                            pltpu.SemaphoreType.DMA((2,)), pltpu.SemaphoreType.DMA(())]),
        out_shape=jax.ShapeDtypeStruct((2 * T + MOE_BLOCK, D), F32),
        name="moe_ffn")(block_e, n_used, slot_src, slot_dst, h, w1, w3, w2)


def _moe_combine_kernel(y0_ref, y1_ref, x_ref, gate_ref, route_ref, fn_ref, o_ref):
    route = route_ref[...]
    y = route[:, 2:3] * y0_ref[...] + route[:, 3:4] * y1_ref[...]
    o_ref[...] = _rms(x_ref[...] + gate_ref[0] * y, fn_ref[...])


def _moe_combine(y, xf, gate, route, fn, S):
    T, D = xf.shape
    tm = TOKEN_TILE
    nps = S // tm
    nt = T // tm
    return pl.pallas_call(
        _moe_combine_kernel, grid=(nt,),
        in_specs=[pl.BlockSpec((tm, D), lambda i: (i, 0)),
                  pl.BlockSpec((tm, D), lambda i: (nt + i, 0)),
                  pl.BlockSpec((tm, D), lambda i: (i, 0)),
                  pl.BlockSpec((1, 1, D), lambda i: (i // nps, 0, 0)),
                  pl.BlockSpec((tm, LANES), lambda i: (i, 0)),
                  pl.BlockSpec((1, D), lambda i: (0, 0))],
        out_specs=pl.BlockSpec((tm, D), lambda i: (i, 0)),
        out_shape=jax.ShapeDtypeStruct((T, D), F32), name="moe_combine")(y, y, xf, gate, route, fn)


def _rope_tables(S):
    inv = ROPE_THETA ** (-jnp.arange(0, ROPE_DIM, 2, dtype=F32) / ROPE_DIM)
    ang = jnp.arange(S, dtype=F32)[:, None] * inv[None, :]
    cos, sin = jnp.cos(ang), jnp.sin(ang)
    pm = np.arange(LANES) % HEAD_DIM
    col = pm % (ROPE_DIM // 2)
    rc = jnp.where((pm < ROPE_DIM)[None, :], cos[:, col], 1.0)
    ra = jnp.where((pm < ROPE_DIM // 2)[None, :], -sin[:, col], 0.0)
    rb = jnp.where(((pm >= ROPE_DIM // 2) & (pm < ROPE_DIM))[None, :], sin[:, col], 0.0)
    return rc, ra, rb


def _dsa_weights(w_in, w_uk, w_iq):
    D = w_in.shape[0]
    a, b, c, d = DSA_Q_LORA, DSA_Q_LORA + DSA_KV_LORA, DSA_Q_LORA + DSA_KV_LORA + ROPE_DIM, \
        DSA_Q_LORA + DSA_KV_LORA + ROPE_DIM + IDX_DIM
    win = jnp.concatenate([w_in[:, :b], w_in[:, c:d], w_in[:, b:c], w_in[:, d:],
                           jnp.zeros((D, 512 - w_in.shape[1]), F32)], axis=1).astype(BF16)
    H = DSA_HEADS
    blk = jnp.zeros((H, HEAD_DIM, 256), F32)
    blk = blk.at[:, ROPE_DIM:, :DSA_KV_LORA].set(jnp.transpose(w_uk, (0, 2, 1)))
    blk = blk.at[:, :ROPE_DIM, 192:192 + ROPE_DIM].set(jnp.eye(ROPE_DIM, dtype=F32))
    z = jnp.zeros((H // 2, HEAD_DIM, 256), F32)
    wcat = jnp.concatenate([jnp.concatenate([blk[0::2], z], axis=2),
                            jnp.concatenate([z, blk[1::2]], axis=2)], axis=1).astype(BF16)
    wiq = w_iq.reshape(DSA_Q_LORA, IDX_HEADS, IDX_DIM)
    wiq = jnp.concatenate([wiq, jnp.zeros_like(wiq)], axis=2).reshape(DSA_Q_LORA, IDX_HEADS * 128)
    return win, wcat, wiq.astype(BF16)


def _nsa_weights(w_in):
    D = w_in.shape[0]
    nq = NSA_HEADS * HEAD_DIM
    wq = w_in[:, :nq].reshape(D, NSA_HEADS, HEAD_DIM)
    z = jnp.zeros_like(wq)
    odd = ((np.arange(NSA_HEADS) // NSA_HPG) % 2 == 1)[None, :, None]
    wq = jnp.concatenate([jnp.where(odd, z, wq), jnp.where(odd, wq, z)], axis=2).reshape(D, NSA_HEADS * 128)
    wkv = w_in[:, nq:nq + 6 * 256]
    wg = jnp.concatenate([w_in[:, nq + 6 * 256:], jnp.zeros((D, LANES - 3 * NSA_HEADS), F32)], axis=1)
    return wq.astype(BF16), wkv.astype(BF16), wg.astype(BF16)


def _nsa_tables(S):
    nch = S // CMP_STRIDE
    nc = (S - CMP_LEN) // CMP_STRIDE + 1
    nb = S // SEL_LEN
    cstart = np.arange(nch) * CMP_STRIDE
    bstart = np.arange(nb) * SEL_LEN
    cov = ((cstart[None, :] < bstart[:, None] + SEL_LEN) & (cstart[None, :] + CMP_LEN > bstart[:, None])
           & (np.arange(nch)[None, :] < nc)).astype(np.float32)
    kpos = np.arange(S).reshape(S // NSA_KC, 1, NSA_KC)
    expand = (kpos // SEL_LEN == np.arange(nb)[None, :, None]).astype(np.float32)
    return jnp.asarray(cov), jnp.asarray(expand, dtype=BF16)


def _slot_tables(dest, T, n_blocks):
    ns = n_blocks * MOE_BLOCK
    asg = jnp.full((ns,), -1, jnp.int32).at[dest].set(jnp.arange(2 * T, dtype=jnp.int32))
    real = asg >= 0
    src = jnp.where(real, asg // 2, 0)
    spare = 2 * T + jnp.arange(ns, dtype=jnp.int32) % MOE_BLOCK
    dst = jnp.where(real, (asg % 2) * T + asg // 2, spare)
    return src.reshape(n_blocks, MOE_BLOCK), dst.reshape(n_blocks, MOE_BLOCK)


def _chunk_tokens(a, B, S):
    a = a.reshape(B, S // CMP_STRIDE, CMP_STRIDE, NSA_GROUPS, HEAD_DIM)
    return jnp.transpose(a, (0, 3, 1, 2, 4)).reshape(B, NSA_GROUPS, S // CMP_STRIDE, CMP_STRIDE * HEAD_DIM)


def kernel(x, c, norm_mix, norm_ffn, ada_w, ada_b, final_norm, dsa_w_in, dsa_g_q, dsa_w_uq, dsa_g_kv,
           dsa_w_uk, dsa_w_uv, dsa_w_iq, dsa_w_o, ffn_w1, ffn_w3, ffn_w2, nsa_w_in, nsa_cmp_pe,
           nsa_cmp_k1, nsa_cmp_k2, nsa_cmp_v1, nsa_cmp_v2, nsa_w_o, moe_router, moe_w1, moe_w3, moe_w2):
    B, S, D = x.shape
    T = B * S
    xf = x.reshape(T, D)
    mods = _ada(c, ada_w, ada_b).reshape(4, B, 3, 1, D)
    shift = lambda s: mods[s, :, 0]
    scale = lambda s: mods[s, :, 1]
    gate = lambda s: mods[s, :, 2]
    rope = _rope_tables(S)

    win, wcat, wiq = _dsa_weights(dsa_w_in[0], dsa_w_uk[0], dsa_w_iq[0])
    qcat, iq, kcat, ik, iw = _dsa_proj(
        xf, shift(0), scale(0), norm_mix[0:1], win, dsa_g_q[0:1], dsa_g_kv[0:1],
        dsa_w_uq[0].astype(BF16), wiq, wcat, rope, S)
    olat = _dsa_attn(qcat, iq, iw, kcat, ik, B, S)
    wuvo = _wuvo(dsa_w_uv[0], dsa_w_o[0])
    x1, h1 = _post(olat, wuvo, xf, gate(0), norm_ffn[0:1], scale(1), shift(1), S, BF16)
    x2 = _ffn(h1, x1, gate(1), ffn_w1[0].astype(BF16), ffn_w3[0].astype(BF16), ffn_w2[0].astype(BF16),
              S, ffn_w1.shape[2] // 2)

    wq, wkv, wg = _nsa_weights(nsa_w_in[0])
    q, kc, vc, ks, vs, kw, vw, gates = _nsa_proj(xf=x2, shift=shift(2), scale=scale(2), gn=norm_mix[1:2],
                                                 wq=wq, wkv=wkv, wg=wg, rope=rope, S=S)
    zpad = jnp.zeros((CMP_HIDDEN, HEAD_DIM), F32)
    pad2 = lambda w2: jnp.stack([jnp.concatenate([w2, zpad], axis=1),
                                 jnp.concatenate([zpad, w2], axis=1)]).astype(BF16)
    kcmp, vcmp = _compress(_chunk_tokens(kc, B, S), _chunk_tokens(vc, B, S),
                           nsa_cmp_pe[0].reshape(1, CMP_LEN * HEAD_DIM),
                           nsa_cmp_k1[0].astype(BF16), pad2(nsa_cmp_k2[0]),
                           nsa_cmp_v1[0].astype(BF16), pad2(nsa_cmp_v2[0]))
    covt, expand = _nsa_tables(S)
    o = _nsa_attn(q, gates, kcmp, vcmp, ks, vs, kw, vw, covt, expand, B, S)
    x3, h3 = _post(o, nsa_w_o[0].astype(BF16), x2, gate(2), norm_ffn[1:2], scale(3), shift(3), S, F32)

    wr = jnp.concatenate([moe_router[0], jnp.zeros((D, LANES - N_EXPERTS), F32)], axis=1)
    route, cnt = _router(h3, wr)
    counts = cnt[0, :N_EXPERTS].astype(jnp.int32)
    padded = (counts + MOE_BLOCK - 1) // MOE_BLOCK * MOE_BLOCK
    ends = jnp.cumsum(padded)
    pstart = ends - padded
    eidx = route[:, 0:2].astype(jnp.int32)
    dest = (pstart[eidx] + route[:, 4:6].astype(jnp.int32)).reshape(-1)
    n_blocks = -(-(T * 2) // MOE_BLOCK) + N_EXPERTS
    block_start = jnp.arange(n_blocks, dtype=jnp.int32) * MOE_BLOCK
    block_e = jnp.minimum(jnp.sum((ends[None, :] <= block_start[:, None]).astype(jnp.int32), axis=1),
                          N_EXPERTS - 1)
    n_used = (ends[-1:] // MOE_BLOCK).astype(jnp.int32)
    slot_src, slot_dst = _slot_tables(dest, T, n_blocks)
    y = _moe_ffn(block_e, n_used, slot_src, slot_dst, h3, moe_w1[0].astype(BF16), moe_w3[0].astype(BF16),
                 moe_w2[0].astype(BF16), 512)
    out = _moe_combine(y, x3, gate(3), route, final_norm.reshape(1, D), S)
    return out.reshape(B, S, D)
```

```python
import functools

import numpy as np
import jax
import jax.numpy as jnp
from jax import lax
from jax.experimental import pallas as pl
from jax.experimental.pallas import tpu as pltpu

F32 = jnp.float32
BF16 = jnp.bfloat16
HIGHEST = lax.Precision.HIGHEST
INT_MIN = -2147483648

HEAD_DIM = 64
ROPE_DIM = 16
ROPE_THETA = 500000.0
Q_BLOCK = 128
NORM_EPS = 1e-6

DSA_HEADS = 16
DSA_NOPE = 48
DSA_Q_LORA = 256
DSA_KV_LORA = 128
IDX_HEADS = 8
IDX_DIM = 64
DSA_TOPK = 256
DSA_KC = 512

NSA_HEADS = 16
NSA_GROUPS = 4
NSA_HPG = 4
CMP_LEN = 32
CMP_STRIDE = 16
CMP_HIDDEN = 256
SEL_LEN = 64
SEL_BLOCKS = 16
WINDOW = 512
FORCE_SCORE = 1e4
NSA_KC = 512

N_EXPERTS = 8
MOE_BLOCK = 512
SRC_BITS = 15
TOKEN_TILE = 512
ROW_TILE = 256
LANES = 128

LOG2E = 1.4426950408889634
ATT_SCALE = HEAD_DIM ** -0.5 * LOG2E


def _dot(a, b):
    return jnp.dot(a, b, preferred_element_type=F32)


def _dot_nt(a, b):
    return lax.dot_general(a, b, (((1,), (1,)), ((), ())), preferred_element_type=F32)


def _sigmoid(v):
    return 1.0 / (1.0 + jnp.exp(-v))


def _rms(v, g):
    return v * lax.rsqrt(jnp.mean(v * v, axis=-1, keepdims=True) + NORM_EPS) * g


def _modulate(v, g, scale, shift):
    return _rms(v, g) * (1.0 + scale) + shift


def _rope(v, c, sa, sb):
    return v * c + pltpu.roll(v, LANES - 8, 1) * sa + pltpu.roll(v, 8, 1) * sb


def _tile_rows(a, n):
    return jnp.concatenate([a] * n, axis=0) if n > 1 else a


def _tile_lanes(a, n):
    return jnp.concatenate([a] * n, axis=1) if n > 1 else a


def _paired_loop(n, body):
    def two(i, carry):
        return body(2 * i + 1, body(2 * i, carry))
    lax.fori_loop(0, n // 2, two, 0)

    @pl.when(n % 2 == 1)
    def _():
        body(n - 1, 0)


def _flash_init(m_ref, l_ref, acc_ref):
    m_ref[...] = jnp.full(m_ref.shape, -1e30, F32)
    l_ref[...] = jnp.zeros(l_ref.shape, F32)
    acc_ref[...] = jnp.zeros(acc_ref.shape, F32)


def _flash_update(s, v, m_ref, l_ref, acc_ref, row0):
    n = s.shape[0]
    ps = []
    for r in range(0, n, Q_BLOCK):
        rs = slice(row0 + r, row0 + r + Q_BLOCK)
        sl = s[r:r + Q_BLOCK]
        m_old = m_ref[rs, :]
        m_new = jnp.maximum(m_old, jnp.max(sl, axis=1, keepdims=True))
        p = jnp.exp2(sl - _tile_lanes(m_new, sl.shape[1] // LANES))
        alpha = jnp.exp2(m_old - m_new)
        l_ref[rs, :] = alpha * l_ref[rs, :] + jnp.sum(p, axis=1, keepdims=True)
        acc_ref[rs, :] = alpha * acc_ref[rs, :]
        m_ref[rs, :] = m_new
        ps.append(p.astype(BF16))
    pv = _dot(jnp.concatenate(ps, axis=0), v)
    acc_ref[row0:row0 + n, :] += pv


def _ada_kernel(c_ref, w_ref, b_ref, o_ref):
    cv = c_ref[...]
    sc = cv * _sigmoid(cv)
    o_ref[0] = jnp.dot(sc, w_ref[0], precision=HIGHEST, preferred_element_type=F32) + b_ref[0]


def _ada(c, ada_w, ada_b):
    B, D = c.shape
    w = ada_w.reshape(4, D, 3 * D)
    b = ada_b.reshape(4, 1, 3 * D)
    return pl.pallas_call(
        _ada_kernel, grid=(4, 3),
        in_specs=[pl.BlockSpec((B, D), lambda l, j: (0, 0)),
                  pl.BlockSpec((1, D, D), lambda l, j: (l, 0, j)),
                  pl.BlockSpec((1, 1, D), lambda l, j: (l, 0, j))],
        out_specs=pl.BlockSpec((1, B, D), lambda l, j: (l, 0, j)),
        out_shape=jax.ShapeDtypeStruct((4, B, 3 * D), F32), name="ada")(c, w, b)


def _dsa_proj_kernel(x_ref, sh_ref, sc_ref, gn_ref, win_ref, gq_ref, gkv_ref, wuq_ref, wiq_ref,
                     wcat_ref, rc_ref, ra_ref, rb_ref, qcat_ref, iq_ref, kcat_ref, ik_ref, iw_ref):
    h = _modulate(x_ref[...], gn_ref[...], sc_ref[0], sh_ref[0])
    proj = _dot(h.astype(BF16), win_ref[...])
    q_lat = _rms(proj[:, :256], gq_ref[...]).astype(BF16)
    c_kv = _rms(proj[:, 256:384], gkv_ref[...])
    rc, ra, rb = rc_ref[...], ra_ref[...], rb_ref[...]
    rest = _rope(proj[:, 384:512], rc, ra, rb)
    lane = lax.broadcasted_iota(jnp.int32, rest.shape, 1)
    kcat_ref[:, :128] = c_kv.astype(BF16)
    kcat_ref[:, 128:] = jnp.where((lane >= 64) & (lane < 80), rest, 0.0).astype(BF16)
    ik_ref[...] = jnp.where(lane < 64, rest, 0.0).astype(BF16)
    iw_ref[...] = rest * (IDX_HEADS ** -0.5)
    q = _dot(q_lat, wuq_ref[...])
    for p in range(DSA_HEADS // 2):
        qp = (_rope(q[:, p * 128:(p + 1) * 128], rc, ra, rb) * ATT_SCALE).astype(BF16)
        res = _dot(qp, wcat_ref[p]).astype(BF16)
        qcat_ref[2 * p] = res[:, :256]
        qcat_ref[2 * p + 1] = res[:, 256:]
    iqv = _dot(q_lat, wiq_ref[...])
    for hh in range(IDX_HEADS):
        iq_ref[hh] = (_rope(iqv[:, hh * 128:(hh + 1) * 128], rc, ra, rb) * (IDX_DIM ** -0.5)).astype(BF16)


def _dsa_proj(xf, shift, scale, gn, win, gq, gkv, wuq, wiq, wcat, rope, S):
    T, D = xf.shape
    tm = TOKEN_TILE
    nps = S // tm
    row = lambda i: (i, 0)
    bat = lambda i: (i // nps, 0, 0)
    pos = lambda i: (i % nps, 0)
    cst2 = lambda i: (0, 0)
    cst3 = lambda i: (0, 0, 0)
    return pl.pallas_call(
        _dsa_proj_kernel, grid=(T // tm,),
        in_specs=[pl.BlockSpec((tm, D), row),
                  pl.BlockSpec((1, 1, D), bat), pl.BlockSpec((1, 1, D), bat),
                  pl.BlockSpec((1, D), cst2),
                  pl.BlockSpec(win.shape, cst2),
                  pl.BlockSpec((1, DSA_Q_LORA), cst2), pl.BlockSpec((1, DSA_KV_LORA), cst2),
                  pl.BlockSpec(wuq.shape, cst2), pl.BlockSpec(wiq.shape, cst2),
                  pl.BlockSpec(wcat.shape, cst3),
                  pl.BlockSpec((tm, LANES), pos), pl.BlockSpec((tm, LANES), pos),
                  pl.BlockSpec((tm, LANES), pos)],
        out_specs=[pl.BlockSpec((DSA_HEADS, tm, 256), lambda i: (0, i, 0)),
                   pl.BlockSpec((IDX_HEADS, tm, LANES), lambda i: (0, i, 0)),
                   pl.BlockSpec((tm, 256), row),
                   pl.BlockSpec((tm, LANES), row),
                   pl.BlockSpec((tm, LANES), row)],
        out_shape=[jax.ShapeDtypeStruct((DSA_HEADS, T, 256), BF16),
                   jax.ShapeDtypeStruct((IDX_HEADS, T, LANES), BF16),
                   jax.ShapeDtypeStruct((T, 256), BF16),
                   jax.ShapeDtypeStruct((T, LANES), BF16),
                   jax.ShapeDtypeStruct((T, LANES), F32)],
        name="dsa_proj")(xf, shift, scale, gn, win, gq, gkv, wuq, wiq, wcat, *rope)


def _dsa_attn_kernel(q_ref, iq_ref, iw_ref, kcat_ref, ik_ref, o_ref, keys_ref, w_ref, m_ref, l_ref, acc_ref,
                     thr_ref, cnt_ref, *, k_sel, idx_bits):
    QB, KC, H = Q_BLOCK, DSA_KC, DSA_HEADS
    qi = pl.program_id(1)
    n_ch = (qi * QB) // KC + 1
    row_tl = qi * QB + lax.broadcasted_iota(jnp.int32, (QB, LANES), 0)
    iw = iw_ref[...]
    for hh in range(IDX_HEADS):
        w_ref[hh * QB:(hh + 1) * QB, :] = jnp.broadcast_to(iw[:, 80 + hh:81 + hh], (QB, KC))
    iq_all = iq_ref[...].reshape(IDX_HEADS * QB, LANES)

    def score_chunk(c, carry):
        k0 = pl.multiple_of(c * KC, KC)
        ikc = ik_ref[0, pl.ds(k0, KC), :]
        r = jnp.maximum(_dot_nt(iq_all, ikc), 0.0) * w_ref[...]
        sc = r[0:QB]
        for hh in range(1, IDX_HEADS):
            sc = sc + r[hh * QB:(hh + 1) * QB]
        sc = sc + 0.0
        bits = pltpu.bitcast(sc, jnp.int32)
        key = jnp.where(bits < 0, bits ^ 0x7FFFFFFF, bits)
        pos = k0 + lax.broadcasted_iota(jnp.int32, (QB, KC), 1)
        keys_ref[c] = jnp.where(pos <= _tile_lanes(row_tl, KC // LANES), key, INT_MIN)
        return carry
    _paired_loop(n_ch, score_chunk)

    lane_l = lax.broadcasted_iota(jnp.int32, (QB, LANES), 1)

    def count(pred):
        def body(c, a):
            for j in range(KC // LANES):
                kk = keys_ref[c, :, j * LANES:(j + 1) * LANES]
                a = a + jnp.where(pred(kk, c * KC + j * LANES + lane_l), 1.0, 0.0)
            return a
        a = lax.fori_loop(0, n_ch, body, jnp.zeros((QB, LANES), F32))
        return jnp.sum(a, axis=1, keepdims=True)

    def count3(c1, c2, c3):
        def body(c, a):
            for j in range(KC // LANES):
                kk = keys_ref[c, :, j * LANES:(j + 1) * LANES]
                a = a + jnp.where(kk >= c3, 4161, jnp.where(kk >= c2, 65, jnp.where(kk >= c1, 1, 0)))
            return a
        a = lax.fori_loop(0, n_ch, body, jnp.zeros((QB, LANES), jnp.int32))
        lane_sum = lambda x: jnp.sum(x.astype(F32), axis=1, keepdims=True)
        return lane_sum(a & 63), lane_sum((a >> 6) & 63), lane_sum(a >> 12)

    def bit_body(it, carry):
        thr, cnt = carry
        b1 = jnp.left_shift(jnp.int32(1), 31 - 2 * it)
        b0 = jnp.left_shift(jnp.int32(1), 30 - 2 * it)
        c1, c2 = thr ^ b0, thr ^ b1
        c3 = c2 ^ b0
        n1, n2, n3 = count3(c1, c2, c3)
        pick = lambda x3, x2, x1, x0: jnp.where(n3 >= k_sel, x3, jnp.where(n2 >= k_sel, x2,
                                                                           jnp.where(n1 >= k_sel, x1, x0)))
        return pick(c3, c2, c1, thr), pick(n3, n2, n1, cnt)

    def sweeps(first, n):
        thr, cnt = lax.fori_loop(first, first + n, bit_body, (thr_ref[...], cnt_ref[...]))
        thr_ref[...] = thr
        cnt_ref[...] = cnt
    thr_ref[...] = jnp.full((QB, LANES), INT_MIN, jnp.int32)
    cnt_ref[...] = jnp.full((QB, LANES), -1.0, F32)
    sweeps(0, 10)
    few = (row_tl + 1) < k_sel
    for stage in range(3):
        unsettled = jnp.where((cnt_ref[...] == k_sel) | few, 0.0, 1.0)

        @pl.when(jnp.max(unsettled) > 0.0)
        def _():
            sweeps(10 + 2 * stage, 2)
    thr = thr_ref[...]

    n_gt = count(lambda kk, pos: kk > thr)
    n_ge = count(lambda kk, pos: kk >= thr)
    tie = jnp.where((n_ge > k_sel) & (thr > INT_MIN), 1.0, 0.0)

    @pl.when(jnp.max(tie) > 0.0)
    def _():
        need = k_sel - n_gt
        def jbit(it, jcut):
            cand = jcut | jnp.left_shift(jnp.int32(1), idx_bits - 1 - it)
            f = count(lambda kk, pos: (kk == thr) & (pos < cand))
            return jnp.where(f <= need, cand, jcut)
        jcut = lax.fori_loop(0, idx_bits, jbit, jnp.zeros((QB, LANES), jnp.int32))
        def drop(c, carry):
            for j in range(KC // LANES):
                sl = slice(j * LANES, (j + 1) * LANES)
                kk = keys_ref[c, :, sl]
                pos = c * KC + j * LANES + lane_l
                keys_ref[c, :, sl] = jnp.where((kk == thr) & (pos >= jcut), INT_MIN, kk)
            return carry
        lax.fori_loop(0, n_ch, drop, 0)

    thr_eff = _tile_lanes(jnp.maximum(thr, INT_MIN + 1), KC // LANES)
    _flash_init(m_ref, l_ref, acc_ref)
    q_all = q_ref[...].reshape(H * QB, 256)

    def chunk(c, carry):
        k0 = pl.multiple_of(c * KC, KC)
        kc = kcat_ref[0, pl.ds(k0, KC), :]
        bias = jnp.where(keys_ref[c] >= thr_eff, 0.0, -jnp.inf)
        s = _dot_nt(q_all, kc) + _tile_rows(bias, H)
        _flash_update(s, kc[:, :DSA_KV_LORA], m_ref, l_ref, acc_ref, 0)
        return carry
    _paired_loop(n_ch, chunk)
    o = acc_ref[...] / jnp.maximum(l_ref[...], 1e-30)
    for h in range(H):
        o_ref[:, h * 128:(h + 1) * 128] = o[h * QB:(h + 1) * QB].astype(BF16)


def _dsa_attn(qcat, iq, iw, kcat, ik, B, S):
    T = B * S
    nq = S // Q_BLOCK
    row = lambda b, q: (b * nq + q, 0)
    bat = lambda b, q: (b, 0, 0)
    k_sel = min(DSA_TOPK, S // 4)
    assert S // LANES < 64, "per-lane key counts are packed 6 bits each"
    kern = functools.partial(_dsa_attn_kernel, k_sel=k_sel, idx_bits=int(S).bit_length())
    return pl.pallas_call(
        kern, grid=(B, nq),
        in_specs=[pl.BlockSpec((DSA_HEADS, Q_BLOCK, 256), lambda b, q: (0, b * nq + q, 0)),
                  pl.BlockSpec((IDX_HEADS, Q_BLOCK, LANES), lambda b, q: (0, b * nq + q, 0)),
                  pl.BlockSpec((Q_BLOCK, LANES), row),
                  pl.BlockSpec((1, S, 256), bat),
                  pl.BlockSpec((1, S, LANES), bat)],
        out_specs=pl.BlockSpec((Q_BLOCK, DSA_HEADS * DSA_KV_LORA), row),
        out_shape=jax.ShapeDtypeStruct((T, DSA_HEADS * DSA_KV_LORA), BF16),
        scratch_shapes=[pltpu.VMEM((S // DSA_KC, Q_BLOCK, DSA_KC), jnp.int32),
                        pltpu.VMEM((IDX_HEADS * Q_BLOCK, DSA_KC), F32),
                        pltpu.VMEM((DSA_HEADS * Q_BLOCK, LANES), F32),
                        pltpu.VMEM((DSA_HEADS * Q_BLOCK, LANES), F32),
                        pltpu.VMEM((DSA_HEADS * Q_BLOCK, DSA_KV_LORA), F32),
                        pltpu.VMEM((Q_BLOCK, LANES), jnp.int32),
                        pltpu.VMEM((Q_BLOCK, LANES), F32)],
        name="dsa_attn")(qcat, iq, iw, kcat.reshape(B, S, 256), ik.reshape(B, S, LANES))


def _wuvo_kernel(uv_ref, wo_ref, o_ref):
    o_ref[0] = jnp.dot(uv_ref[0], wo_ref[...], precision=HIGHEST,
                       preferred_element_type=F32).astype(BF16)


def _wuvo(w_uv, w_o):
    H, C, V = w_uv.shape
    D = w_o.shape[1]
    out = pl.pallas_call(
        _wuvo_kernel, grid=(H,),
        in_specs=[pl.BlockSpec((1, C, V), lambda h: (h, 0, 0)),
                  pl.BlockSpec((V, D), lambda h: (h, 0))],
        out_specs=pl.BlockSpec((1, C, D), lambda h: (h, 0, 0)),
        out_shape=jax.ShapeDtypeStruct((H, C, D), BF16), name="wuvo")(w_uv, w_o)
    return out.reshape(H * C, D)


def _post_kernel(a_ref, w_ref, x_ref, gate_ref, gn_ref, sc_ref, sh_ref, xo_ref, ho_ref):
    x1 = x_ref[...] + gate_ref[0] * _dot(a_ref[...], w_ref[...])
    xo_ref[...] = x1
    ho_ref[...] = _modulate(x1, gn_ref[...], sc_ref[0], sh_ref[0]).astype(ho_ref.dtype)


def _post(a, w, xf, gate, gn, scale, shift, S, h_dtype):
    T, D = xf.shape
    tm = TOKEN_TILE
    nps = S // tm
    row = lambda i: (i, 0)
    bat = lambda i: (i // nps, 0, 0)
    cst2 = lambda i: (0, 0)
    return pl.pallas_call(
        _post_kernel, grid=(T // tm,),
        in_specs=[pl.BlockSpec((tm, a.shape[1]), row), pl.BlockSpec(w.shape, cst2),
                  pl.BlockSpec((tm, D), row), pl.BlockSpec((1, 1, D), bat),
                  pl.BlockSpec((1, D), cst2), pl.BlockSpec((1, 1, D), bat), pl.BlockSpec((1, 1, D), bat)],
        out_specs=[pl.BlockSpec((tm, D), row), pl.BlockSpec((tm, D), row)],
        out_shape=[jax.ShapeDtypeStruct((T, D), F32), jax.ShapeDtypeStruct((T, D), h_dtype)],
        name="post")(a, w, xf, gate, gn, scale, shift)


def _ffn_kernel(h_ref, x_ref, gate_ref, w1_ref, w3_ref, w2_ref, xo_ref, acc_ref):
    j = pl.program_id(1)

    @pl.when(j == 0)
    def _():
        acc_ref[...] = jnp.zeros_like(acc_ref)
    hb = h_ref[...]
    a = _dot(hb, w1_ref[...])
    b = _dot(hb, w3_ref[...])
    acc_ref[...] += _dot((a * _sigmoid(a) * b).astype(BF16), w2_ref[...])

    @pl.when(j == pl.num_programs(1) - 1)
    def _():
        xo_ref[...] = x_ref[...] + gate_ref[0] * acc_ref[...]


def _ffn(h, xf, gate, w1, w3, w2, S, tn):
    T, D = xf.shape
    F = w1.shape[1]
    tm = TOKEN_TILE
    nps = S // tm
    return pl.pallas_call(
        _ffn_kernel, grid=(T // tm, F // tn),
        in_specs=[pl.BlockSpec((tm, D), lambda i, j: (i, 0)),
                  pl.BlockSpec((tm, D), lambda i, j: (i, 0)),
                  pl.BlockSpec((1, 1, D), lambda i, j: (i // nps, 0, 0)),
                  pl.BlockSpec((D, tn), lambda i, j: (0, j)),
                  pl.BlockSpec((D, tn), lambda i, j: (0, j)),
                  pl.BlockSpec((tn, D), lambda i, j: (j, 0))],
        out_specs=pl.BlockSpec((tm, D), lambda i, j: (i, 0)),
        out_shape=jax.ShapeDtypeStruct((T, D), F32),
        scratch_shapes=[pltpu.VMEM((tm, D), F32)],
        name="ffn")(h, xf, gate, w1, w3, w2)


def _nsa_proj_kernel(x_ref, sh_ref, sc_ref, gn_ref, wq_ref, wkv_ref, wg_ref, rc_ref, ra_ref, rb_ref,
                     q_ref, kc_ref, vc_ref, ks_ref, vs_ref, kw_ref, vw_ref, g_ref):
    hb = _modulate(x_ref[...], gn_ref[...], sc_ref[0], sh_ref[0]).astype(BF16)
    rc, ra, rb = rc_ref[...], ra_ref[...], rb_ref[...]

    def roped(v):
        return jnp.concatenate([_rope(v[:, :128], rc, ra, rb), _rope(v[:, 128:], rc, ra, rb)], axis=1)

    for p in range(NSA_HEADS // 2):
        qv = (roped(_dot(hb, wq_ref[:, p * 256:(p + 1) * 256])) * ATT_SCALE).astype(BF16)
        q_ref[2 * p] = qv[:, :128]
        q_ref[2 * p + 1] = qv[:, 128:]
    outs = (kc_ref, vc_ref, ks_ref, vs_ref, kw_ref, vw_ref)
    for n, o_ref in enumerate(outs):
        v = _dot(hb, wkv_ref[:, n * 256:(n + 1) * 256])
        o_ref[...] = (roped(v) if n % 2 == 0 else v).astype(BF16)
    g_ref[...] = _sigmoid(_dot(hb, wg_ref[...]))


def _nsa_proj(xf, shift, scale, gn, wq, wkv, wg, rope, S):
    T, D = xf.shape
    tm = TOKEN_TILE
    nps = S // tm
    row = lambda i: (i, 0)
    bat = lambda i: (i // nps, 0, 0)
    pos = lambda i: (i % nps, 0)
    cst2 = lambda i: (0, 0)
    kv_spec = pl.BlockSpec((tm, 256), row)
    kv_shape = jax.ShapeDtypeStruct((T, 256), BF16)
    return pl.pallas_call(
        _nsa_proj_kernel, grid=(T // tm,),
        in_specs=[pl.BlockSpec((tm, D), row),
                  pl.BlockSpec((1, 1, D), bat), pl.BlockSpec((1, 1, D), bat),
                  pl.BlockSpec((1, D), cst2),
                  pl.BlockSpec(wq.shape, cst2), pl.BlockSpec(wkv.shape, cst2), pl.BlockSpec(wg.shape, cst2),
                  pl.BlockSpec((tm, LANES), pos), pl.BlockSpec((tm, LANES), pos),
                  pl.BlockSpec((tm, LANES), pos)],
        out_specs=[pl.BlockSpec((NSA_HEADS, tm, LANES), lambda i: (0, i, 0))] + [kv_spec] * 6
                  + [pl.BlockSpec((tm, LANES), row)],
        out_shape=[jax.ShapeDtypeStruct((NSA_HEADS, T, LANES), BF16)] + [kv_shape] * 6
                  + [jax.ShapeDtypeStruct((T, LANES), F32)],
        name="nsa_proj")(xf, shift, scale, gn, wq, wkv, wg, *rope)


def _compress_kernel(uk_ref, uv_ref, pe_ref, k1_ref, k2_ref, v1_ref, v2_ref, ko_ref, vo_ref):
    half = CMP_STRIDE * HEAD_DIM
    pe = jnp.broadcast_to(pe_ref[...], (8, 2 * half)).astype(BF16)

    def comp(u_ref, w1_ref, w2_ref):
        bias = _dot(pe, w1_ref[...])[0:1]
        out = None
        for gg in range(2):
            u = u_ref[0, gg]
            a = _dot(u, w1_ref[:half, :])
            b = _dot(u, w1_ref[half:, :])
            nrow = b.shape[0]
            hid = a + pltpu.roll(b, nrow - 1, 0) + bias
            hid = (hid * _sigmoid(hid)).astype(BF16)
            o = _dot(hid, w2_ref[gg])
            out = o if out is None else out + o
        return out
    ko_ref[0] = comp(uk_ref, k1_ref, k2_ref).astype(BF16)
    vo_ref[0] = comp(uv_ref, v1_ref, v2_ref).astype(BF16)


def _compress(uk, uv, pe, k1, k2, v1, v2):
    B, G, nch, W = uk.shape
    u_spec = pl.BlockSpec((1, 2, nch, W), lambda b, p: (b, p, 0, 0))
    w1_spec = pl.BlockSpec(k1.shape, lambda b, p: (0, 0))
    w2_spec = pl.BlockSpec(k2.shape, lambda b, p: (0, 0, 0))
    o_spec = pl.BlockSpec((1, nch, LANES), lambda b, p: (b, 0, p))
    o_shape = jax.ShapeDtypeStruct((B, nch, G * HEAD_DIM), BF16)
    return pl.pallas_call(
        _compress_kernel, grid=(B, G // 2),
        in_specs=[u_spec, u_spec, pl.BlockSpec(pe.shape, lambda b, p: (0, 0)),
                  w1_spec, w2_spec, w1_spec, w2_spec],
        out_specs=[o_spec, o_spec], out_shape=[o_shape, o_shape],
        name="compress")(uk, uv, pe, k1, k2, v1, v2)


def _nsa_attn_kernel(q_ref, g_ref, kcmp_ref, vcmp_ref, ks_ref, vs_ref, kw_ref, vw_ref,
                     covt_ref, exp_ref, o_ref, oc_ref, psum_ref, imp_ref, sel_ref, m_ref, l_ref, acc_ref,
                     osel_ref, *, n_sel):
    QB, KC, HPG = Q_BLOCK, NSA_KC, NSA_HPG
    GR = HPG * QB
    qi = pl.program_id(1)
    qs = qi * QB
    row_t = qs + lax.broadcasted_iota(jnp.int32, (QB, 1), 0)
    nch = kcmp_ref.shape[1]
    nb = covt_ref.shape[0]
    n_ch = qs // KC + 1
    gates = g_ref[...]
    pair = lambda g: slice((g // 2) * 128, (g // 2) * 128 + 128)
    rows = lambda g: slice(g * GR, (g + 1) * GR)
    q_grp = lambda g: q_ref[g * HPG:(g + 1) * HPG].reshape(GR, LANES)

    cmp_end = lax.broadcasted_iota(jnp.int32, (QB, nch), 1) * CMP_STRIDE + (CMP_LEN - 1)
    cbias = _tile_rows(jnp.where(cmp_end <= row_t, 0.0, -jnp.inf), HPG)
    for g in range(NSA_GROUPS):
        s = _dot_nt(q_grp(g), kcmp_ref[0, :, pair(g)]) + cbias
        m = jnp.max(s, axis=1, keepdims=True)
        m = jnp.where(m == -jnp.inf, 0.0, m)
        e = jnp.exp2(s - m)
        p = e * (1.0 / jnp.maximum(jnp.sum(e, axis=1, keepdims=True), 1e-30))
        oc_ref[rows(g), :] = _dot(p.astype(BF16), vcmp_ref[0, :, pair(g)])
        psum_ref[g] = p[0:QB] + p[QB:2 * QB] + p[2 * QB:3 * QB] + p[3 * QB:4 * QB]

    jb = lax.broadcasted_iota(jnp.int32, (nb, QB), 0)
    cur = (qs + lax.broadcasted_iota(jnp.int32, (nb, QB), 1)) // SEL_LEN
    forced = (jb == 0) | (jb == cur) | (jb == cur - 1)
    ri = lax.broadcasted_iota(jnp.int32, (QB, QB), 0)
    ci = lax.broadcasted_iota(jnp.int32, (QB, QB), 1)
    eye = jnp.where(ri == ci, 1.0, 0.0).astype(BF16)
    imps = []
    for g in range(NSA_GROUPS):
        imp = lax.dot_general(covt_ref[...], psum_ref[g], (((1,), (1,)), ((), ())),
                              precision=HIGHEST, preferred_element_type=F32)
        imp = jnp.where(forced, FORCE_SCORE, imp)
        imp = jnp.where(jb <= cur, imp, -jnp.inf)
        imp_ref[g] = imp
        imps.append(imp)

    def rank_body(i, ranks):
        before = jnp.where(i < jb, 1.0, 0.0)
        out = []
        for g in range(NSA_GROUPS):
            ri_ = imp_ref[g, pl.ds(i, 1), :]
            out.append(ranks[g] + jnp.where(ri_ > imps[g], 1.0, 0.0) + jnp.where(ri_ == imps[g], before, 0.0))
        return tuple(out)
    n_live = jnp.minimum((qs + QB - 1) // SEL_LEN + 1, nb)
    ranks = lax.fori_loop(0, n_live, rank_body, (jnp.zeros((nb, QB), F32),) * NSA_GROUPS)
    for g in range(NSA_GROUPS):
        selt = jnp.where(ranks[g] < n_sel, 1.0, 0.0).astype(BF16)
        sel_ref[g] = _dot_nt(eye, selt).astype(BF16)

    _flash_init(m_ref, l_ref, acc_ref)

    def sel_chunk(c, carry):
        k0 = pl.multiple_of(c * KC, KC)
        causal = (k0 + lax.broadcasted_iota(jnp.int32, (QB, KC), 1)) <= row_t
        for g in range(NSA_GROUPS):
            keep = (_dot(sel_ref[g], exp_ref[c]) > 0.5) & causal
            bias = _tile_rows(jnp.where(keep, 0.0, -jnp.inf), HPG)
            s = _dot_nt(q_grp(g), ks_ref[0, pl.ds(k0, KC), pair(g)]) + bias
            _flash_update(s, vs_ref[0, pl.ds(k0, KC), pair(g)], m_ref, l_ref, acc_ref, g * GR)
        return carry
    _paired_loop(n_ch, sel_chunk)
    osel_ref[...] = acc_ref[...] / jnp.maximum(l_ref[...], 1e-30)

    wl = WINDOW + QB
    w0 = pl.multiple_of(jnp.maximum(qs - WINDOW, 0), QB)
    wpos = w0 + lax.broadcasted_iota(jnp.int32, (QB, wl), 1)
    wbias = _tile_rows(jnp.where((wpos <= row_t) & (wpos > row_t - WINDOW), 0.0, -jnp.inf), HPG)
    for g in range(NSA_GROUPS):
        s = _dot_nt(q_grp(g), kw_ref[0, pl.ds(w0, wl), pair(g)]) + wbias
        m = jnp.max(s, axis=1, keepdims=True)
        m = jnp.where(m == -jnp.inf, 0.0, m)
        e = jnp.exp2(s - m)
        den = jnp.maximum(jnp.sum(e, axis=1, keepdims=True), 1e-30)
        acc_ref[rows(g), :] = _dot(e.astype(BF16), vw_ref[0, pl.ds(w0, wl), pair(g)]) / den

    lane = lax.broadcasted_iota(jnp.int32, (QB, LANES), 1)
    pair_out = [None, None]
    for h in range(NSA_HEADS):
        g = h // HPG
        hr = slice(h * QB, (h + 1) * QB)
        o = (gates[:, 3 * h:3 * h + 1] * oc_ref[hr, :] + gates[:, 3 * h + 1:3 * h + 2] * osel_ref[hr, :]
             + gates[:, 3 * h + 2:3 * h + 3] * acc_ref[hr, :])
        pair_out[h % 2] = o
        if h % 2 == 1:
            if g % 2 == 0:
                both = jnp.where(lane < 64, pair_out[0], pltpu.roll(pair_out[1], 64, 1))
            else:
                both = jnp.where(lane < 64, pltpu.roll(pair_out[0], 64, 1), pair_out[1])
            o_ref[:, (h // 2) * 128:(h // 2) * 128 + 128] = both.astype(BF16)


def _nsa_attn(q, gates, kcmp, vcmp, ks, vs, kw, vw, covt, expand, B, S):
    T = B * S
    nq = S // Q_BLOCK
    nch = kcmp.shape[1]
    nb = S // SEL_LEN
    row = lambda b, i: (b * nq + i, 0)
    bat = lambda b, i: (b, 0, 0)
    kv = lambda a: a.reshape(B, S, 256)
    kv_spec = pl.BlockSpec((1, S, 256), bat)
    cmp_spec = pl.BlockSpec((1, nch, 256), bat)
    kern = functools.partial(_nsa_attn_kernel, n_sel=min(SEL_BLOCKS, nb))
    return pl.pallas_call(
        kern, grid=(B, nq),
        in_specs=[pl.BlockSpec((NSA_HEADS, Q_BLOCK, LANES), lambda b, i: (0, b * nq + i, 0)),
                  pl.BlockSpec((Q_BLOCK, LANES), row),
                  cmp_spec, cmp_spec, kv_spec, kv_spec, kv_spec, kv_spec,
                  pl.BlockSpec(covt.shape, lambda b, i: (0, 0)),
                  pl.BlockSpec(expand.shape, lambda b, i: (0, 0, 0))],
        out_specs=pl.BlockSpec((Q_BLOCK, NSA_HEADS * HEAD_DIM), row),
        out_shape=jax.ShapeDtypeStruct((T, NSA_HEADS * HEAD_DIM), BF16),
        scratch_shapes=[pltpu.VMEM((NSA_HEADS * Q_BLOCK, LANES), F32),
                        pltpu.VMEM((NSA_GROUPS, Q_BLOCK, nch), F32),
                        pltpu.VMEM((NSA_GROUPS, nb, Q_BLOCK), F32),
                        pltpu.VMEM((NSA_GROUPS, Q_BLOCK, nb), BF16),
                        pltpu.VMEM((NSA_HEADS * Q_BLOCK, LANES), F32),
                        pltpu.VMEM((NSA_HEADS * Q_BLOCK, LANES), F32),
                        pltpu.VMEM((NSA_HEADS * Q_BLOCK, LANES), F32),
                        pltpu.VMEM((NSA_HEADS * Q_BLOCK, LANES), F32)],
        name="nsa_attn")(q, gates, kcmp, vcmp, kv(ks), kv(vs), kv(kw), kv(vw), covt, expand)


def _router_kernel(h_ref, wr_ref, route_ref, cnt_ref, carry_ref):
    i = pl.program_id(0)
    tm = h_ref.shape[0]

    @pl.when(i == 0)
    def _():
        carry_ref[...] = jnp.zeros_like(carry_ref)
    lane = lax.broadcasted_iota(jnp.int32, (tm, LANES), 1).astype(F32)
    lg = jnp.dot(h_ref[...], wr_ref[...], precision=HIGHEST, preferred_element_type=F32)
    lg = jnp.where(lane < N_EXPERTS, lg, -jnp.inf)
    v1 = jnp.max(lg, axis=1, keepdims=True)
    i1 = jnp.min(jnp.where(lg == v1, lane, float(LANES)), axis=1, keepdims=True)
    lg2 = jnp.where(lane == i1, -jnp.inf, lg)
    v2 = jnp.max(lg2, axis=1, keepdims=True)
    i2 = jnp.min(jnp.where(lg2 == v2, lane, float(LANES)), axis=1, keepdims=True)
    e2 = jnp.exp(v2 - v1)
    g1 = 1.0 / (1.0 + e2)
    g2 = e2 / (1.0 + e2)
    oh1 = jnp.where(lane == i1, 1.0, 0.0)
    oh2 = jnp.where(lane == i2, 1.0, 0.0)
    both = oh1 + oh2
    ri = lax.broadcasted_iota(jnp.int32, (tm, tm), 0)
    ci = lax.broadcasted_iota(jnp.int32, (tm, tm), 1)
    lower = jnp.where(ri > ci, 1.0, 0.0).astype(BF16)
    tot = carry_ref[0:1, :] + _dot(lower, both.astype(BF16))
    r1 = jnp.sum(oh1 * tot, axis=1, keepdims=True)
    r2 = jnp.sum(oh2 * tot, axis=1, keepdims=True)
    new_carry = carry_ref[...] + jnp.sum(both, axis=0, keepdims=True)
    carry_ref[...] = new_carry
    cnt_ref[...] = new_carry
    out = jnp.zeros((tm, LANES), F32)
    for col, val in enumerate((i1, i2, g1, g2, r1, r2)):
        out = jnp.where(lane == col, val, out)
    route_ref[...] = out


def _router(h, wr):
    T, D = h.shape
    tm = TOKEN_TILE
    return pl.pallas_call(
        _router_kernel, grid=(T // tm,),
        in_specs=[pl.BlockSpec((tm, D), lambda i: (i, 0)), pl.BlockSpec(wr.shape, lambda i: (0, 0))],
        out_specs=[pl.BlockSpec((tm, LANES), lambda i: (i, 0)), pl.BlockSpec((8, LANES), lambda i: (0, 0))],
        out_shape=[jax.ShapeDtypeStruct((T, LANES), F32), jax.ShapeDtypeStruct((8, LANES), F32)],
        scratch_shapes=[pltpu.VMEM((8, LANES), F32)],
        name="router")(h, wr)


def _row_copy(src, dst, sem):
    return pltpu.make_async_copy(src, dst, sem)


def _moe_ffn_kernel(be_ref, nu_ref, tab_ref, h_ref, w1_ref, w3_ref, w2_ref, y_ref,
                    xin_ref, xb_ref, acc_ref, yout_ref, sem_in, sem_out, *, nj):
    i = pl.program_id(0)
    j = pl.program_id(1)
    nblk = tab_ref.shape[0] // MOE_BLOCK
    issue_steps = 4
    per_step = MOE_BLOCK // issue_steps
    assert nj > issue_steps and per_step * issue_steps == MOE_BLOCK
    used = i < nu_ref[0]
    slot = i % 2
    nxt = jnp.minimum(i + 1, nblk - 1)
    prv = jnp.maximum(i - 1, 0)

    def gather(entry, r, buf):
        src = tab_ref[entry] & ((1 << SRC_BITS) - 1)
        return _row_copy(h_ref.at[pl.ds(src, 1), :], xin_ref.at[buf, pl.ds(r, 1), :], sem_in.at[buf])

    def scatter(entry, r):
        dst = lax.shift_right_logical(tab_ref[entry], SRC_BITS)
        return _row_copy(yout_ref.at[pl.ds(r, 1), :], y_ref.at[pl.ds(dst, 1), :], sem_out)

    def issue(r0, n):
        r0 = pl.multiple_of(r0, per_step)
        e_nxt = nxt * MOE_BLOCK + r0
        e_prv = prv * MOE_BLOCK + r0
        for u in range(n):
            gather(e_nxt + u, r0 + u, 1 - slot).start(priority=1)
            scatter(e_prv + u, r0 + u).start(priority=1)

    def wait_gathered(buf):
        _row_copy(h_ref.at[pl.ds(0, MOE_BLOCK), :], xin_ref.at[buf], sem_in.at[buf]).wait()

    def wait_scattered():
        _row_copy(yout_ref, y_ref.at[pl.ds(0, MOE_BLOCK), :], sem_out).wait()

    @pl.when((i == 0) & (j == 0))
    def _():
        yout_ref[...] = jnp.zeros_like(yout_ref)

        def first(r, carry):
            gather(r, r, 0).start()
            return carry
        lax.fori_loop(0, MOE_BLOCK, first, 0)

    @pl.when(j == 0)
    def _():
        wait_gathered(slot)
        xb_ref[...] = xin_ref[slot].astype(BF16)
        acc_ref[...] = jnp.zeros_like(acc_ref)

    def expert_step():
        xb = xb_ref[...]
        a = _dot(xb, w1_ref[0])
        b = _dot(xb, w3_ref[0])
        acc_ref[...] += _dot((a * _sigmoid(a) * b).astype(BF16), w2_ref[0])

    issuing = j < issue_steps

    @pl.when(used & issuing)
    def _():
        issue(j * per_step, per_step)
        expert_step()

    @pl.when(used & jnp.logical_not(issuing))
    def _():
        expert_step()

    @pl.when(jnp.logical_not(used) & issuing)
    def _():
        issue(j * per_step, per_step)

    @pl.when(j == nj - 1)
    def _():
        wait_scattered()
        yout_ref[...] = acc_ref[...]

    @pl.when((i == pl.num_programs(0) - 1) & (j == nj - 1))
    def _():
        wait_gathered(1 - slot)


def _moe_ffn(block_e, n_used, slot_tab, h, w1, w3, w2, tn):
    T, D = h.shape
    nblk = slot_tab.shape[0] // MOE_BLOCK
    E, _, F = w1.shape
    nj = F // tn
    bi = lambda i: jnp.minimum(i, nblk - 1)
    jj = lambda i, j, nu: jnp.where(i < nu[0], j, nj - 1)
    return pl.pallas_call(
        functools.partial(_moe_ffn_kernel, nj=nj),
        grid_spec=pltpu.PrefetchScalarGridSpec(
            num_scalar_prefetch=3, grid=(nblk + 1, nj),
            in_specs=[pl.BlockSpec(memory_space=pl.ANY),
                      pl.BlockSpec((1, D, tn), lambda i, j, be, nu, tb: (be[bi(i)], 0, jj(i, j, nu))),
                      pl.BlockSpec((1, D, tn), lambda i, j, be, nu, tb: (be[bi(i)], 0, jj(i, j, nu))),
                      pl.BlockSpec((1, tn, D), lambda i, j, be, nu, tb: (be[bi(i)], jj(i, j, nu), 0))],
            out_specs=pl.BlockSpec(memory_space=pl.ANY),
            scratch_shapes=[pltpu.VMEM((2, MOE_BLOCK, D), F32), pltpu.VMEM((MOE_BLOCK, D), BF16),
                            pltpu.VMEM((MOE_BLOCK, D), F32), pltpu.VMEM((MOE_BLOCK, D), F32),
                            pltpu.SemaphoreType.DMA((2,)), pltpu.SemaphoreType.DMA(())]),
        out_shape=jax.ShapeDtypeStruct((2 * T + MOE_BLOCK, D), F32),
        name="moe_ffn")(block_e, n_used, slot_tab, h, w1, w3, w2)


def _moe_combine_kernel(y0_ref, y1_ref, x_ref, gate_ref, route_ref, fn_ref, o_ref):
    route = route_ref[...]
    y = route[:, 2:3] * y0_ref[...] + route[:, 3:4] * y1_ref[...]
    o_ref[...] = _rms(x_ref[...] + gate_ref[0] * y, fn_ref[...])


def _moe_combine(y, xf, gate, route, fn, S):
    T, D = xf.shape
    tm = TOKEN_TILE
    nps = S // tm
    nt = T // tm
    return pl.pallas_call(
        _moe_combine_kernel, grid=(nt,),
        in_specs=[pl.BlockSpec((tm, D), lambda i: (i, 0)),
                  pl.BlockSpec((tm, D), lambda i: (nt + i, 0)),
                  pl.BlockSpec((tm, D), lambda i: (i, 0)),
                  pl.BlockSpec((1, 1, D), lambda i: (i // nps, 0, 0)),
                  pl.BlockSpec((tm, LANES), lambda i: (i, 0)),
                  pl.BlockSpec((1, D), lambda i: (0, 0))],
        out_specs=pl.BlockSpec((tm, D), lambda i: (i, 0)),
        out_shape=jax.ShapeDtypeStruct((T, D), F32), name="moe_combine")(y, y, xf, gate, route, fn)


def _rope_tables(S):
    inv = ROPE_THETA ** (-jnp.arange(0, ROPE_DIM, 2, dtype=F32) / ROPE_DIM)
    ang = jnp.arange(S, dtype=F32)[:, None] * inv[None, :]
    cos, sin = jnp.cos(ang), jnp.sin(ang)
    pm = np.arange(LANES) % HEAD_DIM
    col = pm % (ROPE_DIM // 2)
    rc = jnp.where((pm < ROPE_DIM)[None, :], cos[:, col], 1.0)
    ra = jnp.where((pm < ROPE_DIM // 2)[None, :], -sin[:, col], 0.0)
    rb = jnp.where(((pm >= ROPE_DIM // 2) & (pm < ROPE_DIM))[None, :], sin[:, col], 0.0)
    return rc, ra, rb


def _dsa_weights(w_in, w_uk, w_iq):
    D = w_in.shape[0]
    a, b, c, d = DSA_Q_LORA, DSA_Q_LORA + DSA_KV_LORA, DSA_Q_LORA + DSA_KV_LORA + ROPE_DIM, \
        DSA_Q_LORA + DSA_KV_LORA + ROPE_DIM + IDX_DIM
    win = jnp.concatenate([w_in[:, :b], w_in[:, c:d], w_in[:, b:c], w_in[:, d:],
                           jnp.zeros((D, 512 - w_in.shape[1]), F32)], axis=1).astype(BF16)
    H = DSA_HEADS
    blk = jnp.zeros((H, HEAD_DIM, 256), F32)
    blk = blk.at[:, ROPE_DIM:, :DSA_KV_LORA].set(jnp.transpose(w_uk, (0, 2, 1)))
    blk = blk.at[:, :ROPE_DIM, 192:192 + ROPE_DIM].set(jnp.eye(ROPE_DIM, dtype=F32))
    z = jnp.zeros((H // 2, HEAD_DIM, 256), F32)
    wcat = jnp.concatenate([jnp.concatenate([blk[0::2], z], axis=2),
                            jnp.concatenate([z, blk[1::2]], axis=2)], axis=1).astype(BF16)
    wiq = w_iq.reshape(DSA_Q_LORA, IDX_HEADS, IDX_DIM)
    wiq = jnp.concatenate([wiq, jnp.zeros_like(wiq)], axis=2).reshape(DSA_Q_LORA, IDX_HEADS * 128)
    return win, wcat, wiq.astype(BF16)


def _nsa_weights(w_in):
    D = w_in.shape[0]
    nq = NSA_HEADS * HEAD_DIM
    wq = w_in[:, :nq].reshape(D, NSA_HEADS, HEAD_DIM)
    z = jnp.zeros_like(wq)
    odd = ((np.arange(NSA_HEADS) // NSA_HPG) % 2 == 1)[None, :, None]
    wq = jnp.concatenate([jnp.where(odd, z, wq), jnp.where(odd, wq, z)], axis=2).reshape(D, NSA_HEADS * 128)
    wkv = w_in[:, nq:nq + 6 * 256]
    wg = jnp.concatenate([w_in[:, nq + 6 * 256:], jnp.zeros((D, LANES - 3 * NSA_HEADS), F32)], axis=1)
    return wq.astype(BF16), wkv.astype(BF16), wg.astype(BF16)


def _nsa_tables(S):
    nch = S // CMP_STRIDE
    nc = (S - CMP_LEN) // CMP_STRIDE + 1
    nb = S // SEL_LEN
    cstart = np.arange(nch) * CMP_STRIDE
    bstart = np.arange(nb) * SEL_LEN
    cov = ((cstart[None, :] < bstart[:, None] + SEL_LEN) & (cstart[None, :] + CMP_LEN > bstart[:, None])
           & (np.arange(nch)[None, :] < nc)).astype(np.float32)
    kpos = np.arange(S).reshape(S // NSA_KC, 1, NSA_KC)
    expand = (kpos // SEL_LEN == np.arange(nb)[None, :, None]).astype(np.float32)
    return jnp.asarray(cov), jnp.asarray(expand, dtype=BF16)


def _slot_tables(dest, T, n_blocks):
    ns = n_blocks * MOE_BLOCK
    asg = jnp.full((ns,), -1, jnp.int32).at[dest].set(jnp.arange(2 * T, dtype=jnp.int32))
    real = asg >= 0
    src = jnp.where(real, asg // 2, 0)
    spare = 2 * T + jnp.arange(ns, dtype=jnp.int32) % MOE_BLOCK
    dst = jnp.where(real, (asg % 2) * T + asg // 2, spare)
    assert T <= 1 << SRC_BITS and 2 * T + MOE_BLOCK <= 1 << (32 - SRC_BITS)
    return src | lax.shift_left(dst, SRC_BITS)


def _chunk_tokens(a, B, S):
    a = a.reshape(B, S // CMP_STRIDE, CMP_STRIDE, NSA_GROUPS, HEAD_DIM)
    return jnp.transpose(a, (0, 3, 1, 2, 4)).reshape(B, NSA_GROUPS, S // CMP_STRIDE, CMP_STRIDE * HEAD_DIM)


def kernel(x, c, norm_mix, norm_ffn, ada_w, ada_b, final_norm, dsa_w_in, dsa_g_q, dsa_w_uq, dsa_g_kv,
           dsa_w_uk, dsa_w_uv, dsa_w_iq, dsa_w_o, ffn_w1, ffn_w3, ffn_w2, nsa_w_in, nsa_cmp_pe,
           nsa_cmp_k1, nsa_cmp_k2, nsa_cmp_v1, nsa_cmp_v2, nsa_w_o, moe_router, moe_w1, moe_w3, moe_w2):
    B, S, D = x.shape
    T = B * S
    xf = x.reshape(T, D)
    mods = _ada(c, ada_w, ada_b).reshape(4, B, 3, 1, D)
    shift = lambda s: mods[s, :, 0]
    scale = lambda s: mods[s, :, 1]
    gate = lambda s: mods[s, :, 2]
    rope = _rope_tables(S)

    win, wcat, wiq = _dsa_weights(dsa_w_in[0], dsa_w_uk[0], dsa_w_iq[0])
    qcat, iq, kcat, ik, iw = _dsa_proj(
        xf, shift(0), scale(0), norm_mix[0:1], win, dsa_g_q[0:1], dsa_g_kv[0:1],
        dsa_w_uq[0].astype(BF16), wiq, wcat, rope, S)
    olat = _dsa_attn(qcat, iq, iw, kcat, ik, B, S)
    wuvo = _wuvo(dsa_w_uv[0], dsa_w_o[0])
    x1, h1 = _post(olat, wuvo, xf, gate(0), norm_ffn[0:1], scale(1), shift(1), S, BF16)
    x2 = _ffn(h1, x1, gate(1), ffn_w1[0].astype(BF16), ffn_w3[0].astype(BF16), ffn_w2[0].astype(BF16),
              S, ffn_w1.shape[2] // 2)

    wq, wkv, wg = _nsa_weights(nsa_w_in[0])
    q, kc, vc, ks, vs, kw, vw, gates = _nsa_proj(xf=x2, shift=shift(2), scale=scale(2), gn=norm_mix[1:2],
                                                 wq=wq, wkv=wkv, wg=wg, rope=rope, S=S)
    zpad = jnp.zeros((CMP_HIDDEN, HEAD_DIM), F32)
    pad2 = lambda w2: jnp.stack([jnp.concatenate([w2, zpad], axis=1),
                                 jnp.concatenate([zpad, w2], axis=1)]).astype(BF16)
    kcmp, vcmp = _compress(_chunk_tokens(kc, B, S), _chunk_tokens(vc, B, S),
                           nsa_cmp_pe[0].reshape(1, CMP_LEN * HEAD_DIM),
                           nsa_cmp_k1[0].astype(BF16), pad2(nsa_cmp_k2[0]),
                           nsa_cmp_v1[0].astype(BF16), pad2(nsa_cmp_v2[0]))
    covt, expand = _nsa_tables(S)
    o = _nsa_attn(q, gates, kcmp, vcmp, ks, vs, kw, vw, covt, expand, B, S)
    x3, h3 = _post(o, nsa_w_o[0].astype(BF16), x2, gate(2), norm_ffn[1:2], scale(3), shift(3), S, F32)

    wr = jnp.concatenate([moe_router[0], jnp.zeros((D, LANES - N_EXPERTS), F32)], axis=1)
    route, cnt = _router(h3, wr)
    counts = cnt[0, :N_EXPERTS].astype(jnp.int32)
    padded = (counts + MOE_BLOCK - 1) // MOE_BLOCK * MOE_BLOCK
    ends = jnp.cumsum(padded)
    pstart = ends - padded
    eidx = route[:, 0:2].astype(jnp.int32)
    dest = (pstart[eidx] + route[:, 4:6].astype(jnp.int32)).reshape(-1)
    n_blocks = -(-(T * 2) // MOE_BLOCK) + N_EXPERTS
    block_start = jnp.arange(n_blocks, dtype=jnp.int32) * MOE_BLOCK
    block_e = jnp.minimum(jnp.sum((ends[None, :] <= block_start[:, None]).astype(jnp.int32), axis=1),
                          N_EXPERTS - 1)
    n_used = (ends[-1:] // MOE_BLOCK).astype(jnp.int32)
    y = _moe_ffn(block_e, n_used, _slot_tables(dest, T, n_blocks), h3, moe_w1[0].astype(BF16), moe_w3[0].astype(BF16),
                 moe_w2[0].astype(BF16), 512)
    out = _moe_combine(y, x3, gate(3), route, final_norm.reshape(1, D), S)
    return out.reshape(B, S, D)
```

```python
import functools

import numpy as np
import jax
import jax.numpy as jnp
from jax import lax
from jax.experimental import pallas as pl
from jax.experimental.pallas import tpu as pltpu

F32 = jnp.float32
BF16 = jnp.bfloat16
HIGHEST = lax.Precision.HIGHEST
INT_MIN = -2147483648

HEAD_DIM = 64
ROPE_DIM = 16
ROPE_THETA = 500000.0
Q_BLOCK = 128
NORM_EPS = 1e-6

DSA_HEADS = 16
DSA_NOPE = 48
DSA_Q_LORA = 256
DSA_KV_LORA = 128
IDX_HEADS = 8
IDX_DIM = 64
DSA_TOPK = 256
DSA_KC = 512

NSA_HEADS = 16
NSA_GROUPS = 4
NSA_HPG = 4
CMP_LEN = 32
CMP_STRIDE = 16
CMP_HIDDEN = 256
SEL_LEN = 64
SEL_BLOCKS = 16
WINDOW = 512
FORCE_SCORE = 1e4
NSA_KC = 512

N_EXPERTS = 8
MOE_BLOCK = 512
SRC_BITS = 15
RPT = 8
TOKEN_TILE = 512
ROW_TILE = 256
LANES = 128

LOG2E = 1.4426950408889634
ATT_SCALE = HEAD_DIM ** -0.5 * LOG2E


def _dot(a, b):
    return jnp.dot(a, b, preferred_element_type=F32)


def _dot_nt(a, b):
    return lax.dot_general(a, b, (((1,), (1,)), ((), ())), preferred_element_type=F32)


def _sigmoid(v):
    return 1.0 / (1.0 + jnp.exp(-v))


def _rms(v, g):
    return v * lax.rsqrt(jnp.mean(v * v, axis=-1, keepdims=True) + NORM_EPS) * g


def _modulate(v, g, scale, shift):
    return _rms(v, g) * (1.0 + scale) + shift


def _rope(v, c, sa, sb):
    return v * c + pltpu.roll(v, LANES - 8, 1) * sa + pltpu.roll(v, 8, 1) * sb


def _tile_rows(a, n):
    return jnp.concatenate([a] * n, axis=0) if n > 1 else a


def _tile_lanes(a, n):
    return jnp.concatenate([a] * n, axis=1) if n > 1 else a


def _paired_loop(n, body):
    def two(i, carry):
        return body(2 * i + 1, body(2 * i, carry))
    lax.fori_loop(0, n // 2, two, 0)

    @pl.when(n % 2 == 1)
    def _():
        body(n - 1, 0)


def _flash_init(m_ref, l_ref, acc_ref):
    m_ref[...] = jnp.full(m_ref.shape, -1e30, F32)
    l_ref[...] = jnp.zeros(l_ref.shape, F32)
    acc_ref[...] = jnp.zeros(acc_ref.shape, F32)


def _flash_update(s, v, m_ref, l_ref, acc_ref, row0):
    n = s.shape[0]
    ps = []
    for r in range(0, n, Q_BLOCK):
        rs = slice(row0 + r, row0 + r + Q_BLOCK)
        sl = s[r:r + Q_BLOCK]
        m_old = m_ref[rs, :]
        m_new = jnp.maximum(m_old, jnp.max(sl, axis=1, keepdims=True))
        p = jnp.exp2(sl - _tile_lanes(m_new, sl.shape[1] // LANES))
        alpha = jnp.exp2(m_old - m_new)
        l_ref[rs, :] = alpha * l_ref[rs, :] + jnp.sum(p, axis=1, keepdims=True)
        acc_ref[rs, :] = alpha * acc_ref[rs, :]
        m_ref[rs, :] = m_new
        ps.append(p.astype(BF16))
    pv = _dot(jnp.concatenate(ps, axis=0), v)
    acc_ref[row0:row0 + n, :] += pv


def _ada_kernel(c_ref, w_ref, b_ref, o_ref):
    cv = c_ref[...]
    sc = cv * _sigmoid(cv)
    o_ref[0] = jnp.dot(sc, w_ref[0], precision=HIGHEST, preferred_element_type=F32) + b_ref[0]


def _ada(c, ada_w, ada_b):
    B, D = c.shape
    w = ada_w.reshape(4, D, 3 * D)
    b = ada_b.reshape(4, 1, 3 * D)
    return pl.pallas_call(
        _ada_kernel, grid=(4, 3),
        in_specs=[pl.BlockSpec((B, D), lambda l, j: (0, 0)),
                  pl.BlockSpec((1, D, D), lambda l, j: (l, 0, j)),
                  pl.BlockSpec((1, 1, D), lambda l, j: (l, 0, j))],
        out_specs=pl.BlockSpec((1, B, D), lambda l, j: (l, 0, j)),
        out_shape=jax.ShapeDtypeStruct((4, B, 3 * D), F32), name="ada")(c, w, b)


def _dsa_proj_kernel(x_ref, sh_ref, sc_ref, gn_ref, win_ref, gq_ref, gkv_ref, wuq_ref, wiq_ref,
                     wcat_ref, rc_ref, ra_ref, rb_ref, qcat_ref, iq_ref, kcat_ref, ik_ref, iw_ref):
    h = _modulate(x_ref[...], gn_ref[...], sc_ref[0], sh_ref[0])
    proj = _dot(h.astype(BF16), win_ref[...])
    q_lat = _rms(proj[:, :256], gq_ref[...]).astype(BF16)
    c_kv = _rms(proj[:, 256:384], gkv_ref[...])
    rc, ra, rb = rc_ref[...], ra_ref[...], rb_ref[...]
    rest = _rope(proj[:, 384:512], rc, ra, rb)
    lane = lax.broadcasted_iota(jnp.int32, rest.shape, 1)
    kcat_ref[:, :128] = c_kv.astype(BF16)
    kcat_ref[:, 128:] = jnp.where((lane >= 64) & (lane < 80), rest, 0.0).astype(BF16)
    ik_ref[...] = jnp.where(lane < 64, rest, 0.0).astype(BF16)
    iw_ref[...] = rest * (IDX_HEADS ** -0.5)
    q = _dot(q_lat, wuq_ref[...])
    for p in range(DSA_HEADS // 2):
        qp = (_rope(q[:, p * 128:(p + 1) * 128], rc, ra, rb) * ATT_SCALE).astype(BF16)
        res = _dot(qp, wcat_ref[p]).astype(BF16)
        qcat_ref[2 * p] = res[:, :256]
        qcat_ref[2 * p + 1] = res[:, 256:]
    iqv = _dot(q_lat, wiq_ref[...])
    for hh in range(IDX_HEADS):
        iq_ref[hh] = (_rope(iqv[:, hh * 128:(hh + 1) * 128], rc, ra, rb) * (IDX_DIM ** -0.5)).astype(BF16)


def _dsa_proj(xf, shift, scale, gn, win, gq, gkv, wuq, wiq, wcat, rope, S):
    T, D = xf.shape
    tm = TOKEN_TILE
    nps = S // tm
    row = lambda i: (i, 0)
    bat = lambda i: (i // nps, 0, 0)
    pos = lambda i: (i % nps, 0)
    cst2 = lambda i: (0, 0)
    cst3 = lambda i: (0, 0, 0)
    return pl.pallas_call(
        _dsa_proj_kernel, grid=(T // tm,),
        in_specs=[pl.BlockSpec((tm, D), row),
                  pl.BlockSpec((1, 1, D), bat), pl.BlockSpec((1, 1, D), bat),
                  pl.BlockSpec((1, D), cst2),
                  pl.BlockSpec(win.shape, cst2),
                  pl.BlockSpec((1, DSA_Q_LORA), cst2), pl.BlockSpec((1, DSA_KV_LORA), cst2),
                  pl.BlockSpec(wuq.shape, cst2), pl.BlockSpec(wiq.shape, cst2),
                  pl.BlockSpec(wcat.shape, cst3),
                  pl.BlockSpec((tm, LANES), pos), pl.BlockSpec((tm, LANES), pos),
                  pl.BlockSpec((tm, LANES), pos)],
        out_specs=[pl.BlockSpec((DSA_HEADS, tm, 256), lambda i: (0, i, 0)),
                   pl.BlockSpec((IDX_HEADS, tm, LANES), lambda i: (0, i, 0)),
                   pl.BlockSpec((tm, 256), row),
                   pl.BlockSpec((tm, LANES), row),
                   pl.BlockSpec((tm, LANES), row)],
        out_shape=[jax.ShapeDtypeStruct((DSA_HEADS, T, 256), BF16),
                   jax.ShapeDtypeStruct((IDX_HEADS, T, LANES), BF16),
                   jax.ShapeDtypeStruct((T, 256), BF16),
                   jax.ShapeDtypeStruct((T, LANES), BF16),
                   jax.ShapeDtypeStruct((T, LANES), F32)],
        name="dsa_proj")(xf, shift, scale, gn, win, gq, gkv, wuq, wiq, wcat, *rope)


def _dsa_attn_kernel(q_ref, iq_ref, iw_ref, kcat_ref, ik_ref, o_ref, keys_ref, w_ref, m_ref, l_ref, acc_ref,
                     thr_ref, cnt_ref, *, k_sel, idx_bits):
    QB, KC, H = Q_BLOCK, DSA_KC, DSA_HEADS
    qi = pl.program_id(1)
    n_ch = (qi * QB) // KC + 1
    row_tl = qi * QB + lax.broadcasted_iota(jnp.int32, (QB, LANES), 0)
    iw = iw_ref[...]
    for hh in range(IDX_HEADS):
        w_ref[hh * QB:(hh + 1) * QB, :] = jnp.broadcast_to(iw[:, 80 + hh:81 + hh], (QB, KC))
    iq_all = iq_ref[...].reshape(IDX_HEADS * QB, LANES)

    def score_chunk(c, carry):
        k0 = pl.multiple_of(c * KC, KC)
        ikc = ik_ref[0, pl.ds(k0, KC), :]
        r = jnp.maximum(_dot_nt(iq_all, ikc), 0.0) * w_ref[...]
        sc = r[0:QB]
        for hh in range(1, IDX_HEADS):
            sc = sc + r[hh * QB:(hh + 1) * QB]
        sc = sc + 0.0
        bits = pltpu.bitcast(sc, jnp.int32)
        key = jnp.where(bits < 0, bits ^ 0x7FFFFFFF, bits)
        pos = k0 + lax.broadcasted_iota(jnp.int32, (QB, KC), 1)
        keys_ref[c] = jnp.where(pos <= _tile_lanes(row_tl, KC // LANES), key, INT_MIN)
        return carry
    _paired_loop(n_ch, score_chunk)

    lane_l = lax.broadcasted_iota(jnp.int32, (QB, LANES), 1)

    def count(pred):
        def body(c, a):
            for j in range(KC // LANES):
                kk = keys_ref[c, :, j * LANES:(j + 1) * LANES]
                a = a + jnp.where(pred(kk, c * KC + j * LANES + lane_l), 1.0, 0.0)
            return a
        a = lax.fori_loop(0, n_ch, body, jnp.zeros((QB, LANES), F32))
        return jnp.sum(a, axis=1, keepdims=True)

    def count3(c1, c2, c3):
        def body(c, a):
            for j in range(KC // LANES):
                kk = keys_ref[c, :, j * LANES:(j + 1) * LANES]
                a = a + jnp.where(kk >= c3, 4161, jnp.where(kk >= c2, 65, jnp.where(kk >= c1, 1, 0)))
            return a
        a = lax.fori_loop(0, n_ch, body, jnp.zeros((QB, LANES), jnp.int32))
        lane_sum = lambda x: jnp.sum(x.astype(F32), axis=1, keepdims=True)
        return lane_sum(a & 63), lane_sum((a >> 6) & 63), lane_sum(a >> 12)

    def bit_body(it, carry):
        thr, cnt = carry
        b1 = jnp.left_shift(jnp.int32(1), 31 - 2 * it)
        b0 = jnp.left_shift(jnp.int32(1), 30 - 2 * it)
        c1, c2 = thr ^ b0, thr ^ b1
        c3 = c2 ^ b0
        n1, n2, n3 = count3(c1, c2, c3)
        pick = lambda x3, x2, x1, x0: jnp.where(n3 >= k_sel, x3, jnp.where(n2 >= k_sel, x2,
                                                                           jnp.where(n1 >= k_sel, x1, x0)))
        return pick(c3, c2, c1, thr), pick(n3, n2, n1, cnt)

    def sweeps(first, n):
        thr, cnt = lax.fori_loop(first, first + n, bit_body, (thr_ref[...], cnt_ref[...]))
        thr_ref[...] = thr
        cnt_ref[...] = cnt
    thr_ref[...] = jnp.full((QB, LANES), INT_MIN, jnp.int32)
    cnt_ref[...] = jnp.full((QB, LANES), -1.0, F32)
    sweeps(0, 10)
    few = (row_tl + 1) < k_sel
    for stage in range(3):
        unsettled = jnp.where((cnt_ref[...] == k_sel) | few, 0.0, 1.0)

        @pl.when(jnp.max(unsettled) > 0.0)
        def _():
            sweeps(10 + 2 * stage, 2)
    thr = thr_ref[...]

    n_gt = count(lambda kk, pos: kk > thr)
    n_ge = count(lambda kk, pos: kk >= thr)
    tie = jnp.where((n_ge > k_sel) & (thr > INT_MIN), 1.0, 0.0)

    @pl.when(jnp.max(tie) > 0.0)
    def _():
        need = k_sel - n_gt
        def jbit(it, jcut):
            cand = jcut | jnp.left_shift(jnp.int32(1), idx_bits - 1 - it)
            f = count(lambda kk, pos: (kk == thr) & (pos < cand))
            return jnp.where(f <= need, cand, jcut)
        jcut = lax.fori_loop(0, idx_bits, jbit, jnp.zeros((QB, LANES), jnp.int32))
        def drop(c, carry):
            for j in range(KC // LANES):
                sl = slice(j * LANES, (j + 1) * LANES)
                kk = keys_ref[c, :, sl]
                pos = c * KC + j * LANES + lane_l
                keys_ref[c, :, sl] = jnp.where((kk == thr) & (pos >= jcut), INT_MIN, kk)
            return carry
        lax.fori_loop(0, n_ch, drop, 0)

    thr_eff = _tile_lanes(jnp.maximum(thr, INT_MIN + 1), KC // LANES)
    _flash_init(m_ref, l_ref, acc_ref)
    q_all = q_ref[...].reshape(H * QB, 256)

    def chunk(c, carry):
        k0 = pl.multiple_of(c * KC, KC)
        kc = kcat_ref[0, pl.ds(k0, KC), :]
        bias = jnp.where(keys_ref[c] >= thr_eff, 0.0, -jnp.inf)
        s = _dot_nt(q_all, kc) + _tile_rows(bias, H)
        _flash_update(s, kc[:, :DSA_KV_LORA], m_ref, l_ref, acc_ref, 0)
        return carry
    _paired_loop(n_ch, chunk)
    o = acc_ref[...] / jnp.maximum(l_ref[...], 1e-30)
    for h in range(H):
        o_ref[:, h * 128:(h + 1) * 128] = o[h * QB:(h + 1) * QB].astype(BF16)


def _dsa_attn(qcat, iq, iw, kcat, ik, B, S):
    T = B * S
    nq = S // Q_BLOCK
    row = lambda b, q: (b * nq + q, 0)
    bat = lambda b, q: (b, 0, 0)
    k_sel = min(DSA_TOPK, S // 4)
    assert S // LANES < 64, "per-lane key counts are packed 6 bits each"
    kern = functools.partial(_dsa_attn_kernel, k_sel=k_sel, idx_bits=int(S).bit_length())
    return pl.pallas_call(
        kern, grid=(B, nq),
        in_specs=[pl.BlockSpec((DSA_HEADS, Q_BLOCK, 256), lambda b, q: (0, b * nq + q, 0)),
                  pl.BlockSpec((IDX_HEADS, Q_BLOCK, LANES), lambda b, q: (0, b * nq + q, 0)),
                  pl.BlockSpec((Q_BLOCK, LANES), row),
                  pl.BlockSpec((1, S, 256), bat),
                  pl.BlockSpec((1, S, LANES), bat)],
        out_specs=pl.BlockSpec((Q_BLOCK, DSA_HEADS * DSA_KV_LORA), row),
        out_shape=jax.ShapeDtypeStruct((T, DSA_HEADS * DSA_KV_LORA), BF16),
        scratch_shapes=[pltpu.VMEM((S // DSA_KC, Q_BLOCK, DSA_KC), jnp.int32),
                        pltpu.VMEM((IDX_HEADS * Q_BLOCK, DSA_KC), F32),
                        pltpu.VMEM((DSA_HEADS * Q_BLOCK, LANES), F32),
                        pltpu.VMEM((DSA_HEADS * Q_BLOCK, LANES), F32),
                        pltpu.VMEM((DSA_HEADS * Q_BLOCK, DSA_KV_LORA), F32),
                        pltpu.VMEM((Q_BLOCK, LANES), jnp.int32),
                        pltpu.VMEM((Q_BLOCK, LANES), F32)],
        name="dsa_attn")(qcat, iq, iw, kcat.reshape(B, S, 256), ik.reshape(B, S, LANES))


def _wuvo_kernel(uv_ref, wo_ref, o_ref):
    o_ref[0] = jnp.dot(uv_ref[0], wo_ref[...], precision=HIGHEST,
                       preferred_element_type=F32).astype(BF16)


def _wuvo(w_uv, w_o):
    H, C, V = w_uv.shape
    D = w_o.shape[1]
    out = pl.pallas_call(
        _wuvo_kernel, grid=(H,),
        in_specs=[pl.BlockSpec((1, C, V), lambda h: (h, 0, 0)),
                  pl.BlockSpec((V, D), lambda h: (h, 0))],
        out_specs=pl.BlockSpec((1, C, D), lambda h: (h, 0, 0)),
        out_shape=jax.ShapeDtypeStruct((H, C, D), BF16), name="wuvo")(w_uv, w_o)
    return out.reshape(H * C, D)


def _post_kernel(a_ref, w_ref, x_ref, gate_ref, gn_ref, sc_ref, sh_ref, xo_ref, ho_ref):
    x1 = x_ref[...] + gate_ref[0] * _dot(a_ref[...], w_ref[...])
    xo_ref[...] = x1
    ho_ref[...] = _modulate(x1, gn_ref[...], sc_ref[0], sh_ref[0]).astype(ho_ref.dtype)


def _post_moe_kernel(a_ref, w_ref, x_ref, gate_ref, gn_ref, sc_ref, sh_ref, wr_ref, xo_ref, ho_ref, lg_ref):
    tm, D = x_ref.shape
    x1 = x_ref[...] + gate_ref[0] * _dot(a_ref[...], w_ref[...])
    xo_ref[...] = x1
    h = _modulate(x1, gn_ref[...], sc_ref[0], sh_ref[0])
    for c in range(D // LANES):
        ho_ref[pl.ds(c, tm, stride=D // LANES), :] = h[:, c * LANES:(c + 1) * LANES]
    lg_ref[...] = jnp.dot(h, wr_ref[...], precision=HIGHEST, preferred_element_type=F32)


def _post(a, w, xf, gate, gn, scale, shift, S, wr=None):
    T, D = xf.shape
    tm = TOKEN_TILE
    nps = S // tm
    row = lambda i: (i, 0)
    bat = lambda i: (i // nps, 0, 0)
    cst2 = lambda i: (0, 0)
    in_specs = [pl.BlockSpec((tm, a.shape[1]), row), pl.BlockSpec(w.shape, cst2),
                pl.BlockSpec((tm, D), row), pl.BlockSpec((1, 1, D), bat),
                pl.BlockSpec((1, D), cst2), pl.BlockSpec((1, 1, D), bat), pl.BlockSpec((1, 1, D), bat)]
    if wr is None:
        return pl.pallas_call(
            _post_kernel, grid=(T // tm,), in_specs=in_specs,
            out_specs=[pl.BlockSpec((tm, D), row), pl.BlockSpec((tm, D), row)],
            out_shape=[jax.ShapeDtypeStruct((T, D), F32), jax.ShapeDtypeStruct((T, D), BF16)],
            name="post")(a, w, xf, gate, gn, scale, shift)
    rpt = D // LANES
    return pl.pallas_call(
        _post_moe_kernel, grid=(T // tm,), in_specs=in_specs + [pl.BlockSpec(wr.shape, cst2)],
        out_specs=[pl.BlockSpec((tm, D), row), pl.BlockSpec((tm * rpt, LANES), row),
                   pl.BlockSpec((tm, LANES), row)],
        out_shape=[jax.ShapeDtypeStruct((T, D), F32), jax.ShapeDtypeStruct((T * rpt, LANES), F32),
                   jax.ShapeDtypeStruct((T, LANES), F32)],
        name="post_moe")(a, w, xf, gate, gn, scale, shift, wr)


def _ffn_kernel(h_ref, x_ref, gate_ref, w1_ref, w3_ref, w2_ref, xo_ref, acc_ref):
    j = pl.program_id(1)

    @pl.when(j == 0)
    def _():
        acc_ref[...] = jnp.zeros_like(acc_ref)
    hb = h_ref[...]
    a = _dot(hb, w1_ref[...])
    b = _dot(hb, w3_ref[...])
    acc_ref[...] += _dot((a * _sigmoid(a) * b).astype(BF16), w2_ref[...])

    @pl.when(j == pl.num_programs(1) - 1)
    def _():
        xo_ref[...] = x_ref[...] + gate_ref[0] * acc_ref[...]


def _ffn(h, xf, gate, w1, w3, w2, S, tn):
    T, D = xf.shape
    F = w1.shape[1]
    tm = TOKEN_TILE
    nps = S // tm
    return pl.pallas_call(
        _ffn_kernel, grid=(T // tm, F // tn),
        in_specs=[pl.BlockSpec((tm, D), lambda i, j: (i, 0)),
                  pl.BlockSpec((tm, D), lambda i, j: (i, 0)),
                  pl.BlockSpec((1, 1, D), lambda i, j: (i // nps, 0, 0)),
                  pl.BlockSpec((D, tn), lambda i, j: (0, j)),
                  pl.BlockSpec((D, tn), lambda i, j: (0, j)),
                  pl.BlockSpec((tn, D), lambda i, j: (j, 0))],
        out_specs=pl.BlockSpec((tm, D), lambda i, j: (i, 0)),
        out_shape=jax.ShapeDtypeStruct((T, D), F32),
        scratch_shapes=[pltpu.VMEM((tm, D), F32)],
        name="ffn")(h, xf, gate, w1, w3, w2)


def _nsa_proj_kernel(x_ref, sh_ref, sc_ref, gn_ref, wq_ref, wkv_ref, wg_ref, rc_ref, ra_ref, rb_ref,
                     q_ref, kc_ref, vc_ref, ks_ref, vs_ref, kw_ref, vw_ref, g_ref):
    hb = _modulate(x_ref[...], gn_ref[...], sc_ref[0], sh_ref[0]).astype(BF16)
    rc, ra, rb = rc_ref[...], ra_ref[...], rb_ref[...]

    def roped(v):
        return jnp.concatenate([_rope(v[:, :128], rc, ra, rb), _rope(v[:, 128:], rc, ra, rb)], axis=1)

    for p in range(NSA_HEADS // 2):
        qv = (roped(_dot(hb, wq_ref[:, p * 256:(p + 1) * 256])) * ATT_SCALE).astype(BF16)
        q_ref[2 * p] = qv[:, :128]
        q_ref[2 * p + 1] = qv[:, 128:]
    outs = (kc_ref, vc_ref, ks_ref, vs_ref, kw_ref, vw_ref)
    for n, o_ref in enumerate(outs):
        v = _dot(hb, wkv_ref[:, n * 256:(n + 1) * 256])
        o_ref[...] = (roped(v) if n % 2 == 0 else v).astype(BF16)
    g_ref[...] = _sigmoid(_dot(hb, wg_ref[...]))


def _nsa_proj(xf, shift, scale, gn, wq, wkv, wg, rope, S):
    T, D = xf.shape
    tm = TOKEN_TILE
    nps = S // tm
    row = lambda i: (i, 0)
    bat = lambda i: (i // nps, 0, 0)
    pos = lambda i: (i % nps, 0)
    cst2 = lambda i: (0, 0)
    kv_spec = pl.BlockSpec((tm, 256), row)
    kv_shape = jax.ShapeDtypeStruct((T, 256), BF16)
    return pl.pallas_call(
        _nsa_proj_kernel, grid=(T // tm,),
        in_specs=[pl.BlockSpec((tm, D), row),
                  pl.BlockSpec((1, 1, D), bat), pl.BlockSpec((1, 1, D), bat),
                  pl.BlockSpec((1, D), cst2),
                  pl.BlockSpec(wq.shape, cst2), pl.BlockSpec(wkv.shape, cst2), pl.BlockSpec(wg.shape, cst2),
                  pl.BlockSpec((tm, LANES), pos), pl.BlockSpec((tm, LANES), pos),
                  pl.BlockSpec((tm, LANES), pos)],
        out_specs=[pl.BlockSpec((NSA_HEADS, tm, LANES), lambda i: (0, i, 0))] + [kv_spec] * 6
                  + [pl.BlockSpec((tm, LANES), row)],
        out_shape=[jax.ShapeDtypeStruct((NSA_HEADS, T, LANES), BF16)] + [kv_shape] * 6
                  + [jax.ShapeDtypeStruct((T, LANES), F32)],
        name="nsa_proj")(xf, shift, scale, gn, wq, wkv, wg, *rope)


def _compress_kernel(uk_ref, uv_ref, pe_ref, k1_ref, k2_ref, v1_ref, v2_ref, ko_ref, vo_ref):
    half = CMP_STRIDE * HEAD_DIM
    pe = jnp.broadcast_to(pe_ref[...], (8, 2 * half)).astype(BF16)

    def comp(u_ref, w1_ref, w2_ref):
        bias = _dot(pe, w1_ref[...])[0:1]
        out = None
        for gg in range(2):
            u = u_ref[0, gg]
            a = _dot(u, w1_ref[:half, :])
            b = _dot(u, w1_ref[half:, :])
            nrow = b.shape[0]
            hid = a + pltpu.roll(b, nrow - 1, 0) + bias
            hid = (hid * _sigmoid(hid)).astype(BF16)
            o = _dot(hid, w2_ref[gg])
            out = o if out is None else out + o
        return out
    ko_ref[0] = comp(uk_ref, k1_ref, k2_ref).astype(BF16)
    vo_ref[0] = comp(uv_ref, v1_ref, v2_ref).astype(BF16)


def _compress(uk, uv, pe, k1, k2, v1, v2):
    B, G, nch, W = uk.shape
    u_spec = pl.BlockSpec((1, 2, nch, W), lambda b, p: (b, p, 0, 0))
    w1_spec = pl.BlockSpec(k1.shape, lambda b, p: (0, 0))
    w2_spec = pl.BlockSpec(k2.shape, lambda b, p: (0, 0, 0))
    o_spec = pl.BlockSpec((1, nch, LANES), lambda b, p: (b, 0, p))
    o_shape = jax.ShapeDtypeStruct((B, nch, G * HEAD_DIM), BF16)
    return pl.pallas_call(
        _compress_kernel, grid=(B, G // 2),
        in_specs=[u_spec, u_spec, pl.BlockSpec(pe.shape, lambda b, p: (0, 0)),
                  w1_spec, w2_spec, w1_spec, w2_spec],
        out_specs=[o_spec, o_spec], out_shape=[o_shape, o_shape],
        name="compress")(uk, uv, pe, k1, k2, v1, v2)


def _nsa_attn_kernel(q_ref, g_ref, kcmp_ref, vcmp_ref, ks_ref, vs_ref, kw_ref, vw_ref,
                     covt_ref, exp_ref, o_ref, oc_ref, psum_ref, imp_ref, sel_ref, m_ref, l_ref, acc_ref,
                     osel_ref, *, n_sel):
    QB, KC, HPG = Q_BLOCK, NSA_KC, NSA_HPG
    GR = HPG * QB
    qi = pl.program_id(1)
    qs = qi * QB
    row_t = qs + lax.broadcasted_iota(jnp.int32, (QB, 1), 0)
    nch = kcmp_ref.shape[1]
    nb = covt_ref.shape[0]
    n_ch = qs // KC + 1
    gates = g_ref[...]
    pair = lambda g: slice((g // 2) * 128, (g // 2) * 128 + 128)
    rows = lambda g: slice(g * GR, (g + 1) * GR)
    q_grp = lambda g: q_ref[g * HPG:(g + 1) * HPG].reshape(GR, LANES)

    cmp_end = lax.broadcasted_iota(jnp.int32, (QB, nch), 1) * CMP_STRIDE + (CMP_LEN - 1)
    cbias = _tile_rows(jnp.where(cmp_end <= row_t, 0.0, -jnp.inf), HPG)
    for g in range(NSA_GROUPS):
        s = _dot_nt(q_grp(g), kcmp_ref[0, :, pair(g)]) + cbias
        m = jnp.max(s, axis=1, keepdims=True)
        m = jnp.where(m == -jnp.inf, 0.0, m)
        e = jnp.exp2(s - m)
        p = e * (1.0 / jnp.maximum(jnp.sum(e, axis=1, keepdims=True), 1e-30))
        oc_ref[rows(g), :] = _dot(p.astype(BF16), vcmp_ref[0, :, pair(g)])
        psum_ref[g] = p[0:QB] + p[QB:2 * QB] + p[2 * QB:3 * QB] + p[3 * QB:4 * QB]

    jb = lax.broadcasted_iota(jnp.int32, (nb, QB), 0)
    cur = (qs + lax.broadcasted_iota(jnp.int32, (nb, QB), 1)) // SEL_LEN
    forced = (jb == 0) | (jb == cur) | (jb == cur - 1)
    ri = lax.broadcasted_iota(jnp.int32, (QB, QB), 0)
    ci = lax.broadcasted_iota(jnp.int32, (QB, QB), 1)
    eye = jnp.where(ri == ci, 1.0, 0.0).astype(BF16)
    imps = []
    for g in range(NSA_GROUPS):
        imp = lax.dot_general(covt_ref[...], psum_ref[g], (((1,), (1,)), ((), ())),
                              precision=HIGHEST, preferred_element_type=F32)
        imp = jnp.where(forced, FORCE_SCORE, imp)
        imp = jnp.where(jb <= cur, imp, -jnp.inf)
        imp_ref[g] = imp
        imps.append(imp)

    def rank_body(i, ranks):
        before = jnp.where(i < jb, 1.0, 0.0)
        out = []
        for g in range(NSA_GROUPS):
            ri_ = imp_ref[g, pl.ds(i, 1), :]
            out.append(ranks[g] + jnp.where(ri_ > imps[g], 1.0, 0.0) + jnp.where(ri_ == imps[g], before, 0.0))
        return tuple(out)
    n_live = jnp.minimum((qs + QB - 1) // SEL_LEN + 1, nb)
    ranks = lax.fori_loop(0, n_live, rank_body, (jnp.zeros((nb, QB), F32),) * NSA_GROUPS)
    for g in range(NSA_GROUPS):
        selt = jnp.where(ranks[g] < n_sel, 1.0, 0.0).astype(BF16)
        sel_ref[g] = _dot_nt(eye, selt).astype(BF16)

    _flash_init(m_ref, l_ref, acc_ref)

    def sel_chunk(c, carry):
        k0 = pl.multiple_of(c * KC, KC)
        causal = (k0 + lax.broadcasted_iota(jnp.int32, (QB, KC), 1)) <= row_t
        for g in range(NSA_GROUPS):
            keep = (_dot(sel_ref[g], exp_ref[c]) > 0.5) & causal
            bias = _tile_rows(jnp.where(keep, 0.0, -jnp.inf), HPG)
            s = _dot_nt(q_grp(g), ks_ref[0, pl.ds(k0, KC), pair(g)]) + bias
            _flash_update(s, vs_ref[0, pl.ds(k0, KC), pair(g)], m_ref, l_ref, acc_ref, g * GR)
        return carry
    _paired_loop(n_ch, sel_chunk)
    osel_ref[...] = acc_ref[...] / jnp.maximum(l_ref[...], 1e-30)

    wl = WINDOW + QB
    w0 = pl.multiple_of(jnp.maximum(qs - WINDOW, 0), QB)
    wpos = w0 + lax.broadcasted_iota(jnp.int32, (QB, wl), 1)
    wbias = _tile_rows(jnp.where((wpos <= row_t) & (wpos > row_t - WINDOW), 0.0, -jnp.inf), HPG)
    for g in range(NSA_GROUPS):
        s = _dot_nt(q_grp(g), kw_ref[0, pl.ds(w0, wl), pair(g)]) + wbias
        m = jnp.max(s, axis=1, keepdims=True)
        m = jnp.where(m == -jnp.inf, 0.0, m)
        e = jnp.exp2(s - m)
        den = jnp.maximum(jnp.sum(e, axis=1, keepdims=True), 1e-30)
        acc_ref[rows(g), :] = _dot(e.astype(BF16), vw_ref[0, pl.ds(w0, wl), pair(g)]) / den

    lane = lax.broadcasted_iota(jnp.int32, (QB, LANES), 1)
    pair_out = [None, None]
    for h in range(NSA_HEADS):
        g = h // HPG
        hr = slice(h * QB, (h + 1) * QB)
        o = (gates[:, 3 * h:3 * h + 1] * oc_ref[hr, :] + gates[:, 3 * h + 1:3 * h + 2] * osel_ref[hr, :]
             + gates[:, 3 * h + 2:3 * h + 3] * acc_ref[hr, :])
        pair_out[h % 2] = o
        if h % 2 == 1:
            if g % 2 == 0:
                both = jnp.where(lane < 64, pair_out[0], pltpu.roll(pair_out[1], 64, 1))
            else:
                both = jnp.where(lane < 64, pltpu.roll(pair_out[0], 64, 1), pair_out[1])
            o_ref[:, (h // 2) * 128:(h // 2) * 128 + 128] = both.astype(BF16)


def _nsa_attn(q, gates, kcmp, vcmp, ks, vs, kw, vw, covt, expand, B, S):
    T = B * S
    nq = S // Q_BLOCK
    nch = kcmp.shape[1]
    nb = S // SEL_LEN
    row = lambda b, i: (b * nq + i, 0)
    bat = lambda b, i: (b, 0, 0)
    kv = lambda a: a.reshape(B, S, 256)
    kv_spec = pl.BlockSpec((1, S, 256), bat)
    cmp_spec = pl.BlockSpec((1, nch, 256), bat)
    kern = functools.partial(_nsa_attn_kernel, n_sel=min(SEL_BLOCKS, nb))
    return pl.pallas_call(
        kern, grid=(B, nq),
        in_specs=[pl.BlockSpec((NSA_HEADS, Q_BLOCK, LANES), lambda b, i: (0, b * nq + i, 0)),
                  pl.BlockSpec((Q_BLOCK, LANES), row),
                  cmp_spec, cmp_spec, kv_spec, kv_spec, kv_spec, kv_spec,
                  pl.BlockSpec(covt.shape, lambda b, i: (0, 0)),
                  pl.BlockSpec(expand.shape, lambda b, i: (0, 0, 0))],
        out_specs=pl.BlockSpec((Q_BLOCK, NSA_HEADS * HEAD_DIM), row),
        out_shape=jax.ShapeDtypeStruct((T, NSA_HEADS * HEAD_DIM), BF16),
        scratch_shapes=[pltpu.VMEM((NSA_HEADS * Q_BLOCK, LANES), F32),
                        pltpu.VMEM((NSA_GROUPS, Q_BLOCK, nch), F32),
                        pltpu.VMEM((NSA_GROUPS, nb, Q_BLOCK), F32),
                        pltpu.VMEM((NSA_GROUPS, Q_BLOCK, nb), BF16),
                        pltpu.VMEM((NSA_HEADS * Q_BLOCK, LANES), F32),
                        pltpu.VMEM((NSA_HEADS * Q_BLOCK, LANES), F32),
                        pltpu.VMEM((NSA_HEADS * Q_BLOCK, LANES), F32),
                        pltpu.VMEM((NSA_HEADS * Q_BLOCK, LANES), F32)],
        name="nsa_attn")(q, gates, kcmp, vcmp, kv(ks), kv(vs), kv(kw), kv(vw), covt, expand)


def _router_kernel(lg_ref, route_ref, cnt_ref, carry_ref):
    i = pl.program_id(0)
    tm = lg_ref.shape[0]

    @pl.when(i == 0)
    def _():
        carry_ref[...] = jnp.zeros_like(carry_ref)
    lane = lax.broadcasted_iota(jnp.int32, (tm, LANES), 1).astype(F32)
    lg = jnp.where(lane < N_EXPERTS, lg_ref[...], -jnp.inf)
    v1 = jnp.max(lg, axis=1, keepdims=True)
    i1 = jnp.min(jnp.where(lg == v1, lane, float(LANES)), axis=1, keepdims=True)
    lg2 = jnp.where(lane == i1, -jnp.inf, lg)
    v2 = jnp.max(lg2, axis=1, keepdims=True)
    i2 = jnp.min(jnp.where(lg2 == v2, lane, float(LANES)), axis=1, keepdims=True)
    e2 = jnp.exp(v2 - v1)
    g1 = 1.0 / (1.0 + e2)
    g2 = e2 / (1.0 + e2)
    oh1 = jnp.where(lane == i1, 1.0, 0.0)
    oh2 = jnp.where(lane == i2, 1.0, 0.0)
    both = oh1 + oh2
    ri = lax.broadcasted_iota(jnp.int32, (tm, tm), 0)
    ci = lax.broadcasted_iota(jnp.int32, (tm, tm), 1)
    lower = jnp.where(ri > ci, 1.0, 0.0).astype(BF16)
    tot = carry_ref[0:1, :] + _dot(lower, both.astype(BF16))
    r1 = jnp.sum(oh1 * tot, axis=1, keepdims=True)
    r2 = jnp.sum(oh2 * tot, axis=1, keepdims=True)
    new_carry = carry_ref[...] + jnp.sum(both, axis=0, keepdims=True)
    carry_ref[...] = new_carry
    cnt_ref[...] = new_carry
    out = jnp.zeros((tm, LANES), F32)
    for col, val in enumerate((i1, i2, g1, g2, r1, r2)):
        out = jnp.where(lane == col, val, out)
    route_ref[...] = out


def _router(logits):
    T = logits.shape[0]
    tm = TOKEN_TILE
    return pl.pallas_call(
        _router_kernel, grid=(T // tm,),
        in_specs=[pl.BlockSpec((tm, LANES), lambda i: (i, 0))],
        out_specs=[pl.BlockSpec((tm, LANES), lambda i: (i, 0)), pl.BlockSpec((8, LANES), lambda i: (0, 0))],
        out_shape=[jax.ShapeDtypeStruct((T, LANES), F32), jax.ShapeDtypeStruct((8, LANES), F32)],
        scratch_shapes=[pltpu.VMEM((8, LANES), F32)],
        name="router")(logits)


def _row_copy(src, dst, sem):
    return pltpu.make_async_copy(src, dst, sem)


def _moe_ffn_kernel(be_ref, nu_ref, tab_ref, h_ref, w1_ref, w3_ref, w2_ref, y_ref,
                    xin_ref, xb_ref, acc_ref, yout_ref, sem_in, sem_out, *, nj):
    i = pl.program_id(0)
    j = pl.program_id(1)
    nblk = tab_ref.shape[0] // MOE_BLOCK
    issue_steps = 4
    per_step = MOE_BLOCK // issue_steps
    assert nj > issue_steps and per_step * issue_steps == MOE_BLOCK
    used = i < nu_ref[0]
    slot = i % 2
    nxt = jnp.minimum(i + 1, nblk - 1)
    prv = jnp.maximum(i - 1, 0)

    def tile(ref, t):
        return ref.at[pl.ds(pl.multiple_of(t * RPT, RPT), RPT), :]

    def gather(entry, r, buf):
        src = tab_ref[entry] & ((1 << SRC_BITS) - 1)
        return _row_copy(tile(h_ref, src), tile(xin_ref.at[buf], r), sem_in.at[buf])

    def scatter(entry, r):
        dst = lax.shift_right_logical(tab_ref[entry], SRC_BITS)
        return _row_copy(tile(yout_ref, r), tile(y_ref, dst), sem_out)

    def issue(r0, n):
        r0 = pl.multiple_of(r0, per_step)
        e_nxt = nxt * MOE_BLOCK + r0
        e_prv = prv * MOE_BLOCK + r0
        for u in range(n):
            gather(e_nxt + u, r0 + u, 1 - slot).start(priority=1)
            scatter(e_prv + u, r0 + u).start(priority=1)

    def wait_gathered(buf):
        _row_copy(h_ref.at[pl.ds(0, MOE_BLOCK * RPT), :], xin_ref.at[buf], sem_in.at[buf]).wait()

    def wait_scattered():
        _row_copy(yout_ref, y_ref.at[pl.ds(0, MOE_BLOCK * RPT), :], sem_out).wait()

    @pl.when((i == 0) & (j == 0))
    def _():
        yout_ref[...] = jnp.zeros_like(yout_ref)

        def first(r, carry):
            gather(r, r, 0).start()
            return carry
        lax.fori_loop(0, MOE_BLOCK, first, 0)

    @pl.when(j == 0)
    def _():
        wait_gathered(slot)
        for c in range(RPT):
            xb_ref[:, c * LANES:(c + 1) * LANES] = (
                xin_ref[slot, pl.ds(c, MOE_BLOCK, stride=RPT), :].astype(BF16))
        acc_ref[...] = jnp.zeros_like(acc_ref)

    def expert_step():
        xb = xb_ref[...]
        a = _dot(xb, w1_ref[0])
        b = _dot(xb, w3_ref[0])
        acc_ref[...] += _dot((a * _sigmoid(a) * b).astype(BF16), w2_ref[0])

    issuing = j < issue_steps

    @pl.when(used & issuing)
    def _():
        issue(j * per_step, per_step)
        expert_step()

    @pl.when(used & jnp.logical_not(issuing))
    def _():
        expert_step()

    @pl.when(jnp.logical_not(used) & issuing)
    def _():
        issue(j * per_step, per_step)

    @pl.when(j == nj - 1)
    def _():
        wait_scattered()
        for c in range(RPT):
            yout_ref[pl.ds(c, MOE_BLOCK, stride=RPT), :] = acc_ref[:, c * LANES:(c + 1) * LANES]

    @pl.when((i == pl.num_programs(0) - 1) & (j == nj - 1))
    def _():
        wait_gathered(1 - slot)


def _moe_ffn(block_e, n_used, slot_tab, h, w1, w3, w2, tn):
    D = w1.shape[1]
    assert D == RPT * LANES
    T = h.shape[0] // RPT
    nblk = slot_tab.shape[0] // MOE_BLOCK
    E, _, F = w1.shape
    nj = F // tn
    bi = lambda i: jnp.minimum(i, nblk - 1)
    jj = lambda i, j, nu: jnp.where(i < nu[0], j, nj - 1)
    return pl.pallas_call(
        functools.partial(_moe_ffn_kernel, nj=nj),
        grid_spec=pltpu.PrefetchScalarGridSpec(
            num_scalar_prefetch=3, grid=(nblk + 1, nj),
            in_specs=[pl.BlockSpec(memory_space=pl.ANY),
                      pl.BlockSpec((1, D, tn), lambda i, j, be, nu, tb: (be[bi(i)], 0, jj(i, j, nu))),
                      pl.BlockSpec((1, D, tn), lambda i, j, be, nu, tb: (be[bi(i)], 0, jj(i, j, nu))),
                      pl.BlockSpec((1, tn, D), lambda i, j, be, nu, tb: (be[bi(i)], jj(i, j, nu), 0))],
            out_specs=pl.BlockSpec(memory_space=pl.ANY),
            scratch_shapes=[pltpu.VMEM((2, MOE_BLOCK * RPT, LANES), F32), pltpu.VMEM((MOE_BLOCK, D), BF16),
                            pltpu.VMEM((MOE_BLOCK, D), F32), pltpu.VMEM((MOE_BLOCK * RPT, LANES), F32),
                            pltpu.SemaphoreType.DMA((2,)), pltpu.SemaphoreType.DMA(())]),
        out_shape=jax.ShapeDtypeStruct(((2 * T + MOE_BLOCK) * RPT, LANES), F32),
        name="moe_ffn")(block_e, n_used, slot_tab, h, w1, w3, w2)


def _moe_combine_kernel(y0_ref, y1_ref, x_ref, gate_ref, route_ref, fn_ref, o_ref):
    tm = x_ref.shape[0]
    route = route_ref[...]
    rows = lambda ref: jnp.concatenate([ref[pl.ds(c, tm, stride=RPT), :] for c in range(RPT)], axis=1)
    y = route[:, 2:3] * rows(y0_ref) + route[:, 3:4] * rows(y1_ref)
    o_ref[...] = _rms(x_ref[...] + gate_ref[0] * y, fn_ref[...])


def _moe_combine(y, xf, gate, route, fn, S):
    T, D = xf.shape
    tm = TOKEN_TILE
    nps = S // tm
    nt = T // tm
    return pl.pallas_call(
        _moe_combine_kernel, grid=(nt,),
        in_specs=[pl.BlockSpec((tm * RPT, LANES), lambda i: (i, 0)),
                  pl.BlockSpec((tm * RPT, LANES), lambda i: (nt + i, 0)),
                  pl.BlockSpec((tm, D), lambda i: (i, 0)),
                  pl.BlockSpec((1, 1, D), lambda i: (i // nps, 0, 0)),
                  pl.BlockSpec((tm, LANES), lambda i: (i, 0)),
                  pl.BlockSpec((1, D), lambda i: (0, 0))],
        out_specs=pl.BlockSpec((tm, D), lambda i: (i, 0)),
        out_shape=jax.ShapeDtypeStruct((T, D), F32), name="moe_combine")(y, y, xf, gate, route, fn)


def _rope_tables(S):
    inv = ROPE_THETA ** (-jnp.arange(0, ROPE_DIM, 2, dtype=F32) / ROPE_DIM)
    ang = jnp.arange(S, dtype=F32)[:, None] * inv[None, :]
    cos, sin = jnp.cos(ang), jnp.sin(ang)
    pm = np.arange(LANES) % HEAD_DIM
    col = pm % (ROPE_DIM // 2)
    rc = jnp.where((pm < ROPE_DIM)[None, :], cos[:, col], 1.0)
    ra = jnp.where((pm < ROPE_DIM // 2)[None, :], -sin[:, col], 0.0)
    rb = jnp.where(((pm >= ROPE_DIM // 2) & (pm < ROPE_DIM))[None, :], sin[:, col], 0.0)
    return rc, ra, rb


def _dsa_weights(w_in, w_uk, w_iq):
    D = w_in.shape[0]
    a, b, c, d = DSA_Q_LORA, DSA_Q_LORA + DSA_KV_LORA, DSA_Q_LORA + DSA_KV_LORA + ROPE_DIM, \
        DSA_Q_LORA + DSA_KV_LORA + ROPE_DIM + IDX_DIM
    win = jnp.concatenate([w_in[:, :b], w_in[:, c:d], w_in[:, b:c], w_in[:, d:],
                           jnp.zeros((D, 512 - w_in.shape[1]), F32)], axis=1).astype(BF16)
    H = DSA_HEADS
    blk = jnp.zeros((H, HEAD_DIM, 256), F32)
    blk = blk.at[:, ROPE_DIM:, :DSA_KV_LORA].set(jnp.transpose(w_uk, (0, 2, 1)))
    blk = blk.at[:, :ROPE_DIM, 192:192 + ROPE_DIM].set(jnp.eye(ROPE_DIM, dtype=F32))
    z = jnp.zeros((H // 2, HEAD_DIM, 256), F32)
    wcat = jnp.concatenate([jnp.concatenate([blk[0::2], z], axis=2),
                            jnp.concatenate([z, blk[1::2]], axis=2)], axis=1).astype(BF16)
    wiq = w_iq.reshape(DSA_Q_LORA, IDX_HEADS, IDX_DIM)
    wiq = jnp.concatenate([wiq, jnp.zeros_like(wiq)], axis=2).reshape(DSA_Q_LORA, IDX_HEADS * 128)
    return win, wcat, wiq.astype(BF16)


def _nsa_weights(w_in):
    D = w_in.shape[0]
    nq = NSA_HEADS * HEAD_DIM
    wq = w_in[:, :nq].reshape(D, NSA_HEADS, HEAD_DIM)
    z = jnp.zeros_like(wq)
    odd = ((np.arange(NSA_HEADS) // NSA_HPG) % 2 == 1)[None, :, None]
    wq = jnp.concatenate([jnp.where(odd, z, wq), jnp.where(odd, wq, z)], axis=2).reshape(D, NSA_HEADS * 128)
    wkv = w_in[:, nq:nq + 6 * 256]
    wg = jnp.concatenate([w_in[:, nq + 6 * 256:], jnp.zeros((D, LANES - 3 * NSA_HEADS), F32)], axis=1)
    return wq.astype(BF16), wkv.astype(BF16), wg.astype(BF16)


def _nsa_tables(S):
    nch = S // CMP_STRIDE
    nc = (S - CMP_LEN) // CMP_STRIDE + 1
    nb = S // SEL_LEN
    cstart = np.arange(nch) * CMP_STRIDE
    bstart = np.arange(nb) * SEL_LEN
    cov = ((cstart[None, :] < bstart[:, None] + SEL_LEN) & (cstart[None, :] + CMP_LEN > bstart[:, None])
           & (np.arange(nch)[None, :] < nc)).astype(np.float32)
    kpos = np.arange(S).reshape(S // NSA_KC, 1, NSA_KC)
    expand = (kpos // SEL_LEN == np.arange(nb)[None, :, None]).astype(np.float32)
    return jnp.asarray(cov), jnp.asarray(expand, dtype=BF16)


def _slot_tables(dest, T, n_blocks):
    ns = n_blocks * MOE_BLOCK
    asg = jnp.full((ns,), -1, jnp.int32).at[dest].set(jnp.arange(2 * T, dtype=jnp.int32))
    real = asg >= 0
    src = jnp.where(real, asg // 2, 0)
    spare = 2 * T + jnp.arange(ns, dtype=jnp.int32) % MOE_BLOCK
    dst = jnp.where(real, (asg % 2) * T + asg // 2, spare)
    assert T <= 1 << SRC_BITS and 2 * T + MOE_BLOCK <= 1 << (32 - SRC_BITS)
    return src | lax.shift_left(dst, SRC_BITS)


def _chunk_tokens(a, B, S):
    a = a.reshape(B, S // CMP_STRIDE, CMP_STRIDE, NSA_GROUPS, HEAD_DIM)
    return jnp.transpose(a, (0, 3, 1, 2, 4)).reshape(B, NSA_GROUPS, S // CMP_STRIDE, CMP_STRIDE * HEAD_DIM)


def kernel(x, c, norm_mix, norm_ffn, ada_w, ada_b, final_norm, dsa_w_in, dsa_g_q, dsa_w_uq, dsa_g_kv,
           dsa_w_uk, dsa_w_uv, dsa_w_iq, dsa_w_o, ffn_w1, ffn_w3, ffn_w2, nsa_w_in, nsa_cmp_pe,
           nsa_cmp_k1, nsa_cmp_k2, nsa_cmp_v1, nsa_cmp_v2, nsa_w_o, moe_router, moe_w1, moe_w3, moe_w2):
    B, S, D = x.shape
    T = B * S
    xf = x.reshape(T, D)
    mods = _ada(c, ada_w, ada_b).reshape(4, B, 3, 1, D)
    shift = lambda s: mods[s, :, 0]
    scale = lambda s: mods[s, :, 1]
    gate = lambda s: mods[s, :, 2]
    rope = _rope_tables(S)

    win, wcat, wiq = _dsa_weights(dsa_w_in[0], dsa_w_uk[0], dsa_w_iq[0])
    qcat, iq, kcat, ik, iw = _dsa_proj(
        xf, shift(0), scale(0), norm_mix[0:1], win, dsa_g_q[0:1], dsa_g_kv[0:1],
        dsa_w_uq[0].astype(BF16), wiq, wcat, rope, S)
    olat = _dsa_attn(qcat, iq, iw, kcat, ik, B, S)
    wuvo = _wuvo(dsa_w_uv[0], dsa_w_o[0])
    x1, h1 = _post(olat, wuvo, xf, gate(0), norm_ffn[0:1], scale(1), shift(1), S)
    x2 = _ffn(h1, x1, gate(1), ffn_w1[0].astype(BF16), ffn_w3[0].astype(BF16), ffn_w2[0].astype(BF16),
              S, ffn_w1.shape[2] // 2)

    wq, wkv, wg = _nsa_weights(nsa_w_in[0])
    q, kc, vc, ks, vs, kw, vw, gates = _nsa_proj(xf=x2, shift=shift(2), scale=scale(2), gn=norm_mix[1:2],
                                                 wq=wq, wkv=wkv, wg=wg, rope=rope, S=S)
    zpad = jnp.zeros((CMP_HIDDEN, HEAD_DIM), F32)
    pad2 = lambda w2: jnp.stack([jnp.concatenate([w2, zpad], axis=1),
                                 jnp.concatenate([zpad, w2], axis=1)]).astype(BF16)
    kcmp, vcmp = _compress(_chunk_tokens(kc, B, S), _chunk_tokens(vc, B, S),
                           nsa_cmp_pe[0].reshape(1, CMP_LEN * HEAD_DIM),
                           nsa_cmp_k1[0].astype(BF16), pad2(nsa_cmp_k2[0]),
                           nsa_cmp_v1[0].astype(BF16), pad2(nsa_cmp_v2[0]))
    covt, expand = _nsa_tables(S)
    o = _nsa_attn(q, gates, kcmp, vcmp, ks, vs, kw, vw, covt, expand, B, S)
    wr = jnp.concatenate([moe_router[0], jnp.zeros((D, LANES - N_EXPERTS), F32)], axis=1)
    x3, h3, logits = _post(o, nsa_w_o[0].astype(BF16), x2, gate(2), norm_ffn[1:2], scale(3), shift(3), S, wr)
    route, cnt = _router(logits)
    counts = cnt[0, :N_EXPERTS].astype(jnp.int32)
    padded = (counts + MOE_BLOCK - 1) // MOE_BLOCK * MOE_BLOCK
    ends = jnp.cumsum(padded)
    pstart = ends - padded
    eidx = route[:, 0:2].astype(jnp.int32)
    dest = (pstart[eidx] + route[:, 4:6].astype(jnp.int32)).reshape(-1)
    n_blocks = -(-(T * 2) // MOE_BLOCK) + N_EXPERTS
    block_start = jnp.arange(n_blocks, dtype=jnp.int32) * MOE_BLOCK
    block_e = jnp.minimum(jnp.sum((ends[None, :] <= block_start[:, None]).astype(jnp.int32), axis=1),
                          N_EXPERTS - 1)
    n_used = (ends[-1:] // MOE_BLOCK).astype(jnp.int32)
    y = _moe_ffn(block_e, n_used, _slot_tables(dest, T, n_blocks), h3, moe_w1[0].astype(BF16), moe_w3[0].astype(BF16),
                 moe_w2[0].astype(BF16), 512)
    out = _moe_combine(y, x3, gate(3), route, final_norm.reshape(1, D), S)
    return out.reshape(B, S, D)
```

```python
import functools

import numpy as np
import jax
import jax.numpy as jnp
from jax import lax
from jax.experimental import pallas as pl
from jax.experimental.pallas import tpu as pltpu

F32 = jnp.float32
BF16 = jnp.bfloat16
HIGHEST = lax.Precision.HIGHEST
INT_MIN = -2147483648

HEAD_DIM = 64
ROPE_DIM = 16
ROPE_THETA = 500000.0
Q_BLOCK = 128
NORM_EPS = 1e-6

DSA_HEADS = 16
DSA_NOPE = 48
DSA_Q_LORA = 256
DSA_KV_LORA = 128
IDX_HEADS = 8
IDX_DIM = 64
DSA_TOPK = 256
DSA_KC = 512

NSA_HEADS = 16
NSA_GROUPS = 4
NSA_HPG = 4
CMP_LEN = 32
CMP_STRIDE = 16
CMP_HIDDEN = 256
SEL_LEN = 64
SEL_BLOCKS = 16
WINDOW = 512
FORCE_SCORE = 1e4
NSA_KC = 512

N_EXPERTS = 8
MOE_BLOCK = 512
SRC_BITS = 15
RPT = 8
TOKEN_TILE = 512
ROW_TILE = 256
LANES = 128

LOG2E = 1.4426950408889634
ATT_SCALE = HEAD_DIM ** -0.5 * LOG2E


def _dot(a, b):
    return jnp.dot(a, b, preferred_element_type=F32)


def _dot_nt(a, b):
    return lax.dot_general(a, b, (((1,), (1,)), ((), ())), preferred_element_type=F32)


def _sigmoid(v):
    return 1.0 / (1.0 + jnp.exp(-v))


def _rms(v, g):
    return v * lax.rsqrt(jnp.mean(v * v, axis=-1, keepdims=True) + NORM_EPS) * g


def _modulate(v, g, scale, shift):
    return _rms(v, g) * (1.0 + scale) + shift


def _rope(v, c, sa, sb):
    return v * c + pltpu.roll(v, LANES - 8, 1) * sa + pltpu.roll(v, 8, 1) * sb


def _tile_rows(a, n):
    return jnp.concatenate([a] * n, axis=0) if n > 1 else a


def _tile_lanes(a, n):
    return jnp.concatenate([a] * n, axis=1) if n > 1 else a


def _paired_loop(n, body):
    def two(i, carry):
        return body(2 * i + 1, body(2 * i, carry))
    lax.fori_loop(0, n // 2, two, 0)

    @pl.when(n % 2 == 1)
    def _():
        body(n - 1, 0)


def _flash_init(m_ref, l_ref, acc_ref):
    m_ref[...] = jnp.full(m_ref.shape, -1e30, F32)
    l_ref[...] = jnp.zeros(l_ref.shape, F32)
    acc_ref[...] = jnp.zeros(acc_ref.shape, F32)


def _flash_update(s, v, m_ref, l_ref, acc_ref, row0):
    n = s.shape[0]
    ps = []
    for r in range(0, n, Q_BLOCK):
        rs = slice(row0 + r, row0 + r + Q_BLOCK)
        sl = s[r:r + Q_BLOCK]
        m_old = m_ref[rs, :]
        m_new = jnp.maximum(m_old, jnp.max(sl, axis=1, keepdims=True))
        p = jnp.exp2(sl - _tile_lanes(m_new, sl.shape[1] // LANES))
        alpha = jnp.exp2(m_old - m_new)
        l_ref[rs, :] = alpha * l_ref[rs, :] + jnp.sum(p, axis=1, keepdims=True)
        acc_ref[rs, :] = alpha * acc_ref[rs, :]
        m_ref[rs, :] = m_new
        ps.append(p.astype(BF16))
    pv = _dot(jnp.concatenate(ps, axis=0), v)
    acc_ref[row0:row0 + n, :] += pv


def _ada_kernel(c_ref, w_ref, b_ref, o_ref):
    cv = c_ref[...]
    sc = cv * _sigmoid(cv)
    o_ref[0] = jnp.dot(sc, w_ref[0], precision=HIGHEST, preferred_element_type=F32) + b_ref[0]


def _ada(c, ada_w, ada_b):
    B, D = c.shape
    w = ada_w.reshape(4, D, 3 * D)
    b = ada_b.reshape(4, 1, 3 * D)
    return pl.pallas_call(
        _ada_kernel, grid=(4, 3),
        in_specs=[pl.BlockSpec((B, D), lambda l, j: (0, 0)),
                  pl.BlockSpec((1, D, D), lambda l, j: (l, 0, j)),
                  pl.BlockSpec((1, 1, D), lambda l, j: (l, 0, j))],
        out_specs=pl.BlockSpec((1, B, D), lambda l, j: (l, 0, j)),
        out_shape=jax.ShapeDtypeStruct((4, B, 3 * D), F32), name="ada")(c, w, b)


def _dsa_proj_kernel(x_ref, sh_ref, sc_ref, gn_ref, win_ref, gq_ref, gkv_ref, wuq_ref, wiq_ref,
                     wcat_ref, rc_ref, ra_ref, rb_ref, qcat_ref, iq_ref, kcat_ref, ik_ref, iw_ref):
    h = _modulate(x_ref[...], gn_ref[...], sc_ref[0], sh_ref[0])
    proj = _dot(h.astype(BF16), win_ref[...])
    q_lat = _rms(proj[:, :256], gq_ref[...]).astype(BF16)
    c_kv = _rms(proj[:, 256:384], gkv_ref[...])
    rc, ra, rb = rc_ref[...], ra_ref[...], rb_ref[...]
    rest = _rope(proj[:, 384:512], rc, ra, rb)
    lane = lax.broadcasted_iota(jnp.int32, rest.shape, 1)
    kcat_ref[:, :128] = c_kv.astype(BF16)
    kcat_ref[:, 128:] = jnp.where((lane >= 64) & (lane < 80), rest, 0.0).astype(BF16)
    ik_ref[...] = jnp.where(lane < 64, rest, 0.0).astype(BF16)
    iw_ref[...] = rest * (IDX_HEADS ** -0.5)
    q = _dot(q_lat, wuq_ref[...])
    for p in range(DSA_HEADS // 2):
        qp = (_rope(q[:, p * 128:(p + 1) * 128], rc, ra, rb) * ATT_SCALE).astype(BF16)
        res = _dot(qp, wcat_ref[p]).astype(BF16)
        qcat_ref[2 * p] = res[:, :256]
        qcat_ref[2 * p + 1] = res[:, 256:]
    iqv = _dot(q_lat, wiq_ref[...])
    for hh in range(IDX_HEADS):
        iq_ref[hh] = (_rope(iqv[:, hh * 128:(hh + 1) * 128], rc, ra, rb) * (IDX_DIM ** -0.5)).astype(BF16)


def _dsa_proj(xf, shift, scale, gn, win, gq, gkv, wuq, wiq, wcat, rope, S):
    T, D = xf.shape
    tm = TOKEN_TILE
    nps = S // tm
    row = lambda i: (i, 0)
    bat = lambda i: (i // nps, 0, 0)
    pos = lambda i: (i % nps, 0)
    cst2 = lambda i: (0, 0)
    cst3 = lambda i: (0, 0, 0)
    return pl.pallas_call(
        _dsa_proj_kernel, grid=(T // tm,),
        in_specs=[pl.BlockSpec((tm, D), row),
                  pl.BlockSpec((1, 1, D), bat), pl.BlockSpec((1, 1, D), bat),
                  pl.BlockSpec((1, D), cst2),
                  pl.BlockSpec(win.shape, cst2),
                  pl.BlockSpec((1, DSA_Q_LORA), cst2), pl.BlockSpec((1, DSA_KV_LORA), cst2),
                  pl.BlockSpec(wuq.shape, cst2), pl.BlockSpec(wiq.shape, cst2),
                  pl.BlockSpec(wcat.shape, cst3),
                  pl.BlockSpec((tm, LANES), pos), pl.BlockSpec((tm, LANES), pos),
                  pl.BlockSpec((tm, LANES), pos)],
        out_specs=[pl.BlockSpec((DSA_HEADS, tm, 256), lambda i: (0, i, 0)),
                   pl.BlockSpec((IDX_HEADS, tm, LANES), lambda i: (0, i, 0)),
                   pl.BlockSpec((tm, 256), row),
                   pl.BlockSpec((tm, LANES), row),
                   pl.BlockSpec((tm, LANES), row)],
        out_shape=[jax.ShapeDtypeStruct((DSA_HEADS, T, 256), BF16),
                   jax.ShapeDtypeStruct((IDX_HEADS, T, LANES), BF16),
                   jax.ShapeDtypeStruct((T, 256), BF16),
                   jax.ShapeDtypeStruct((T, LANES), BF16),
                   jax.ShapeDtypeStruct((T, LANES), F32)],
        name="dsa_proj")(xf, shift, scale, gn, win, gq, gkv, wuq, wiq, wcat, *rope)


def _dsa_attn_kernel(q_ref, iq_ref, iw_ref, kcat_ref, ik_ref, o_ref, keys_ref, w_ref, m_ref, l_ref, acc_ref,
                     thr_ref, cnt_ref, *, k_sel, idx_bits):
    QB, KC, H = Q_BLOCK, DSA_KC, DSA_HEADS
    qi = pl.program_id(1)
    n_ch = (qi * QB) // KC + 1
    row_tl = qi * QB + lax.broadcasted_iota(jnp.int32, (QB, LANES), 0)
    iw = iw_ref[...]
    for hh in range(IDX_HEADS):
        w_ref[hh * QB:(hh + 1) * QB, :] = jnp.broadcast_to(iw[:, 80 + hh:81 + hh], (QB, KC))
    iq_all = iq_ref[...].reshape(IDX_HEADS * QB, LANES)

    def score_chunk(c, carry):
        k0 = pl.multiple_of(c * KC, KC)
        ikc = ik_ref[0, pl.ds(k0, KC), :]
        r = jnp.maximum(_dot_nt(iq_all, ikc), 0.0) * w_ref[...]
        sc = r[0:QB]
        for hh in range(1, IDX_HEADS):
            sc = sc + r[hh * QB:(hh + 1) * QB]
        sc = sc + 0.0
        bits = pltpu.bitcast(sc, jnp.int32)
        key = jnp.where(bits < 0, bits ^ 0x7FFFFFFF, bits)
        pos = k0 + lax.broadcasted_iota(jnp.int32, (QB, KC), 1)
        keys_ref[c] = jnp.where(pos <= _tile_lanes(row_tl, KC // LANES), key, INT_MIN)
        return carry
    _paired_loop(n_ch, score_chunk)

    lane_l = lax.broadcasted_iota(jnp.int32, (QB, LANES), 1)

    def count(pred):
        def body(c, a):
            for j in range(KC // LANES):
                kk = keys_ref[c, :, j * LANES:(j + 1) * LANES]
                a = a + jnp.where(pred(kk, c * KC + j * LANES + lane_l), 1.0, 0.0)
            return a
        a = lax.fori_loop(0, n_ch, body, jnp.zeros((QB, LANES), F32))
        return jnp.sum(a, axis=1, keepdims=True)

    def count3(c1, c2, c3):
        def body(c, a):
            for j in range(KC // LANES):
                kk = keys_ref[c, :, j * LANES:(j + 1) * LANES]
                a = a + jnp.where(kk >= c3, 4161, jnp.where(kk >= c2, 65, jnp.where(kk >= c1, 1, 0)))
            return a
        a = lax.fori_loop(0, n_ch, body, jnp.zeros((QB, LANES), jnp.int32))
        lane_sum = lambda x: jnp.sum(x.astype(F32), axis=1, keepdims=True)
        return lane_sum(a & 63), lane_sum((a >> 6) & 63), lane_sum(a >> 12)

    def bit_body(it, carry):
        thr, cnt = carry
        b1 = jnp.left_shift(jnp.int32(1), 31 - 2 * it)
        b0 = jnp.left_shift(jnp.int32(1), 30 - 2 * it)
        c1, c2 = thr ^ b0, thr ^ b1
        c3 = c2 ^ b0
        n1, n2, n3 = count3(c1, c2, c3)
        pick = lambda x3, x2, x1, x0: jnp.where(n3 >= k_sel, x3, jnp.where(n2 >= k_sel, x2,
                                                                           jnp.where(n1 >= k_sel, x1, x0)))
        return pick(c3, c2, c1, thr), pick(n3, n2, n1, cnt)

    def sweeps(first, n):
        thr, cnt = lax.fori_loop(first, first + n, bit_body, (thr_ref[...], cnt_ref[...]))
        thr_ref[...] = thr
        cnt_ref[...] = cnt
    thr_ref[...] = jnp.full((QB, LANES), INT_MIN, jnp.int32)
    cnt_ref[...] = jnp.full((QB, LANES), -1.0, F32)
    sweeps(0, 10)
    few = (row_tl + 1) < k_sel
    for stage in range(3):
        unsettled = jnp.where((cnt_ref[...] == k_sel) | few, 0.0, 1.0)

        @pl.when(jnp.max(unsettled) > 0.0)
        def _():
            sweeps(10 + 2 * stage, 2)
    thr = thr_ref[...]

    n_gt = count(lambda kk, pos: kk > thr)
    n_ge = count(lambda kk, pos: kk >= thr)
    tie = jnp.where((n_ge > k_sel) & (thr > INT_MIN), 1.0, 0.0)

    @pl.when(jnp.max(tie) > 0.0)
    def _():
        need = k_sel - n_gt
        def jbit(it, jcut):
            cand = jcut | jnp.left_shift(jnp.int32(1), idx_bits - 1 - it)
            f = count(lambda kk, pos: (kk == thr) & (pos < cand))
            return jnp.where(f <= need, cand, jcut)
        jcut = lax.fori_loop(0, idx_bits, jbit, jnp.zeros((QB, LANES), jnp.int32))
        def drop(c, carry):
            for j in range(KC // LANES):
                sl = slice(j * LANES, (j + 1) * LANES)
                kk = keys_ref[c, :, sl]
                pos = c * KC + j * LANES + lane_l
                keys_ref[c, :, sl] = jnp.where((kk == thr) & (pos >= jcut), INT_MIN, kk)
            return carry
        lax.fori_loop(0, n_ch, drop, 0)

    thr_eff = _tile_lanes(jnp.maximum(thr, INT_MIN + 1), KC // LANES)
    _flash_init(m_ref, l_ref, acc_ref)
    q_all = q_ref[...].reshape(H * QB, 256)

    def chunk(c, carry):
        k0 = pl.multiple_of(c * KC, KC)
        kc = kcat_ref[0, pl.ds(k0, KC), :]
        bias = jnp.where(keys_ref[c] >= thr_eff, 0.0, -jnp.inf)
        s = _dot_nt(q_all, kc) + _tile_rows(bias, H)
        _flash_update(s, kc[:, :DSA_KV_LORA], m_ref, l_ref, acc_ref, 0)
        return carry
    _paired_loop(n_ch, chunk)
    o = acc_ref[...] / jnp.maximum(l_ref[...], 1e-30)
    for h in range(H):
        o_ref[:, h * 128:(h + 1) * 128] = o[h * QB:(h + 1) * QB].astype(BF16)


def _dsa_attn(qcat, iq, iw, kcat, ik, B, S):
    T = B * S
    nq = S // Q_BLOCK
    row = lambda b, q: (b * nq + q, 0)
    bat = lambda b, q: (b, 0, 0)
    k_sel = min(DSA_TOPK, S // 4)
    assert S // LANES < 64, "per-lane key counts are packed 6 bits each"
    kern = functools.partial(_dsa_attn_kernel, k_sel=k_sel, idx_bits=int(S).bit_length())
    return pl.pallas_call(
        kern, grid=(B, nq),
        in_specs=[pl.BlockSpec((DSA_HEADS, Q_BLOCK, 256), lambda b, q: (0, b * nq + q, 0)),
                  pl.BlockSpec((IDX_HEADS, Q_BLOCK, LANES), lambda b, q: (0, b * nq + q, 0)),
                  pl.BlockSpec((Q_BLOCK, LANES), row),
                  pl.BlockSpec((1, S, 256), bat),
                  pl.BlockSpec((1, S, LANES), bat)],
        out_specs=pl.BlockSpec((Q_BLOCK, DSA_HEADS * DSA_KV_LORA), row),
        out_shape=jax.ShapeDtypeStruct((T, DSA_HEADS * DSA_KV_LORA), BF16),
        scratch_shapes=[pltpu.VMEM((S // DSA_KC, Q_BLOCK, DSA_KC), jnp.int32),
                        pltpu.VMEM((IDX_HEADS * Q_BLOCK, DSA_KC), F32),
                        pltpu.VMEM((DSA_HEADS * Q_BLOCK, LANES), F32),
                        pltpu.VMEM((DSA_HEADS * Q_BLOCK, LANES), F32),
                        pltpu.VMEM((DSA_HEADS * Q_BLOCK, DSA_KV_LORA), F32),
                        pltpu.VMEM((Q_BLOCK, LANES), jnp.int32),
                        pltpu.VMEM((Q_BLOCK, LANES), F32)],
        name="dsa_attn")(qcat, iq, iw, kcat.reshape(B, S, 256), ik.reshape(B, S, LANES))


def _wuvo_kernel(uv_ref, wo_ref, o_ref):
    o_ref[0] = jnp.dot(uv_ref[0], wo_ref[...], precision=HIGHEST,
                       preferred_element_type=F32).astype(BF16)


def _wuvo(w_uv, w_o):
    H, C, V = w_uv.shape
    D = w_o.shape[1]
    out = pl.pallas_call(
        _wuvo_kernel, grid=(H,),
        in_specs=[pl.BlockSpec((1, C, V), lambda h: (h, 0, 0)),
                  pl.BlockSpec((V, D), lambda h: (h, 0))],
        out_specs=pl.BlockSpec((1, C, D), lambda h: (h, 0, 0)),
        out_shape=jax.ShapeDtypeStruct((H, C, D), BF16), name="wuvo")(w_uv, w_o)
    return out.reshape(H * C, D)


def _post_kernel(a_ref, w_ref, x_ref, gate_ref, gn_ref, sc_ref, sh_ref, xo_ref, ho_ref):
    x1 = x_ref[...] + gate_ref[0] * _dot(a_ref[...], w_ref[...])
    xo_ref[...] = x1
    ho_ref[...] = _modulate(x1, gn_ref[...], sc_ref[0], sh_ref[0]).astype(ho_ref.dtype)


def _post_moe_kernel(a_ref, w_ref, x_ref, gate_ref, gn_ref, sc_ref, sh_ref, wr_ref, xo_ref, ho_ref, lg_ref):
    tm, D = x_ref.shape
    x1 = x_ref[...] + gate_ref[0] * _dot(a_ref[...], w_ref[...])
    xo_ref[...] = x1
    h = _modulate(x1, gn_ref[...], sc_ref[0], sh_ref[0])
    for c in range(D // LANES):
        ho_ref[pl.ds(c, tm, stride=D // LANES), :] = h[:, c * LANES:(c + 1) * LANES]
    lg_ref[...] = jnp.dot(h, wr_ref[...], precision=HIGHEST, preferred_element_type=F32)


def _post(a, w, xf, gate, gn, scale, shift, S, wr=None):
    T, D = xf.shape
    tm = TOKEN_TILE
    nps = S // tm
    row = lambda i: (i, 0)
    bat = lambda i: (i // nps, 0, 0)
    cst2 = lambda i: (0, 0)
    in_specs = [pl.BlockSpec((tm, a.shape[1]), row), pl.BlockSpec(w.shape, cst2),
                pl.BlockSpec((tm, D), row), pl.BlockSpec((1, 1, D), bat),
                pl.BlockSpec((1, D), cst2), pl.BlockSpec((1, 1, D), bat), pl.BlockSpec((1, 1, D), bat)]
    if wr is None:
        return pl.pallas_call(
            _post_kernel, grid=(T // tm,), in_specs=in_specs,
            out_specs=[pl.BlockSpec((tm, D), row), pl.BlockSpec((tm, D), row)],
            out_shape=[jax.ShapeDtypeStruct((T, D), F32), jax.ShapeDtypeStruct((T, D), BF16)],
            name="post")(a, w, xf, gate, gn, scale, shift)
    rpt = D // LANES
    return pl.pallas_call(
        _post_moe_kernel, grid=(T // tm,), in_specs=in_specs + [pl.BlockSpec(wr.shape, cst2)],
        out_specs=[pl.BlockSpec((tm, D), row), pl.BlockSpec((tm * rpt, LANES), row),
                   pl.BlockSpec((tm, LANES), row)],
        out_shape=[jax.ShapeDtypeStruct((T, D), F32), jax.ShapeDtypeStruct((T * rpt, LANES), F32),
                   jax.ShapeDtypeStruct((T, LANES), F32)],
        name="post_moe")(a, w, xf, gate, gn, scale, shift, wr)


def _ffn_kernel(h_ref, x_ref, gate_ref, w1_ref, w3_ref, w2_ref, xo_ref, acc_ref):
    j = pl.program_id(1)

    @pl.when(j == 0)
    def _():
        acc_ref[...] = jnp.zeros_like(acc_ref)
    hb = h_ref[...]
    a = _dot(hb, w1_ref[...])
    b = _dot(hb, w3_ref[...])
    acc_ref[...] += _dot((a * _sigmoid(a) * b).astype(BF16), w2_ref[...])

    @pl.when(j == pl.num_programs(1) - 1)
    def _():
        xo_ref[...] = x_ref[...] + gate_ref[0] * acc_ref[...]


def _ffn(h, xf, gate, w1, w3, w2, S, tn):
    T, D = xf.shape
    F = w1.shape[1]
    tm = TOKEN_TILE
    nps = S // tm
    return pl.pallas_call(
        _ffn_kernel, grid=(T // tm, F // tn),
        in_specs=[pl.BlockSpec((tm, D), lambda i, j: (i, 0)),
                  pl.BlockSpec((tm, D), lambda i, j: (i, 0)),
                  pl.BlockSpec((1, 1, D), lambda i, j: (i // nps, 0, 0)),
                  pl.BlockSpec((D, tn), lambda i, j: (0, j)),
                  pl.BlockSpec((D, tn), lambda i, j: (0, j)),
                  pl.BlockSpec((tn, D), lambda i, j: (j, 0))],
        out_specs=pl.BlockSpec((tm, D), lambda i, j: (i, 0)),
        out_shape=jax.ShapeDtypeStruct((T, D), F32),
        scratch_shapes=[pltpu.VMEM((tm, D), F32)],
        name="ffn")(h, xf, gate, w1, w3, w2)


def _nsa_proj_kernel(x_ref, sh_ref, sc_ref, gn_ref, wq_ref, wkv_ref, wg_ref, rc_ref, ra_ref, rb_ref,
                     q_ref, kc_ref, vc_ref, ks_ref, vs_ref, kw_ref, vw_ref, g_ref):
    hb = _modulate(x_ref[...], gn_ref[...], sc_ref[0], sh_ref[0]).astype(BF16)
    rc, ra, rb = rc_ref[...], ra_ref[...], rb_ref[...]

    def roped(v):
        return jnp.concatenate([_rope(v[:, :128], rc, ra, rb), _rope(v[:, 128:], rc, ra, rb)], axis=1)

    for p in range(NSA_HEADS // 2):
        qv = (roped(_dot(hb, wq_ref[:, p * 256:(p + 1) * 256])) * ATT_SCALE).astype(BF16)
        q_ref[2 * p] = qv[:, :128]
        q_ref[2 * p + 1] = qv[:, 128:]
    outs = (kc_ref, vc_ref, ks_ref, vs_ref, kw_ref, vw_ref)
    for n, o_ref in enumerate(outs):
        v = _dot(hb, wkv_ref[:, n * 256:(n + 1) * 256])
        o_ref[...] = (roped(v) if n % 2 == 0 else v).astype(BF16)
    g_ref[...] = _sigmoid(_dot(hb, wg_ref[...]))


def _nsa_proj(xf, shift, scale, gn, wq, wkv, wg, rope, S):
    T, D = xf.shape
    tm = TOKEN_TILE
    nps = S // tm
    row = lambda i: (i, 0)
    bat = lambda i: (i // nps, 0, 0)
    pos = lambda i: (i % nps, 0)
    cst2 = lambda i: (0, 0)
    kv_spec = pl.BlockSpec((tm, 256), row)
    kv_shape = jax.ShapeDtypeStruct((T, 256), BF16)
    return pl.pallas_call(
        _nsa_proj_kernel, grid=(T // tm,),
        in_specs=[pl.BlockSpec((tm, D), row),
                  pl.BlockSpec((1, 1, D), bat), pl.BlockSpec((1, 1, D), bat),
                  pl.BlockSpec((1, D), cst2),
                  pl.BlockSpec(wq.shape, cst2), pl.BlockSpec(wkv.shape, cst2), pl.BlockSpec(wg.shape, cst2),
                  pl.BlockSpec((tm, LANES), pos), pl.BlockSpec((tm, LANES), pos),
                  pl.BlockSpec((tm, LANES), pos)],
        out_specs=[pl.BlockSpec((NSA_HEADS, tm, LANES), lambda i: (0, i, 0))] + [kv_spec] * 6
                  + [pl.BlockSpec((tm, LANES), row)],
        out_shape=[jax.ShapeDtypeStruct((NSA_HEADS, T, LANES), BF16)] + [kv_shape] * 6
                  + [jax.ShapeDtypeStruct((T, LANES), F32)],
        name="nsa_proj")(xf, shift, scale, gn, wq, wkv, wg, *rope)


def _compress_kernel(uk_ref, uv_ref, pe_ref, k1_ref, k2_ref, v1_ref, v2_ref, ko_ref, vo_ref):
    half = CMP_STRIDE * HEAD_DIM
    pe = jnp.broadcast_to(pe_ref[...], (8, 2 * half)).astype(BF16)

    def comp(u_ref, w1_ref, w2_ref):
        bias = _dot(pe, w1_ref[...])[0:1]
        out = None
        for gg in range(2):
            u = u_ref[0, gg]
            a = _dot(u, w1_ref[:half, :])
            b = _dot(u, w1_ref[half:, :])
            nrow = b.shape[0]
            hid = a + pltpu.roll(b, nrow - 1, 0) + bias
            hid = (hid * _sigmoid(hid)).astype(BF16)
            o = _dot(hid, w2_ref[gg])
            out = o if out is None else out + o
        return out
    ko_ref[0] = comp(uk_ref, k1_ref, k2_ref).astype(BF16)
    vo_ref[0] = comp(uv_ref, v1_ref, v2_ref).astype(BF16)


def _compress(uk, uv, pe, k1, k2, v1, v2):
    B, G, nch, W = uk.shape
    u_spec = pl.BlockSpec((1, 2, nch, W), lambda b, p: (b, p, 0, 0))
    w1_spec = pl.BlockSpec(k1.shape, lambda b, p: (0, 0))
    w2_spec = pl.BlockSpec(k2.shape, lambda b, p: (0, 0, 0))
    o_spec = pl.BlockSpec((1, nch, LANES), lambda b, p: (b, 0, p))
    o_shape = jax.ShapeDtypeStruct((B, nch, G * HEAD_DIM), BF16)
    return pl.pallas_call(
        _compress_kernel, grid=(B, G // 2),
        in_specs=[u_spec, u_spec, pl.BlockSpec(pe.shape, lambda b, p: (0, 0)),
                  w1_spec, w2_spec, w1_spec, w2_spec],
        out_specs=[o_spec, o_spec], out_shape=[o_shape, o_shape],
        name="compress")(uk, uv, pe, k1, k2, v1, v2)


def _nsa_attn_kernel(q_ref, g_ref, kcmp_ref, vcmp_ref, ks_ref, vs_ref, kw_ref, vw_ref,
                     covt_ref, exp_ref, o_ref, oc_ref, psum_ref, imp_ref, sel_ref, m_ref, l_ref, acc_ref,
                     osel_ref, *, n_sel):
    QB, KC, HPG = Q_BLOCK, NSA_KC, NSA_HPG
    GR = HPG * QB
    qi = pl.program_id(1)
    qs = qi * QB
    row_t = qs + lax.broadcasted_iota(jnp.int32, (QB, 1), 0)
    nch = kcmp_ref.shape[1]
    nb = covt_ref.shape[0]
    n_ch = qs // KC + 1
    gates = g_ref[...]
    pair = lambda g: slice((g // 2) * 128, (g // 2) * 128 + 128)
    rows = lambda g: slice(g * GR, (g + 1) * GR)
    q_grp = lambda g: q_ref[g * HPG:(g + 1) * HPG].reshape(GR, LANES)

    def compressed(width):
        cmp_end = lax.broadcasted_iota(jnp.int32, (QB, width), 1) * CMP_STRIDE + (CMP_LEN - 1)
        cbias = _tile_rows(jnp.where(cmp_end <= row_t, 0.0, -jnp.inf), HPG)
        for g in range(NSA_GROUPS):
            s = _dot_nt(q_grp(g), kcmp_ref[0, :width, pair(g)]) + cbias
            m = jnp.max(s, axis=1, keepdims=True)
            m = jnp.where(m == -jnp.inf, 0.0, m)
            e = jnp.exp2(s - m)
            p = e * (1.0 / jnp.maximum(jnp.sum(e, axis=1, keepdims=True), 1e-30))
            oc_ref[rows(g), :] = _dot(p.astype(BF16), vcmp_ref[0, :width, pair(g)])
            psum_ref[g, :, :width] = p[0:QB] + p[QB:2 * QB] + p[2 * QB:3 * QB] + p[3 * QB:4 * QB]
            if width < nch:
                psum_ref[g, :, width:] = jnp.zeros((QB, nch - width), F32)
    half = nch // 2
    few_visible = (qs + QB - CMP_LEN) // CMP_STRIDE + 1 <= half

    @pl.when(few_visible)
    def _():
        compressed(half)

    @pl.when(jnp.logical_not(few_visible))
    def _():
        compressed(nch)

    jb = lax.broadcasted_iota(jnp.int32, (nb, QB), 0)
    cur = (qs + lax.broadcasted_iota(jnp.int32, (nb, QB), 1)) // SEL_LEN
    forced = (jb == 0) | (jb == cur) | (jb == cur - 1)
    ri = lax.broadcasted_iota(jnp.int32, (QB, QB), 0)
    ci = lax.broadcasted_iota(jnp.int32, (QB, QB), 1)
    eye = jnp.where(ri == ci, 1.0, 0.0).astype(BF16)
    imps = []
    for g in range(NSA_GROUPS):
        ps = psum_ref[g]
        hi = ps.astype(BF16)
        r1 = ps - hi.astype(F32)
        mid = r1.astype(BF16)
        lo = (r1 - mid.astype(F32)).astype(BF16)
        cov = covt_ref[...]
        imp = (_dot_nt(cov, hi) + _dot_nt(cov, mid)) + _dot_nt(cov, lo)
        imp = jnp.where(forced, FORCE_SCORE, imp)
        imp = jnp.where(jb <= cur, imp, -jnp.inf)
        imp_ref[g] = imp
        imps.append(imp)

    def rank_body(i, ranks):
        before = jnp.where(i < jb, 1.0, 0.0)
        out = []
        for g in range(NSA_GROUPS):
            ri_ = imp_ref[g, pl.ds(i, 1), :]
            out.append(ranks[g] + jnp.where(ri_ > imps[g], 1.0, 0.0) + jnp.where(ri_ == imps[g], before, 0.0))
        return tuple(out)
    n_live = jnp.minimum((qs + QB - 1) // SEL_LEN + 1, nb)
    ranks = lax.fori_loop(0, n_live, rank_body, (jnp.zeros((nb, QB), F32),) * NSA_GROUPS)
    for g in range(NSA_GROUPS):
        selt = jnp.where(ranks[g] < n_sel, 1.0, 0.0).astype(BF16)
        sel_ref[g] = _dot_nt(eye, selt).astype(BF16)

    _flash_init(m_ref, l_ref, acc_ref)

    def sel_chunk(c, carry):
        k0 = pl.multiple_of(c * KC, KC)
        causal = (k0 + lax.broadcasted_iota(jnp.int32, (QB, KC), 1)) <= row_t
        for g in range(NSA_GROUPS):
            keep = (_dot(sel_ref[g], exp_ref[c]) > 0.5) & causal
            bias = _tile_rows(jnp.where(keep, 0.0, -jnp.inf), HPG)
            s = _dot_nt(q_grp(g), ks_ref[0, pl.ds(k0, KC), pair(g)]) + bias
            _flash_update(s, vs_ref[0, pl.ds(k0, KC), pair(g)], m_ref, l_ref, acc_ref, g * GR)
        return carry
    _paired_loop(n_ch, sel_chunk)
    osel_ref[...] = acc_ref[...] / jnp.maximum(l_ref[...], 1e-30)

    wl = WINDOW + QB
    w0 = pl.multiple_of(jnp.maximum(qs - WINDOW, 0), QB)
    wpos = w0 + lax.broadcasted_iota(jnp.int32, (QB, wl), 1)
    wbias = _tile_rows(jnp.where((wpos <= row_t) & (wpos > row_t - WINDOW), 0.0, -jnp.inf), HPG)
    for g in range(NSA_GROUPS):
        s = _dot_nt(q_grp(g), kw_ref[0, pl.ds(w0, wl), pair(g)]) + wbias
        m = jnp.max(s, axis=1, keepdims=True)
        m = jnp.where(m == -jnp.inf, 0.0, m)
        e = jnp.exp2(s - m)
        den = jnp.maximum(jnp.sum(e, axis=1, keepdims=True), 1e-30)
        acc_ref[rows(g), :] = _dot(e.astype(BF16), vw_ref[0, pl.ds(w0, wl), pair(g)]) / den

    lane = lax.broadcasted_iota(jnp.int32, (QB, LANES), 1)
    pair_out = [None, None]
    for h in range(NSA_HEADS):
        g = h // HPG
        hr = slice(h * QB, (h + 1) * QB)
        o = (gates[:, 3 * h:3 * h + 1] * oc_ref[hr, :] + gates[:, 3 * h + 1:3 * h + 2] * osel_ref[hr, :]
             + gates[:, 3 * h + 2:3 * h + 3] * acc_ref[hr, :])
        pair_out[h % 2] = o
        if h % 2 == 1:
            if g % 2 == 0:
                both = jnp.where(lane < 64, pair_out[0], pltpu.roll(pair_out[1], 64, 1))
            else:
                both = jnp.where(lane < 64, pltpu.roll(pair_out[0], 64, 1), pair_out[1])
            o_ref[:, (h // 2) * 128:(h // 2) * 128 + 128] = both.astype(BF16)


def _nsa_attn(q, gates, kcmp, vcmp, ks, vs, kw, vw, covt, expand, B, S):
    T = B * S
    nq = S // Q_BLOCK
    nch = kcmp.shape[1]
    nb = S // SEL_LEN
    row = lambda b, i: (b * nq + i, 0)
    bat = lambda b, i: (b, 0, 0)
    kv = lambda a: a.reshape(B, S, 256)
    kv_spec = pl.BlockSpec((1, S, 256), bat)
    cmp_spec = pl.BlockSpec((1, nch, 256), bat)
    kern = functools.partial(_nsa_attn_kernel, n_sel=min(SEL_BLOCKS, nb))
    return pl.pallas_call(
        kern, grid=(B, nq),
        in_specs=[pl.BlockSpec((NSA_HEADS, Q_BLOCK, LANES), lambda b, i: (0, b * nq + i, 0)),
                  pl.BlockSpec((Q_BLOCK, LANES), row),
                  cmp_spec, cmp_spec, kv_spec, kv_spec, kv_spec, kv_spec,
                  pl.BlockSpec(covt.shape, lambda b, i: (0, 0)),
                  pl.BlockSpec(expand.shape, lambda b, i: (0, 0, 0))],
        out_specs=pl.BlockSpec((Q_BLOCK, NSA_HEADS * HEAD_DIM), row),
        out_shape=jax.ShapeDtypeStruct((T, NSA_HEADS * HEAD_DIM), BF16),
        scratch_shapes=[pltpu.VMEM((NSA_HEADS * Q_BLOCK, LANES), F32),
                        pltpu.VMEM((NSA_GROUPS, Q_BLOCK, nch), F32),
                        pltpu.VMEM((NSA_GROUPS, nb, Q_BLOCK), F32),
                        pltpu.VMEM((NSA_GROUPS, Q_BLOCK, nb), BF16),
                        pltpu.VMEM((NSA_HEADS * Q_BLOCK, LANES), F32),
                        pltpu.VMEM((NSA_HEADS * Q_BLOCK, LANES), F32),
                        pltpu.VMEM((NSA_HEADS * Q_BLOCK, LANES), F32),
                        pltpu.VMEM((NSA_HEADS * Q_BLOCK, LANES), F32)],
        name="nsa_attn")(q, gates, kcmp, vcmp, kv(ks), kv(vs), kv(kw), kv(vw), covt, expand)


def _router_kernel(lg_ref, route_ref, cnt_ref, carry_ref):
    i = pl.program_id(0)
    tm = lg_ref.shape[0]

    @pl.when(i == 0)
    def _():
        carry_ref[...] = jnp.zeros_like(carry_ref)
    lane = lax.broadcasted_iota(jnp.int32, (tm, LANES), 1).astype(F32)
    lg = jnp.where(lane < N_EXPERTS, lg_ref[...], -jnp.inf)
    v1 = jnp.max(lg, axis=1, keepdims=True)
    i1 = jnp.min(jnp.where(lg == v1, lane, float(LANES)), axis=1, keepdims=True)
    lg2 = jnp.where(lane == i1, -jnp.inf, lg)
    v2 = jnp.max(lg2, axis=1, keepdims=True)
    i2 = jnp.min(jnp.where(lg2 == v2, lane, float(LANES)), axis=1, keepdims=True)
    e2 = jnp.exp(v2 - v1)
    g1 = 1.0 / (1.0 + e2)
    g2 = e2 / (1.0 + e2)
    oh1 = jnp.where(lane == i1, 1.0, 0.0)
    oh2 = jnp.where(lane == i2, 1.0, 0.0)
    both = oh1 + oh2
    ri = lax.broadcasted_iota(jnp.int32, (tm, tm), 0)
    ci = lax.broadcasted_iota(jnp.int32, (tm, tm), 1)
    lower = jnp.where(ri > ci, 1.0, 0.0).astype(BF16)
    tot = carry_ref[0:1, :] + _dot(lower, both.astype(BF16))
    r1 = jnp.sum(oh1 * tot, axis=1, keepdims=True)
    r2 = jnp.sum(oh2 * tot, axis=1, keepdims=True)
    new_carry = carry_ref[...] + jnp.sum(both, axis=0, keepdims=True)
    carry_ref[...] = new_carry
    cnt_ref[...] = new_carry
    out = jnp.zeros((tm, LANES), F32)
    for col, val in enumerate((i1, i2, g1, g2, r1, r2)):
        out = jnp.where(lane == col, val, out)
    route_ref[...] = out


def _router(logits):
    T = logits.shape[0]
    tm = TOKEN_TILE
    return pl.pallas_call(
        _router_kernel, grid=(T // tm,),
        in_specs=[pl.BlockSpec((tm, LANES), lambda i: (i, 0))],
        out_specs=[pl.BlockSpec((tm, LANES), lambda i: (i, 0)), pl.BlockSpec((8, LANES), lambda i: (0, 0))],
        out_shape=[jax.ShapeDtypeStruct((T, LANES), F32), jax.ShapeDtypeStruct((8, LANES), F32)],
        scratch_shapes=[pltpu.VMEM((8, LANES), F32)],
        name="router")(logits)


def _row_copy(src, dst, sem):
    return pltpu.make_async_copy(src, dst, sem)


def _moe_ffn_kernel(be_ref, nu_ref, tab_ref, h_ref, w1_ref, w3_ref, w2_ref, y_ref,
                    xin_ref, xb_ref, acc_ref, yout_ref, sem_in, sem_out, *, nj):
    i = pl.program_id(0)
    j = pl.program_id(1)
    nblk = tab_ref.shape[0] // MOE_BLOCK
    issue_steps = 4
    per_step = MOE_BLOCK // issue_steps
    assert nj > issue_steps and per_step * issue_steps == MOE_BLOCK
    used = i < nu_ref[0]
    slot = i % 2
    nxt = jnp.minimum(i + 1, nblk - 1)
    prv = jnp.maximum(i - 1, 0)

    def tile(ref, t):
        return ref.at[pl.ds(pl.multiple_of(t * RPT, RPT), RPT), :]

    def gather(entry, r, buf):
        src = tab_ref[entry] & ((1 << SRC_BITS) - 1)
        return _row_copy(tile(h_ref, src), tile(xin_ref.at[buf], r), sem_in.at[buf])

    def scatter(entry, r):
        dst = lax.shift_right_logical(tab_ref[entry], SRC_BITS)
        return _row_copy(tile(yout_ref, r), tile(y_ref, dst), sem_out)

    def issue(r0, n):
        r0 = pl.multiple_of(r0, per_step)
        e_nxt = nxt * MOE_BLOCK + r0
        e_prv = prv * MOE_BLOCK + r0
        for u in range(n):
            gather(e_nxt + u, r0 + u, 1 - slot).start(priority=1)
            scatter(e_prv + u, r0 + u).start(priority=1)

    def wait_gathered(buf):
        _row_copy(h_ref.at[pl.ds(0, MOE_BLOCK * RPT), :], xin_ref.at[buf], sem_in.at[buf]).wait()

    def wait_scattered():
        _row_copy(yout_ref, y_ref.at[pl.ds(0, MOE_BLOCK * RPT), :], sem_out).wait()

    @pl.when((i == 0) & (j == 0))
    def _():
        yout_ref[...] = jnp.zeros_like(yout_ref)

        def first(r, carry):
            gather(r, r, 0).start()
            return carry
        lax.fori_loop(0, MOE_BLOCK, first, 0)

    @pl.when(j == 0)
    def _():
        wait_gathered(slot)
        for c in range(RPT):
            xb_ref[:, c * LANES:(c + 1) * LANES] = (
                xin_ref[slot, pl.ds(c, MOE_BLOCK, stride=RPT), :].astype(BF16))
        acc_ref[...] = jnp.zeros_like(acc_ref)

    def expert_step():
        xb = xb_ref[...]
        a = _dot(xb, w1_ref[0])
        b = _dot(xb, w3_ref[0])
        acc_ref[...] += _dot((a * _sigmoid(a) * b).astype(BF16), w2_ref[0])

    issuing = j < issue_steps

    @pl.when(used & issuing)
    def _():
        issue(j * per_step, per_step)
        expert_step()

    @pl.when(used & jnp.logical_not(issuing))
    def _():
        expert_step()

    @pl.when(jnp.logical_not(used) & issuing)
    def _():
        issue(j * per_step, per_step)

    @pl.when(j == nj - 1)
    def _():
        wait_scattered()
        for c in range(RPT):
            yout_ref[pl.ds(c, MOE_BLOCK, stride=RPT), :] = acc_ref[:, c * LANES:(c + 1) * LANES]

    @pl.when((i == pl.num_programs(0) - 1) & (j == nj - 1))
    def _():
        wait_gathered(1 - slot)


def _moe_ffn(block_e, n_used, slot_tab, h, w1, w3, w2, tn):
    D = w1.shape[1]
    assert D == RPT * LANES
    T = h.shape[0] // RPT
    nblk = slot_tab.shape[0] // MOE_BLOCK
    E, _, F = w1.shape
    nj = F // tn
    bi = lambda i: jnp.minimum(i, nblk - 1)
    jj = lambda i, j, nu: jnp.where(i < nu[0], j, nj - 1)
    return pl.pallas_call(
        functools.partial(_moe_ffn_kernel, nj=nj),
        grid_spec=pltpu.PrefetchScalarGridSpec(
            num_scalar_prefetch=3, grid=(nblk + 1, nj),
            in_specs=[pl.BlockSpec(memory_space=pl.ANY),
                      pl.BlockSpec((1, D, tn), lambda i, j, be, nu, tb: (be[bi(i)], 0, jj(i, j, nu))),
                      pl.BlockSpec((1, D, tn), lambda i, j, be, nu, tb: (be[bi(i)], 0, jj(i, j, nu))),
                      pl.BlockSpec((1, tn, D), lambda i, j, be, nu, tb: (be[bi(i)], jj(i, j, nu), 0))],
            out_specs=pl.BlockSpec(memory_space=pl.ANY),
            scratch_shapes=[pltpu.VMEM((2, MOE_BLOCK * RPT, LANES), F32), pltpu.VMEM((MOE_BLOCK, D), BF16),
                            pltpu.VMEM((MOE_BLOCK, D), F32), pltpu.VMEM((MOE_BLOCK * RPT, LANES), F32),
                            pltpu.SemaphoreType.DMA((2,)), pltpu.SemaphoreType.DMA(())]),
        out_shape=jax.ShapeDtypeStruct(((2 * T + MOE_BLOCK) * RPT, LANES), F32),
        name="moe_ffn")(block_e, n_used, slot_tab, h, w1, w3, w2)


def _moe_combine_kernel(y0_ref, y1_ref, x_ref, gate_ref, route_ref, fn_ref, o_ref):
    tm = x_ref.shape[0]
    route = route_ref[...]
    rows = lambda ref: jnp.concatenate([ref[pl.ds(c, tm, stride=RPT), :] for c in range(RPT)], axis=1)
    y = route[:, 2:3] * rows(y0_ref) + route[:, 3:4] * rows(y1_ref)
    o_ref[...] = _rms(x_ref[...] + gate_ref[0] * y, fn_ref[...])


def _moe_combine(y, xf, gate, route, fn, S):
    T, D = xf.shape
    tm = TOKEN_TILE
    nps = S // tm
    nt = T // tm
    return pl.pallas_call(
        _moe_combine_kernel, grid=(nt,),
        in_specs=[pl.BlockSpec((tm * RPT, LANES), lambda i: (i, 0)),
                  pl.BlockSpec((tm * RPT, LANES), lambda i: (nt + i, 0)),
                  pl.BlockSpec((tm, D), lambda i: (i, 0)),
                  pl.BlockSpec((1, 1, D), lambda i: (i // nps, 0, 0)),
                  pl.BlockSpec((tm, LANES), lambda i: (i, 0)),
                  pl.BlockSpec((1, D), lambda i: (0, 0))],
        out_specs=pl.BlockSpec((tm, D), lambda i: (i, 0)),
        out_shape=jax.ShapeDtypeStruct((T, D), F32), name="moe_combine")(y, y, xf, gate, route, fn)


def _rope_tables(S):
    inv = ROPE_THETA ** (-jnp.arange(0, ROPE_DIM, 2, dtype=F32) / ROPE_DIM)
    ang = jnp.arange(S, dtype=F32)[:, None] * inv[None, :]
    cos, sin = jnp.cos(ang), jnp.sin(ang)
    pm = np.arange(LANES) % HEAD_DIM
    col = pm % (ROPE_DIM // 2)
    rc = jnp.where((pm < ROPE_DIM)[None, :], cos[:, col], 1.0)
    ra = jnp.where((pm < ROPE_DIM // 2)[None, :], -sin[:, col], 0.0)
    rb = jnp.where(((pm >= ROPE_DIM // 2) & (pm < ROPE_DIM))[None, :], sin[:, col], 0.0)
    return rc, ra, rb


def _dsa_weights(w_in, w_uk, w_iq):
    D = w_in.shape[0]
    a, b, c, d = DSA_Q_LORA, DSA_Q_LORA + DSA_KV_LORA, DSA_Q_LORA + DSA_KV_LORA + ROPE_DIM, \
        DSA_Q_LORA + DSA_KV_LORA + ROPE_DIM + IDX_DIM
    win = jnp.concatenate([w_in[:, :b], w_in[:, c:d], w_in[:, b:c], w_in[:, d:],
                           jnp.zeros((D, 512 - w_in.shape[1]), F32)], axis=1).astype(BF16)
    H = DSA_HEADS
    blk = jnp.zeros((H, HEAD_DIM, 256), F32)
    blk = blk.at[:, ROPE_DIM:, :DSA_KV_LORA].set(jnp.transpose(w_uk, (0, 2, 1)))
    blk = blk.at[:, :ROPE_DIM, 192:192 + ROPE_DIM].set(jnp.eye(ROPE_DIM, dtype=F32))
    z = jnp.zeros((H // 2, HEAD_DIM, 256), F32)
    wcat = jnp.concatenate([jnp.concatenate([blk[0::2], z], axis=2),
                            jnp.concatenate([z, blk[1::2]], axis=2)], axis=1).astype(BF16)
    wiq = w_iq.reshape(DSA_Q_LORA, IDX_HEADS, IDX_DIM)
    wiq = jnp.concatenate([wiq, jnp.zeros_like(wiq)], axis=2).reshape(DSA_Q_LORA, IDX_HEADS * 128)
    return win, wcat, wiq.astype(BF16)


def _nsa_weights(w_in):
    D = w_in.shape[0]
    nq = NSA_HEADS * HEAD_DIM
    wq = w_in[:, :nq].reshape(D, NSA_HEADS, HEAD_DIM)
    z = jnp.zeros_like(wq)
    odd = ((np.arange(NSA_HEADS) // NSA_HPG) % 2 == 1)[None, :, None]
    wq = jnp.concatenate([jnp.where(odd, z, wq), jnp.where(odd, wq, z)], axis=2).reshape(D, NSA_HEADS * 128)
    wkv = w_in[:, nq:nq + 6 * 256]
    wg = jnp.concatenate([w_in[:, nq + 6 * 256:], jnp.zeros((D, LANES - 3 * NSA_HEADS), F32)], axis=1)
    return wq.astype(BF16), wkv.astype(BF16), wg.astype(BF16)


def _nsa_tables(S):
    nch = S // CMP_STRIDE
    nc = (S - CMP_LEN) // CMP_STRIDE + 1
    nb = S // SEL_LEN
    cstart = np.arange(nch) * CMP_STRIDE
    bstart = np.arange(nb) * SEL_LEN
    cov = ((cstart[None, :] < bstart[:, None] + SEL_LEN) & (cstart[None, :] + CMP_LEN > bstart[:, None])
           & (np.arange(nch)[None, :] < nc)).astype(np.float32)
    kpos = np.arange(S).reshape(S // NSA_KC, 1, NSA_KC)
    expand = (kpos // SEL_LEN == np.arange(nb)[None, :, None]).astype(np.float32)
    return jnp.asarray(cov, dtype=BF16), jnp.asarray(expand, dtype=BF16)


def _slot_tables(dest, T, n_blocks):
    ns = n_blocks * MOE_BLOCK
    asg = jnp.full((ns,), -1, jnp.int32).at[dest].set(jnp.arange(2 * T, dtype=jnp.int32))
    real = asg >= 0
    src = jnp.where(real, asg // 2, 0)
    spare = 2 * T + jnp.arange(ns, dtype=jnp.int32) % MOE_BLOCK
    dst = jnp.where(real, (asg % 2) * T + asg // 2, spare)
    assert T <= 1 << SRC_BITS and 2 * T + MOE_BLOCK <= 1 << (32 - SRC_BITS)
    return src | lax.shift_left(dst, SRC_BITS)


def _chunk_tokens(a, B, S):
    a = a.reshape(B, S // CMP_STRIDE, CMP_STRIDE, NSA_GROUPS, HEAD_DIM)
    return jnp.transpose(a, (0, 3, 1, 2, 4)).reshape(B, NSA_GROUPS, S // CMP_STRIDE, CMP_STRIDE * HEAD_DIM)


def kernel(x, c, norm_mix, norm_ffn, ada_w, ada_b, final_norm, dsa_w_in, dsa_g_q, dsa_w_uq, dsa_g_kv,
           dsa_w_uk, dsa_w_uv, dsa_w_iq, dsa_w_o, ffn_w1, ffn_w3, ffn_w2, nsa_w_in, nsa_cmp_pe,
           nsa_cmp_k1, nsa_cmp_k2, nsa_cmp_v1, nsa_cmp_v2, nsa_w_o, moe_router, moe_w1, moe_w3, moe_w2):
    B, S, D = x.shape
    T = B * S
    xf = x.reshape(T, D)
    mods = _ada(c, ada_w, ada_b).reshape(4, B, 3, 1, D)
    shift = lambda s: mods[s, :, 0]
    scale = lambda s: mods[s, :, 1]
    gate = lambda s: mods[s, :, 2]
    rope = _rope_tables(S)

    win, wcat, wiq = _dsa_weights(dsa_w_in[0], dsa_w_uk[0], dsa_w_iq[0])
    qcat, iq, kcat, ik, iw = _dsa_proj(
        xf, shift(0), scale(0), norm_mix[0:1], win, dsa_g_q[0:1], dsa_g_kv[0:1],
        dsa_w_uq[0].astype(BF16), wiq, wcat, rope, S)
    olat = _dsa_attn(qcat, iq, iw, kcat, ik, B, S)
    wuvo = _wuvo(dsa_w_uv[0], dsa_w_o[0])
    x1, h1 = _post(olat, wuvo, xf, gate(0), norm_ffn[0:1], scale(1), shift(1), S)
    x2 = _ffn(h1, x1, gate(1), ffn_w1[0].astype(BF16), ffn_w3[0].astype(BF16), ffn_w2[0].astype(BF16),
              S, ffn_w1.shape[2] // 2)

    wq, wkv, wg = _nsa_weights(nsa_w_in[0])
    q, kc, vc, ks, vs, kw, vw, gates = _nsa_proj(xf=x2, shift=shift(2), scale=scale(2), gn=norm_mix[1:2],
                                                 wq=wq, wkv=wkv, wg=wg, rope=rope, S=S)
    zpad = jnp.zeros((CMP_HIDDEN, HEAD_DIM), F32)
    pad2 = lambda w2: jnp.stack([jnp.concatenate([w2, zpad], axis=1),
                                 jnp.concatenate([zpad, w2], axis=1)]).astype(BF16)
    kcmp, vcmp = _compress(_chunk_tokens(kc, B, S), _chunk_tokens(vc, B, S),
                           nsa_cmp_pe[0].reshape(1, CMP_LEN * HEAD_DIM),
                           nsa_cmp_k1[0].astype(BF16), pad2(nsa_cmp_k2[0]),
                           nsa_cmp_v1[0].astype(BF16), pad2(nsa_cmp_v2[0]))
    covt, expand = _nsa_tables(S)
    o = _nsa_attn(q, gates, kcmp, vcmp, ks, vs, kw, vw, covt, expand, B, S)
    wr = jnp.concatenate([moe_router[0], jnp.zeros((D, LANES - N_EXPERTS), F32)], axis=1)
    x3, h3, logits = _post(o, nsa_w_o[0].astype(BF16), x2, gate(2), norm_ffn[1:2], scale(3), shift(3), S, wr)
    route, cnt = _router(logits)
    counts = cnt[0, :N_EXPERTS].astype(jnp.int32)
    padded = (counts + MOE_BLOCK - 1) // MOE_BLOCK * MOE_BLOCK
    ends = jnp.cumsum(padded)
    pstart = ends - padded
    eidx = route[:, 0:2].astype(jnp.int32)
    dest = (pstart[eidx] + route[:, 4:6].astype(jnp.int32)).reshape(-1)
    n_blocks = -(-(T * 2) // MOE_BLOCK) + N_EXPERTS
    block_start = jnp.arange(n_blocks, dtype=jnp.int32) * MOE_BLOCK
    block_e = jnp.minimum(jnp.sum((ends[None, :] <= block_start[:, None]).astype(jnp.int32), axis=1),
                          N_EXPERTS - 1)
    n_used = (ends[-1:] // MOE_BLOCK).astype(jnp.int32)
    y = _moe_ffn(block_e, n_used, _slot_tables(dest, T, n_blocks), h3, moe_w1[0].astype(BF16), moe_w3[0].astype(BF16),
                 moe_w2[0].astype(BF16), 512)
    out = _moe_combine(y, x3, gate(3), route, final_norm.reshape(1, D), S)
    return out.reshape(B, S, D)
```

```python
import functools

import numpy as np
import jax
import jax.numpy as jnp
from jax import lax
from jax.experimental import pallas as pl
from jax.experimental.pallas import tpu as pltpu

F32 = jnp.float32
BF16 = jnp.bfloat16
HIGHEST = lax.Precision.HIGHEST
INT_MIN = -2147483648

HEAD_DIM = 64
ROPE_DIM = 16
ROPE_THETA = 500000.0
Q_BLOCK = 128
NORM_EPS = 1e-6

DSA_HEADS = 16
DSA_NOPE = 48
DSA_Q_LORA = 256
DSA_KV_LORA = 128
IDX_HEADS = 8
IDX_DIM = 64
DSA_TOPK = 256
DSA_KC = 512
DSA_QB = 256

NSA_HEADS = 16
NSA_GROUPS = 4
NSA_HPG = 4
CMP_LEN = 32
CMP_STRIDE = 16
CMP_HIDDEN = 256
SEL_LEN = 64
SEL_BLOCKS = 16
WINDOW = 512
FORCE_SCORE = 1e4
NSA_KC = 512

N_EXPERTS = 8
MOE_BLOCK = 512
SRC_BITS = 15
RPT = 8
TOKEN_TILE = 512
ROW_TILE = 256
LANES = 128

LOG2E = 1.4426950408889634
ATT_SCALE = HEAD_DIM ** -0.5 * LOG2E


def _dot(a, b):
    return jnp.dot(a, b, preferred_element_type=F32)


def _dot_nt(a, b):
    return lax.dot_general(a, b, (((1,), (1,)), ((), ())), preferred_element_type=F32)


def _sigmoid(v):
    return 1.0 / (1.0 + jnp.exp(-v))


def _rms(v, g):
    return v * lax.rsqrt(jnp.mean(v * v, axis=-1, keepdims=True) + NORM_EPS) * g


def _modulate(v, g, scale, shift):
    return _rms(v, g) * (1.0 + scale) + shift


def _rope(v, c, sa, sb):
    return v * c + pltpu.roll(v, LANES - 8, 1) * sa + pltpu.roll(v, 8, 1) * sb


def _tile_rows(a, n):
    return jnp.concatenate([a] * n, axis=0) if n > 1 else a


def _tile_lanes(a, n):
    return jnp.concatenate([a] * n, axis=1) if n > 1 else a


def _paired_loop(n, body):
    def two(i, carry):
        return body(2 * i + 1, body(2 * i, carry))
    lax.fori_loop(0, n // 2, two, 0)

    @pl.when(n % 2 == 1)
    def _():
        body(n - 1, 0)


def _flash_init(m_ref, l_ref, acc_ref):
    m_ref[...] = jnp.full(m_ref.shape, -1e30, F32)
    l_ref[...] = jnp.zeros(l_ref.shape, F32)
    acc_ref[...] = jnp.zeros(acc_ref.shape, F32)


def _flash_update(s, v, m_ref, l_ref, acc_ref, row0):
    n = s.shape[0]
    ps = []
    for r in range(0, n, Q_BLOCK):
        rs = slice(row0 + r, row0 + r + Q_BLOCK)
        sl = s[r:r + Q_BLOCK]
        m_old = m_ref[rs, :]
        m_new = jnp.maximum(m_old, jnp.max(sl, axis=1, keepdims=True))
        p = jnp.exp2(sl - _tile_lanes(m_new, sl.shape[1] // LANES))
        alpha = jnp.exp2(m_old - m_new)
        l_ref[rs, :] = alpha * l_ref[rs, :] + jnp.sum(p, axis=1, keepdims=True)
        acc_ref[rs, :] = alpha * acc_ref[rs, :]
        m_ref[rs, :] = m_new
        ps.append(p.astype(BF16))
    pv = _dot(jnp.concatenate(ps, axis=0), v)
    acc_ref[row0:row0 + n, :] += pv


def _ada_kernel(c_ref, w_ref, b_ref, o_ref):
    cv = c_ref[...]
    sc = cv * _sigmoid(cv)
    o_ref[0] = jnp.dot(sc, w_ref[0], precision=HIGHEST, preferred_element_type=F32) + b_ref[0]


def _ada(c, ada_w, ada_b):
    B, D = c.shape
    w = ada_w.reshape(4, D, 3 * D)
    b = ada_b.reshape(4, 1, 3 * D)
    return pl.pallas_call(
        _ada_kernel, grid=(4, 3),
        in_specs=[pl.BlockSpec((B, D), lambda l, j: (0, 0)),
                  pl.BlockSpec((1, D, D), lambda l, j: (l, 0, j)),
                  pl.BlockSpec((1, 1, D), lambda l, j: (l, 0, j))],
        out_specs=pl.BlockSpec((1, B, D), lambda l, j: (l, 0, j)),
        out_shape=jax.ShapeDtypeStruct((4, B, 3 * D), F32), name="ada")(c, w, b)


def _dsa_proj_kernel(x_ref, sh_ref, sc_ref, gn_ref, win_ref, gq_ref, gkv_ref, wuq_ref, wiq_ref,
                     wcat_ref, rc_ref, ra_ref, rb_ref, qcat_ref, iq_ref, kcat_ref, ik_ref, iw_ref):
    h = _modulate(x_ref[...], gn_ref[...], sc_ref[0], sh_ref[0])
    proj = _dot(h.astype(BF16), win_ref[...])
    q_lat = _rms(proj[:, :256], gq_ref[...]).astype(BF16)
    c_kv = _rms(proj[:, 256:384], gkv_ref[...])
    rc, ra, rb = rc_ref[...], ra_ref[...], rb_ref[...]
    rest = _rope(proj[:, 384:512], rc, ra, rb)
    lane = lax.broadcasted_iota(jnp.int32, rest.shape, 1)
    kcat_ref[:, :128] = c_kv.astype(BF16)
    kcat_ref[:, 128:] = jnp.where((lane >= 64) & (lane < 80), rest, 0.0).astype(BF16)
    ik_ref[...] = jnp.where(lane < 64, rest, 0.0).astype(BF16)
    iw_ref[...] = rest * (IDX_HEADS ** -0.5)
    q = _dot(q_lat, wuq_ref[...])
    for p in range(DSA_HEADS // 2):
        qp = (_rope(q[:, p * 128:(p + 1) * 128], rc, ra, rb) * ATT_SCALE).astype(BF16)
        res = _dot(qp, wcat_ref[p]).astype(BF16)
        qcat_ref[2 * p] = res[:, :256]
        qcat_ref[2 * p + 1] = res[:, 256:]
    iqv = _dot(q_lat, wiq_ref[...])
    for hh in range(IDX_HEADS):
        iq_ref[hh] = (_rope(iqv[:, hh * 128:(hh + 1) * 128], rc, ra, rb) * (IDX_DIM ** -0.5)).astype(BF16)


def _dsa_proj(xf, shift, scale, gn, win, gq, gkv, wuq, wiq, wcat, rope, S):
    T, D = xf.shape
    tm = TOKEN_TILE
    nps = S // tm
    row = lambda i: (i, 0)
    bat = lambda i: (i // nps, 0, 0)
    pos = lambda i: (i % nps, 0)
    cst2 = lambda i: (0, 0)
    cst3 = lambda i: (0, 0, 0)
    return pl.pallas_call(
        _dsa_proj_kernel, grid=(T // tm,),
        in_specs=[pl.BlockSpec((tm, D), row),
                  pl.BlockSpec((1, 1, D), bat), pl.BlockSpec((1, 1, D), bat),
                  pl.BlockSpec((1, D), cst2),
                  pl.BlockSpec(win.shape, cst2),
                  pl.BlockSpec((1, DSA_Q_LORA), cst2), pl.BlockSpec((1, DSA_KV_LORA), cst2),
                  pl.BlockSpec(wuq.shape, cst2), pl.BlockSpec(wiq.shape, cst2),
                  pl.BlockSpec(wcat.shape, cst3),
                  pl.BlockSpec((tm, LANES), pos), pl.BlockSpec((tm, LANES), pos),
                  pl.BlockSpec((tm, LANES), pos)],
        out_specs=[pl.BlockSpec((DSA_HEADS, tm, 256), lambda i: (0, i, 0)),
                   pl.BlockSpec((IDX_HEADS, tm, LANES), lambda i: (0, i, 0)),
                   pl.BlockSpec((tm, 256), row),
                   pl.BlockSpec((tm, LANES), row),
                   pl.BlockSpec((tm, LANES), row)],
        out_shape=[jax.ShapeDtypeStruct((DSA_HEADS, T, 256), BF16),
                   jax.ShapeDtypeStruct((IDX_HEADS, T, LANES), BF16),
                   jax.ShapeDtypeStruct((T, 256), BF16),
                   jax.ShapeDtypeStruct((T, LANES), BF16),
                   jax.ShapeDtypeStruct((T, LANES), F32)],
        name="dsa_proj")(xf, shift, scale, gn, win, gq, gkv, wuq, wiq, wcat, *rope)


def _dsa_attn_kernel(q_ref, iq_ref, iw_ref, kcat_ref, ik_ref, o_ref, keys_ref, w_ref, m_ref, l_ref, acc_ref,
                     thr_ref, cnt_ref, *, k_sel, idx_bits):
    QB, KC, H = DSA_QB, DSA_KC, DSA_HEADS
    qi = pl.program_id(1)
    n_ch = (qi * QB + QB - 1) // KC + 1
    row_tl = qi * QB + lax.broadcasted_iota(jnp.int32, (QB, LANES), 0)
    iw = iw_ref[...]
    for hh in range(IDX_HEADS):
        w_ref[hh * QB:(hh + 1) * QB, :] = jnp.broadcast_to(iw[:, 80 + hh:81 + hh], (QB, KC))
    iq_all = iq_ref[...].reshape(IDX_HEADS * QB, LANES)

    def score_chunk(c, carry):
        k0 = pl.multiple_of(c * KC, KC)
        ikc = ik_ref[0, pl.ds(k0, KC), :]
        r = jnp.maximum(_dot_nt(iq_all, ikc), 0.0) * w_ref[...]
        sc = r[0:QB]
        for hh in range(1, IDX_HEADS):
            sc = sc + r[hh * QB:(hh + 1) * QB]
        sc = sc + 0.0
        bits = pltpu.bitcast(sc, jnp.int32)
        key = jnp.where(bits < 0, bits ^ 0x7FFFFFFF, bits)
        pos = k0 + lax.broadcasted_iota(jnp.int32, (QB, KC), 1)
        keys_ref[c] = jnp.where(pos <= _tile_lanes(row_tl, KC // LANES), key, INT_MIN)
        return carry
    _paired_loop(n_ch, score_chunk)

    lane_l = lax.broadcasted_iota(jnp.int32, (QB, LANES), 1)

    def count(pred):
        def body(c, a):
            for j in range(KC // LANES):
                kk = keys_ref[c, :, j * LANES:(j + 1) * LANES]
                a = a + jnp.where(pred(kk, c * KC + j * LANES + lane_l), 1.0, 0.0)
            return a
        a = lax.fori_loop(0, n_ch, body, jnp.zeros((QB, LANES), F32))
        return jnp.sum(a, axis=1, keepdims=True)

    def count3(c1, c2, c3):
        halves = [slice(r, r + Q_BLOCK) for r in range(0, QB, Q_BLOCK)]

        def body(c, accs):
            out = []
            for rs, a in zip(halves, accs):
                for j in range(KC // LANES):
                    kk = keys_ref[c, rs, j * LANES:(j + 1) * LANES]
                    a = a + jnp.where(kk >= c3[rs], 4161, jnp.where(kk >= c2[rs], 65, jnp.where(kk >= c1[rs], 1, 0)))
                out.append(a)
            return tuple(out)
        accs = lax.fori_loop(0, n_ch, body, tuple(jnp.zeros((Q_BLOCK, LANES), jnp.int32) for _ in halves))
        a = jnp.concatenate(accs, axis=0)
        lane_sum = lambda x: jnp.sum(x.astype(F32), axis=1, keepdims=True)
        return lane_sum(a & 63), lane_sum((a >> 6) & 63), lane_sum(a >> 12)

    def bit_body(it, carry):
        thr, cnt = carry
        b1 = jnp.left_shift(jnp.int32(1), 31 - 2 * it)
        b0 = jnp.left_shift(jnp.int32(1), 30 - 2 * it)
        c1, c2 = thr ^ b0, thr ^ b1
        c3 = c2 ^ b0
        n1, n2, n3 = count3(c1, c2, c3)
        pick = lambda x3, x2, x1, x0: jnp.where(n3 >= k_sel, x3, jnp.where(n2 >= k_sel, x2,
                                                                           jnp.where(n1 >= k_sel, x1, x0)))
        return pick(c3, c2, c1, thr), pick(n3, n2, n1, cnt)

    def sweeps(first, n):
        thr, cnt = lax.fori_loop(first, first + n, bit_body, (thr_ref[...], cnt_ref[...]))
        thr_ref[...] = thr
        cnt_ref[...] = cnt
    thr_ref[...] = jnp.full((QB, LANES), INT_MIN, jnp.int32)
    cnt_ref[...] = jnp.full((QB, LANES), -1.0, F32)
    sweeps(0, 10)
    few = (row_tl + 1) < k_sel
    for stage in range(3):
        unsettled = jnp.where((cnt_ref[...] == k_sel) | few, 0.0, 1.0)

        @pl.when(jnp.max(unsettled) > 0.0)
        def _():
            sweeps(10 + 2 * stage, 2)
    thr = thr_ref[...]

    n_gt = count(lambda kk, pos: kk > thr)
    n_ge = count(lambda kk, pos: kk >= thr)
    tie = jnp.where((n_ge > k_sel) & (thr > INT_MIN), 1.0, 0.0)

    @pl.when(jnp.max(tie) > 0.0)
    def _():
        need = k_sel - n_gt
        def jbit(it, jcut):
            cand = jcut | jnp.left_shift(jnp.int32(1), idx_bits - 1 - it)
            f = count(lambda kk, pos: (kk == thr) & (pos < cand))
            return jnp.where(f <= need, cand, jcut)
        jcut = lax.fori_loop(0, idx_bits, jbit, jnp.zeros((QB, LANES), jnp.int32))
        def drop(c, carry):
            for j in range(KC // LANES):
                sl = slice(j * LANES, (j + 1) * LANES)
                kk = keys_ref[c, :, sl]
                pos = c * KC + j * LANES + lane_l
                keys_ref[c, :, sl] = jnp.where((kk == thr) & (pos >= jcut), INT_MIN, kk)
            return carry
        lax.fori_loop(0, n_ch, drop, 0)

    thr_eff = _tile_lanes(jnp.maximum(thr, INT_MIN + 1), KC // LANES)
    _flash_init(m_ref, l_ref, acc_ref)
    q_all = q_ref[...].reshape(H * QB, 256)

    def chunk(c, carry):
        k0 = pl.multiple_of(c * KC, KC)
        kc = kcat_ref[0, pl.ds(k0, KC), :]
        bias = jnp.where(keys_ref[c] >= thr_eff, 0.0, -jnp.inf)
        s = _dot_nt(q_all, kc) + _tile_rows(bias, H)
        _flash_update(s, kc[:, :DSA_KV_LORA], m_ref, l_ref, acc_ref, 0)
        return carry
    _paired_loop(n_ch, chunk)
    o = acc_ref[...] / jnp.maximum(l_ref[...], 1e-30)
    for h in range(H):
        o_ref[:, h * 128:(h + 1) * 128] = o[h * QB:(h + 1) * QB].astype(BF16)


def _dsa_attn(qcat, iq, iw, kcat, ik, B, S):
    T = B * S
    QB = DSA_QB
    nq = S // QB
    row = lambda b, q: (b * nq + q, 0)
    bat = lambda b, q: (b, 0, 0)
    k_sel = min(DSA_TOPK, S // 4)
    assert S // LANES < 64, "per-lane key counts are packed 6 bits each"
    kern = functools.partial(_dsa_attn_kernel, k_sel=k_sel, idx_bits=int(S).bit_length())
    return pl.pallas_call(
        kern, grid=(B, nq),
        in_specs=[pl.BlockSpec((DSA_HEADS, QB, 256), lambda b, q: (0, b * nq + q, 0)),
                  pl.BlockSpec((IDX_HEADS, QB, LANES), lambda b, q: (0, b * nq + q, 0)),
                  pl.BlockSpec((QB, LANES), row),
                  pl.BlockSpec((1, S, 256), bat),
                  pl.BlockSpec((1, S, LANES), bat)],
        out_specs=pl.BlockSpec((QB, DSA_HEADS * DSA_KV_LORA), row),
        out_shape=jax.ShapeDtypeStruct((T, DSA_HEADS * DSA_KV_LORA), BF16),
        scratch_shapes=[pltpu.VMEM((S // DSA_KC, QB, DSA_KC), jnp.int32),
                        pltpu.VMEM((IDX_HEADS * QB, DSA_KC), F32),
                        pltpu.VMEM((DSA_HEADS * QB, LANES), F32),
                        pltpu.VMEM((DSA_HEADS * QB, LANES), F32),
                        pltpu.VMEM((DSA_HEADS * QB, DSA_KV_LORA), F32),
                        pltpu.VMEM((QB, LANES), jnp.int32),
                        pltpu.VMEM((QB, LANES), F32)],
        name="dsa_attn")(qcat, iq, iw, kcat.reshape(B, S, 256), ik.reshape(B, S, LANES))


def _wuvo_kernel(uv_ref, wo_ref, o_ref):
    o_ref[0] = jnp.dot(uv_ref[0], wo_ref[...], precision=HIGHEST,
                       preferred_element_type=F32).astype(BF16)


def _wuvo(w_uv, w_o):
    H, C, V = w_uv.shape
    D = w_o.shape[1]
    out = pl.pallas_call(
        _wuvo_kernel, grid=(H,),
        in_specs=[pl.BlockSpec((1, C, V), lambda h: (h, 0, 0)),
                  pl.BlockSpec((V, D), lambda h: (h, 0))],
        out_specs=pl.BlockSpec((1, C, D), lambda h: (h, 0, 0)),
        out_shape=jax.ShapeDtypeStruct((H, C, D), BF16), name="wuvo")(w_uv, w_o)
    return out.reshape(H * C, D)


def _post_kernel(a_ref, w_ref, x_ref, gate_ref, gn_ref, sc_ref, sh_ref, xo_ref, ho_ref):
    x1 = x_ref[...] + gate_ref[0] * _dot(a_ref[...], w_ref[...])
    xo_ref[...] = x1
    ho_ref[...] = _modulate(x1, gn_ref[...], sc_ref[0], sh_ref[0]).astype(ho_ref.dtype)


def _post_moe_kernel(a_ref, w_ref, x_ref, gate_ref, gn_ref, sc_ref, sh_ref, wr_ref, xo_ref, ho_ref, lg_ref):
    tm, D = x_ref.shape
    x1 = x_ref[...] + gate_ref[0] * _dot(a_ref[...], w_ref[...])
    xo_ref[...] = x1
    h = _modulate(x1, gn_ref[...], sc_ref[0], sh_ref[0])
    for c in range(D // LANES):
        ho_ref[pl.ds(c, tm, stride=D // LANES), :] = h[:, c * LANES:(c + 1) * LANES]
    lg_ref[...] = jnp.dot(h, wr_ref[...], precision=HIGHEST, preferred_element_type=F32)


def _post(a, w, xf, gate, gn, scale, shift, S, wr=None):
    T, D = xf.shape
    tm = TOKEN_TILE
    nps = S // tm
    row = lambda i: (i, 0)
    bat = lambda i: (i // nps, 0, 0)
    cst2 = lambda i: (0, 0)
    in_specs = [pl.BlockSpec((tm, a.shape[1]), row), pl.BlockSpec(w.shape, cst2),
                pl.BlockSpec((tm, D), row), pl.BlockSpec((1, 1, D), bat),
                pl.BlockSpec((1, D), cst2), pl.BlockSpec((1, 1, D), bat), pl.BlockSpec((1, 1, D), bat)]
    if wr is None:
        return pl.pallas_call(
            _post_kernel, grid=(T // tm,), in_specs=in_specs,
            out_specs=[pl.BlockSpec((tm, D), row), pl.BlockSpec((tm, D), row)],
            out_shape=[jax.ShapeDtypeStruct((T, D), F32), jax.ShapeDtypeStruct((T, D), BF16)],
            name="post")(a, w, xf, gate, gn, scale, shift)
    rpt = D // LANES
    return pl.pallas_call(
        _post_moe_kernel, grid=(T // tm,), in_specs=in_specs + [pl.BlockSpec(wr.shape, cst2)],
        out_specs=[pl.BlockSpec((tm, D), row), pl.BlockSpec((tm * rpt, LANES), row),
                   pl.BlockSpec((tm, LANES), row)],
        out_shape=[jax.ShapeDtypeStruct((T, D), F32), jax.ShapeDtypeStruct((T * rpt, LANES), F32),
                   jax.ShapeDtypeStruct((T, LANES), F32)],
        name="post_moe")(a, w, xf, gate, gn, scale, shift, wr)


def _ffn_kernel(h_ref, x_ref, gate_ref, w1_ref, w3_ref, w2_ref, xo_ref, acc_ref):
    j = pl.program_id(1)

    @pl.when(j == 0)
    def _():
        acc_ref[...] = jnp.zeros_like(acc_ref)
    hb = h_ref[...]
    a = _dot(hb, w1_ref[...])
    b = _dot(hb, w3_ref[...])
    acc_ref[...] += _dot((a * _sigmoid(a) * b).astype(BF16), w2_ref[...])

    @pl.when(j == pl.num_programs(1) - 1)
    def _():
        xo_ref[...] = x_ref[...] + gate_ref[0] * acc_ref[...]


def _ffn(h, xf, gate, w1, w3, w2, S, tn):
    T, D = xf.shape
    F = w1.shape[1]
    tm = TOKEN_TILE
    nps = S // tm
    return pl.pallas_call(
        _ffn_kernel, grid=(T // tm, F // tn),
        in_specs=[pl.BlockSpec((tm, D), lambda i, j: (i, 0)),
                  pl.BlockSpec((tm, D), lambda i, j: (i, 0)),
                  pl.BlockSpec((1, 1, D), lambda i, j: (i // nps, 0, 0)),
                  pl.BlockSpec((D, tn), lambda i, j: (0, j)),
                  pl.BlockSpec((D, tn), lambda i, j: (0, j)),
                  pl.BlockSpec((tn, D), lambda i, j: (j, 0))],
        out_specs=pl.BlockSpec((tm, D), lambda i, j: (i, 0)),
        out_shape=jax.ShapeDtypeStruct((T, D), F32),
        scratch_shapes=[pltpu.VMEM((tm, D), F32)],
        name="ffn")(h, xf, gate, w1, w3, w2)


def _nsa_proj_kernel(x_ref, sh_ref, sc_ref, gn_ref, wq_ref, wkv_ref, wg_ref, rc_ref, ra_ref, rb_ref,
                     q_ref, kc_ref, vc_ref, ks_ref, vs_ref, kw_ref, vw_ref, g_ref):
    hb = _modulate(x_ref[...], gn_ref[...], sc_ref[0], sh_ref[0]).astype(BF16)
    rc, ra, rb = rc_ref[...], ra_ref[...], rb_ref[...]

    def roped(v):
        return jnp.concatenate([_rope(v[:, :128], rc, ra, rb), _rope(v[:, 128:], rc, ra, rb)], axis=1)

    for p in range(NSA_HEADS // 2):
        qv = (roped(_dot(hb, wq_ref[:, p * 256:(p + 1) * 256])) * ATT_SCALE).astype(BF16)
        q_ref[2 * p] = qv[:, :128]
        q_ref[2 * p + 1] = qv[:, 128:]
    outs = (kc_ref, vc_ref, ks_ref, vs_ref, kw_ref, vw_ref)
    for n, o_ref in enumerate(outs):
        v = _dot(hb, wkv_ref[:, n * 256:(n + 1) * 256])
        o_ref[...] = (roped(v) if n % 2 == 0 else v).astype(BF16)
    g_ref[...] = _sigmoid(_dot(hb, wg_ref[...]))


def _nsa_proj(xf, shift, scale, gn, wq, wkv, wg, rope, S):
    T, D = xf.shape
    tm = TOKEN_TILE
    nps = S // tm
    row = lambda i: (i, 0)
    bat = lambda i: (i // nps, 0, 0)
    pos = lambda i: (i % nps, 0)
    cst2 = lambda i: (0, 0)
    kv_spec = pl.BlockSpec((tm, 256), row)
    kv_shape = jax.ShapeDtypeStruct((T, 256), BF16)
    return pl.pallas_call(
        _nsa_proj_kernel, grid=(T // tm,),
        in_specs=[pl.BlockSpec((tm, D), row),
                  pl.BlockSpec((1, 1, D), bat), pl.BlockSpec((1, 1, D), bat),
                  pl.BlockSpec((1, D), cst2),
                  pl.BlockSpec(wq.shape, cst2), pl.BlockSpec(wkv.shape, cst2), pl.BlockSpec(wg.shape, cst2),
                  pl.BlockSpec((tm, LANES), pos), pl.BlockSpec((tm, LANES), pos),
                  pl.BlockSpec((tm, LANES), pos)],
        out_specs=[pl.BlockSpec((NSA_HEADS, tm, LANES), lambda i: (0, i, 0))] + [kv_spec] * 6
                  + [pl.BlockSpec((tm, LANES), row)],
        out_shape=[jax.ShapeDtypeStruct((NSA_HEADS, T, LANES), BF16)] + [kv_shape] * 6
                  + [jax.ShapeDtypeStruct((T, LANES), F32)],
        name="nsa_proj")(xf, shift, scale, gn, wq, wkv, wg, *rope)


def _compress_kernel(uk_ref, uv_ref, pe_ref, k1_ref, k2_ref, v1_ref, v2_ref, ko_ref, vo_ref):
    half = CMP_STRIDE * HEAD_DIM
    pe = jnp.broadcast_to(pe_ref[...], (8, 2 * half)).astype(BF16)

    def comp(u_ref, w1_ref, w2_ref):
        bias = _dot(pe, w1_ref[...])[0:1]
        out = None
        for gg in range(2):
            u = u_ref[0, gg]
            a = _dot(u, w1_ref[:half, :])
            b = _dot(u, w1_ref[half:, :])
            nrow = b.shape[0]
            hid = a + pltpu.roll(b, nrow - 1, 0) + bias
            hid = (hid * _sigmoid(hid)).astype(BF16)
            o = _dot(hid, w2_ref[gg])
            out = o if out is None else out + o
        return out
    ko_ref[0] = comp(uk_ref, k1_ref, k2_ref).astype(BF16)
    vo_ref[0] = comp(uv_ref, v1_ref, v2_ref).astype(BF16)


def _compress(uk, uv, pe, k1, k2, v1, v2):
    B, G, nch, W = uk.shape
    u_spec = pl.BlockSpec((1, 2, nch, W), lambda b, p: (b, p, 0, 0))
    w1_spec = pl.BlockSpec(k1.shape, lambda b, p: (0, 0))
    w2_spec = pl.BlockSpec(k2.shape, lambda b, p: (0, 0, 0))
    o_spec = pl.BlockSpec((1, nch, LANES), lambda b, p: (b, 0, p))
    o_shape = jax.ShapeDtypeStruct((B, nch, G * HEAD_DIM), BF16)
    return pl.pallas_call(
        _compress_kernel, grid=(B, G // 2),
        in_specs=[u_spec, u_spec, pl.BlockSpec(pe.shape, lambda b, p: (0, 0)),
                  w1_spec, w2_spec, w1_spec, w2_spec],
        out_specs=[o_spec, o_spec], out_shape=[o_shape, o_shape],
        name="compress")(uk, uv, pe, k1, k2, v1, v2)


def _nsa_attn_kernel(q_ref, g_ref, kcmp_ref, vcmp_ref, ks_ref, vs_ref, kw_ref, vw_ref,
                     covt_ref, exp_ref, o_ref, oc_ref, psum_ref, imp_ref, sel_ref, m_ref, l_ref, acc_ref,
                     osel_ref, *, n_sel):
    QB, KC, HPG = Q_BLOCK, NSA_KC, NSA_HPG
    GR = HPG * QB
    qi = pl.program_id(1)
    qs = qi * QB
    row_t = qs + lax.broadcasted_iota(jnp.int32, (QB, 1), 0)
    nch = kcmp_ref.shape[1]
    nb = covt_ref.shape[0]
    n_ch = qs // KC + 1
    gates = g_ref[...]
    pair = lambda g: slice((g // 2) * 128, (g // 2) * 128 + 128)
    rows = lambda g: slice(g * GR, (g + 1) * GR)
    q_grp = lambda g: q_ref[g * HPG:(g + 1) * HPG].reshape(GR, LANES)

    cmp_end = lax.broadcasted_iota(jnp.int32, (QB, nch), 1) * CMP_STRIDE + (CMP_LEN - 1)
    cbias = _tile_rows(jnp.where(cmp_end <= row_t, 0.0, -jnp.inf), HPG)
    for g in range(NSA_GROUPS):
        s = _dot_nt(q_grp(g), kcmp_ref[0, :, pair(g)]) + cbias
        m = jnp.max(s, axis=1, keepdims=True)
        m = jnp.where(m == -jnp.inf, 0.0, m)
        e = jnp.exp2(s - m)
        p = e * (1.0 / jnp.maximum(jnp.sum(e, axis=1, keepdims=True), 1e-30))
        oc_ref[rows(g), :] = _dot(p.astype(BF16), vcmp_ref[0, :, pair(g)])
        psum_ref[g] = p[0:QB] + p[QB:2 * QB] + p[2 * QB:3 * QB] + p[3 * QB:4 * QB]

    jb = lax.broadcasted_iota(jnp.int32, (nb, QB), 0)
    cur = (qs + lax.broadcasted_iota(jnp.int32, (nb, QB), 1)) // SEL_LEN
    forced = (jb == 0) | (jb == cur) | (jb == cur - 1)
    ri = lax.broadcasted_iota(jnp.int32, (QB, QB), 0)
    ci = lax.broadcasted_iota(jnp.int32, (QB, QB), 1)
    eye = jnp.where(ri == ci, 1.0, 0.0).astype(BF16)
    imps = []
    for g in range(NSA_GROUPS):
        imp = lax.dot_general(covt_ref[...], psum_ref[g], (((1,), (1,)), ((), ())),
                              precision=HIGHEST, preferred_element_type=F32)
        imp = jnp.where(forced, FORCE_SCORE, imp)
        imp = jnp.where(jb <= cur, imp, -jnp.inf)
        imp_ref[g] = imp
        imps.append(imp)

    def rank_body(i, ranks):
        before = jnp.where(i < jb, 1.0, 0.0)
        out = []
        for g in range(NSA_GROUPS):
            ri_ = imp_ref[g, pl.ds(i, 1), :]
            out.append(ranks[g] + jnp.where(ri_ > imps[g], 1.0, 0.0) + jnp.where(ri_ == imps[g], before, 0.0))
        return tuple(out)
    n_live = jnp.minimum((qs + QB - 1) // SEL_LEN + 1, nb)
    ranks = lax.fori_loop(0, n_live, rank_body, (jnp.zeros((nb, QB), F32),) * NSA_GROUPS)
    for g in range(NSA_GROUPS):
        selt = jnp.where(ranks[g] < n_sel, 1.0, 0.0).astype(BF16)
        sel_ref[g] = _dot_nt(eye, selt).astype(BF16)

    _flash_init(m_ref, l_ref, acc_ref)

    def sel_chunk(c, carry):
        k0 = pl.multiple_of(c * KC, KC)
        causal = (k0 + lax.broadcasted_iota(jnp.int32, (QB, KC), 1)) <= row_t
        for g in range(NSA_GROUPS):
            keep = (_dot(sel_ref[g], exp_ref[c]) > 0.5) & causal
            bias = _tile_rows(jnp.where(keep, 0.0, -jnp.inf), HPG)
            s = _dot_nt(q_grp(g), ks_ref[0, pl.ds(k0, KC), pair(g)]) + bias
            _flash_update(s, vs_ref[0, pl.ds(k0, KC), pair(g)], m_ref, l_ref, acc_ref, g * GR)
        return carry
    _paired_loop(n_ch, sel_chunk)
    osel_ref[...] = acc_ref[...] / jnp.maximum(l_ref[...], 1e-30)

    wl = WINDOW + QB
    w0 = pl.multiple_of(jnp.maximum(qs - WINDOW, 0), QB)
    wpos = w0 + lax.broadcasted_iota(jnp.int32, (QB, wl), 1)
    wbias = _tile_rows(jnp.where((wpos <= row_t) & (wpos > row_t - WINDOW), 0.0, -jnp.inf), HPG)
    for g in range(NSA_GROUPS):
        s = _dot_nt(q_grp(g), kw_ref[0, pl.ds(w0, wl), pair(g)]) + wbias
        m = jnp.max(s, axis=1, keepdims=True)
        m = jnp.where(m == -jnp.inf, 0.0, m)
        e = jnp.exp2(s - m)
        den = jnp.maximum(jnp.sum(e, axis=1, keepdims=True), 1e-30)
        acc_ref[rows(g), :] = _dot(e.astype(BF16), vw_ref[0, pl.ds(w0, wl), pair(g)]) / den

    lane = lax.broadcasted_iota(jnp.int32, (QB, LANES), 1)
    pair_out = [None, None]
    for h in range(NSA_HEADS):
        g = h // HPG
        hr = slice(h * QB, (h + 1) * QB)
        o = (gates[:, 3 * h:3 * h + 1] * oc_ref[hr, :] + gates[:, 3 * h + 1:3 * h + 2] * osel_ref[hr, :]
             + gates[:, 3 * h + 2:3 * h + 3] * acc_ref[hr, :])
        pair_out[h % 2] = o
        if h % 2 == 1:
            if g % 2 == 0:
                both = jnp.where(lane < 64, pair_out[0], pltpu.roll(pair_out[1], 64, 1))
            else:
                both = jnp.where(lane < 64, pltpu.roll(pair_out[0], 64, 1), pair_out[1])
            o_ref[:, (h // 2) * 128:(h // 2) * 128 + 128] = both.astype(BF16)


def _nsa_attn(q, gates, kcmp, vcmp, ks, vs, kw, vw, covt, expand, B, S):
    T = B * S
    nq = S // Q_BLOCK
    nch = kcmp.shape[1]
    nb = S // SEL_LEN
    row = lambda b, i: (b * nq + i, 0)
    bat = lambda b, i: (b, 0, 0)
    kv = lambda a: a.reshape(B, S, 256)
    kv_spec = pl.BlockSpec((1, S, 256), bat)
    cmp_spec = pl.BlockSpec((1, nch, 256), bat)
    kern = functools.partial(_nsa_attn_kernel, n_sel=min(SEL_BLOCKS, nb))
    return pl.pallas_call(
        kern, grid=(B, nq),
        in_specs=[pl.BlockSpec((NSA_HEADS, Q_BLOCK, LANES), lambda b, i: (0, b * nq + i, 0)),
                  pl.BlockSpec((Q_BLOCK, LANES), row),
                  cmp_spec, cmp_spec, kv_spec, kv_spec, kv_spec, kv_spec,
                  pl.BlockSpec(covt.shape, lambda b, i: (0, 0)),
                  pl.BlockSpec(expand.shape, lambda b, i: (0, 0, 0))],
        out_specs=pl.BlockSpec((Q_BLOCK, NSA_HEADS * HEAD_DIM), row),
        out_shape=jax.ShapeDtypeStruct((T, NSA_HEADS * HEAD_DIM), BF16),
        scratch_shapes=[pltpu.VMEM((NSA_HEADS * Q_BLOCK, LANES), F32),
                        pltpu.VMEM((NSA_GROUPS, Q_BLOCK, nch), F32),
                        pltpu.VMEM((NSA_GROUPS, nb, Q_BLOCK), F32),
                        pltpu.VMEM((NSA_GROUPS, Q_BLOCK, nb), BF16),
                        pltpu.VMEM((NSA_HEADS * Q_BLOCK, LANES), F32),
                        pltpu.VMEM((NSA_HEADS * Q_BLOCK, LANES), F32),
                        pltpu.VMEM((NSA_HEADS * Q_BLOCK, LANES), F32),
                        pltpu.VMEM((NSA_HEADS * Q_BLOCK, LANES), F32)],
        name="nsa_attn")(q, gates, kcmp, vcmp, kv(ks), kv(vs), kv(kw), kv(vw), covt, expand)


def _router_kernel(lg_ref, route_ref, cnt_ref, carry_ref):
    i = pl.program_id(0)
    tm = lg_ref.shape[0]

    @pl.when(i == 0)
    def _():
        carry_ref[...] = jnp.zeros_like(carry_ref)
    lane = lax.broadcasted_iota(jnp.int32, (tm, LANES), 1).astype(F32)
    lg = jnp.where(lane < N_EXPERTS, lg_ref[...], -jnp.inf)
    v1 = jnp.max(lg, axis=1, keepdims=True)
    i1 = jnp.min(jnp.where(lg == v1, lane, float(LANES)), axis=1, keepdims=True)
    lg2 = jnp.where(lane == i1, -jnp.inf, lg)
    v2 = jnp.max(lg2, axis=1, keepdims=True)
    i2 = jnp.min(jnp.where(lg2 == v2, lane, float(LANES)), axis=1, keepdims=True)
    e2 = jnp.exp(v2 - v1)
    g1 = 1.0 / (1.0 + e2)
    g2 = e2 / (1.0 + e2)
    oh1 = jnp.where(lane == i1, 1.0, 0.0)
    oh2 = jnp.where(lane == i2, 1.0, 0.0)
    both = oh1 + oh2
    ri = lax.broadcasted_iota(jnp.int32, (tm, tm), 0)
    ci = lax.broadcasted_iota(jnp.int32, (tm, tm), 1)
    lower = jnp.where(ri > ci, 1.0, 0.0).astype(BF16)
    tot = carry_ref[0:1, :] + _dot(lower, both.astype(BF16))
    r1 = jnp.sum(oh1 * tot, axis=1, keepdims=True)
    r2 = jnp.sum(oh2 * tot, axis=1, keepdims=True)
    new_carry = carry_ref[...] + jnp.sum(both, axis=0, keepdims=True)
    carry_ref[...] = new_carry
    cnt_ref[...] = new_carry
    out = jnp.zeros((tm, LANES), F32)
    for col, val in enumerate((i1, i2, g1, g2, r1, r2)):
        out = jnp.where(lane == col, val, out)
    route_ref[...] = out


def _router(logits):
    T = logits.shape[0]
    tm = TOKEN_TILE
    return pl.pallas_call(
        _router_kernel, grid=(T // tm,),
        in_specs=[pl.BlockSpec((tm, LANES), lambda i: (i, 0))],
        out_specs=[pl.BlockSpec((tm, LANES), lambda i: (i, 0)), pl.BlockSpec((8, LANES), lambda i: (0, 0))],
        out_shape=[jax.ShapeDtypeStruct((T, LANES), F32), jax.ShapeDtypeStruct((8, LANES), F32)],
        scratch_shapes=[pltpu.VMEM((8, LANES), F32)],
        name="router")(logits)


def _row_copy(src, dst, sem):
    return pltpu.make_async_copy(src, dst, sem)


def _moe_ffn_kernel(be_ref, nu_ref, tab_ref, h_ref, w1_ref, w3_ref, w2_ref, y_ref,
                    xin_ref, xb_ref, acc_ref, yout_ref, sem_in, sem_out, *, nj):
    i = pl.program_id(0)
    j = pl.program_id(1)
    nblk = tab_ref.shape[0] // MOE_BLOCK
    issue_steps = 4
    per_step = MOE_BLOCK // issue_steps
    assert nj > issue_steps and per_step * issue_steps == MOE_BLOCK
    used = i < nu_ref[0]
    slot = i % 2
    nxt = jnp.minimum(i + 1, nblk - 1)
    prv = jnp.maximum(i - 1, 0)

    def tile(ref, t):
        return ref.at[pl.ds(pl.multiple_of(t * RPT, RPT), RPT), :]

    def gather(entry, r, buf):
        src = tab_ref[entry] & ((1 << SRC_BITS) - 1)
        return _row_copy(tile(h_ref, src), tile(xin_ref.at[buf], r), sem_in.at[buf])

    def scatter(entry, r):
        dst = lax.shift_right_logical(tab_ref[entry], SRC_BITS)
        return _row_copy(tile(yout_ref, r), tile(y_ref, dst), sem_out)

    def issue(r0, n):
        r0 = pl.multiple_of(r0, per_step)
        e_nxt = nxt * MOE_BLOCK + r0
        e_prv = prv * MOE_BLOCK + r0
        for u in range(n):
            gather(e_nxt + u, r0 + u, 1 - slot).start(priority=1)
            scatter(e_prv + u, r0 + u).start(priority=1)

    def wait_gathered(buf):
        _row_copy(h_ref.at[pl.ds(0, MOE_BLOCK * RPT), :], xin_ref.at[buf], sem_in.at[buf]).wait()

    def wait_scattered():
        _row_copy(yout_ref, y_ref.at[pl.ds(0, MOE_BLOCK * RPT), :], sem_out).wait()

    @pl.when((i == 0) & (j == 0))
    def _():
        yout_ref[...] = jnp.zeros_like(yout_ref)

        def first(r, carry):
            gather(r, r, 0).start()
            return carry
        lax.fori_loop(0, MOE_BLOCK, first, 0)

    @pl.when(j == 0)
    def _():
        wait_gathered(slot)
        for c in range(RPT):
            xb_ref[:, c * LANES:(c + 1) * LANES] = (
                xin_ref[slot, pl.ds(c, MOE_BLOCK, stride=RPT), :].astype(BF16))
        acc_ref[...] = jnp.zeros_like(acc_ref)

    def expert_step():
        xb = xb_ref[...]
        a = _dot(xb, w1_ref[0])
        b = _dot(xb, w3_ref[0])
        acc_ref[...] += _dot((a * _sigmoid(a) * b).astype(BF16), w2_ref[0])

    issuing = j < issue_steps

    @pl.when(used & issuing)
    def _():
        issue(j * per_step, per_step)
        expert_step()

    @pl.when(used & jnp.logical_not(issuing))
    def _():
        expert_step()

    @pl.when(jnp.logical_not(used) & issuing)
    def _():
        issue(j * per_step, per_step)

    @pl.when(j == nj - 1)
    def _():
        wait_scattered()
        for c in range(RPT):
            yout_ref[pl.ds(c, MOE_BLOCK, stride=RPT), :] = acc_ref[:, c * LANES:(c + 1) * LANES]

    @pl.when((i == pl.num_programs(0) - 1) & (j == nj - 1))
    def _():
        wait_gathered(1 - slot)


def _moe_ffn(block_e, n_used, slot_tab, h, w1, w3, w2, tn):
    D = w1.shape[1]
    assert D == RPT * LANES
    T = h.shape[0] // RPT
    nblk = slot_tab.shape[0] // MOE_BLOCK
    E, _, F = w1.shape
    nj = F // tn
    bi = lambda i: jnp.minimum(i, nblk - 1)
    jj = lambda i, j, nu: jnp.where(i < nu[0], j, nj - 1)
    return pl.pallas_call(
        functools.partial(_moe_ffn_kernel, nj=nj),
        grid_spec=pltpu.PrefetchScalarGridSpec(
            num_scalar_prefetch=3, grid=(nblk + 1, nj),
            in_specs=[pl.BlockSpec(memory_space=pl.ANY),
                      pl.BlockSpec((1, D, tn), lambda i, j, be, nu, tb: (be[bi(i)], 0, jj(i, j, nu))),
                      pl.BlockSpec((1, D, tn), lambda i, j, be, nu, tb: (be[bi(i)], 0, jj(i, j, nu))),
                      pl.BlockSpec((1, tn, D), lambda i, j, be, nu, tb: (be[bi(i)], jj(i, j, nu), 0))],
            out_specs=pl.BlockSpec(memory_space=pl.ANY),
            scratch_shapes=[pltpu.VMEM((2, MOE_BLOCK * RPT, LANES), F32), pltpu.VMEM((MOE_BLOCK, D), BF16),
                            pltpu.VMEM((MOE_BLOCK, D), F32), pltpu.VMEM((MOE_BLOCK * RPT, LANES), F32),
                            pltpu.SemaphoreType.DMA((2,)), pltpu.SemaphoreType.DMA(())]),
        out_shape=jax.ShapeDtypeStruct(((2 * T + MOE_BLOCK) * RPT, LANES), F32),
        name="moe_ffn")(block_e, n_used, slot_tab, h, w1, w3, w2)


def _moe_combine_kernel(y0_ref, y1_ref, x_ref, gate_ref, route_ref, fn_ref, o_ref):
    tm = x_ref.shape[0]
    route = route_ref[...]
    rows = lambda ref: jnp.concatenate([ref[pl.ds(c, tm, stride=RPT), :] for c in range(RPT)], axis=1)
    y = route[:, 2:3] * rows(y0_ref) + route[:, 3:4] * rows(y1_ref)
    o_ref[...] = _rms(x_ref[...] + gate_ref[0] * y, fn_ref[...])


def _moe_combine(y, xf, gate, route, fn, S):
    T, D = xf.shape
    tm = TOKEN_TILE
    nps = S // tm
    nt = T // tm
    return pl.pallas_call(
        _moe_combine_kernel, grid=(nt,),
        in_specs=[pl.BlockSpec((tm * RPT, LANES), lambda i: (i, 0)),
                  pl.BlockSpec((tm * RPT, LANES), lambda i: (nt + i, 0)),
                  pl.BlockSpec((tm, D), lambda i: (i, 0)),
                  pl.BlockSpec((1, 1, D), lambda i: (i // nps, 0, 0)),
                  pl.BlockSpec((tm, LANES), lambda i: (i, 0)),
                  pl.BlockSpec((1, D), lambda i: (0, 0))],
        out_specs=pl.BlockSpec((tm, D), lambda i: (i, 0)),
        out_shape=jax.ShapeDtypeStruct((T, D), F32), name="moe_combine")(y, y, xf, gate, route, fn)


def _rope_tables(S):
    inv = ROPE_THETA ** (-jnp.arange(0, ROPE_DIM, 2, dtype=F32) / ROPE_DIM)
    ang = jnp.arange(S, dtype=F32)[:, None] * inv[None, :]
    cos, sin = jnp.cos(ang), jnp.sin(ang)
    pm = np.arange(LANES) % HEAD_DIM
    col = pm % (ROPE_DIM // 2)
    rc = jnp.where((pm < ROPE_DIM)[None, :], cos[:, col], 1.0)
    ra = jnp.where((pm < ROPE_DIM // 2)[None, :], -sin[:, col], 0.0)
    rb = jnp.where(((pm >= ROPE_DIM // 2) & (pm < ROPE_DIM))[None, :], sin[:, col], 0.0)
    return rc, ra, rb


def _dsa_weights(w_in, w_uk, w_iq):
    D = w_in.shape[0]
    a, b, c, d = DSA_Q_LORA, DSA_Q_LORA + DSA_KV_LORA, DSA_Q_LORA + DSA_KV_LORA + ROPE_DIM, \
        DSA_Q_LORA + DSA_KV_LORA + ROPE_DIM + IDX_DIM
    win = jnp.concatenate([w_in[:, :b], w_in[:, c:d], w_in[:, b:c], w_in[:, d:],
                           jnp.zeros((D, 512 - w_in.shape[1]), F32)], axis=1).astype(BF16)
    H = DSA_HEADS
    blk = jnp.zeros((H, HEAD_DIM, 256), F32)
    blk = blk.at[:, ROPE_DIM:, :DSA_KV_LORA].set(jnp.transpose(w_uk, (0, 2, 1)))
    blk = blk.at[:, :ROPE_DIM, 192:192 + ROPE_DIM].set(jnp.eye(ROPE_DIM, dtype=F32))
    z = jnp.zeros((H // 2, HEAD_DIM, 256), F32)
    wcat = jnp.concatenate([jnp.concatenate([blk[0::2], z], axis=2),
                            jnp.concatenate([z, blk[1::2]], axis=2)], axis=1).astype(BF16)
    wiq = w_iq.reshape(DSA_Q_LORA, IDX_HEADS, IDX_DIM)
    wiq = jnp.concatenate([wiq, jnp.zeros_like(wiq)], axis=2).reshape(DSA_Q_LORA, IDX_HEADS * 128)
    return win, wcat, wiq.astype(BF16)


def _nsa_weights(w_in):
    D = w_in.shape[0]
    nq = NSA_HEADS * HEAD_DIM
    wq = w_in[:, :nq].reshape(D, NSA_HEADS, HEAD_DIM)
    z = jnp.zeros_like(wq)
    odd = ((np.arange(NSA_HEADS) // NSA_HPG) % 2 == 1)[None, :, None]
    wq = jnp.concatenate([jnp.where(odd, z, wq), jnp.where(odd, wq, z)], axis=2).reshape(D, NSA_HEADS * 128)
    wkv = w_in[:, nq:nq + 6 * 256]
    wg = jnp.concatenate([w_in[:, nq + 6 * 256:], jnp.zeros((D, LANES - 3 * NSA_HEADS), F32)], axis=1)
    return wq.astype(BF16), wkv.astype(BF16), wg.astype(BF16)


def _nsa_tables(S):
    nch = S // CMP_STRIDE
    nc = (S - CMP_LEN) // CMP_STRIDE + 1
    nb = S // SEL_LEN
    cstart = np.arange(nch) * CMP_STRIDE
    bstart = np.arange(nb) * SEL_LEN
    cov = ((cstart[None, :] < bstart[:, None] + SEL_LEN) & (cstart[None, :] + CMP_LEN > bstart[:, None])
           & (np.arange(nch)[None, :] < nc)).astype(np.float32)
    kpos = np.arange(S).reshape(S // NSA_KC, 1, NSA_KC)
    expand = (kpos // SEL_LEN == np.arange(nb)[None, :, None]).astype(np.float32)
    return jnp.asarray(cov), jnp.asarray(expand, dtype=BF16)


def _slot_tables(dest, T, n_blocks):
    ns = n_blocks * MOE_BLOCK
    asg = jnp.full((ns,), -1, jnp.int32).at[dest].set(jnp.arange(2 * T, dtype=jnp.int32))
    real = asg >= 0
    src = jnp.where(real, asg // 2, 0)
    spare = 2 * T + jnp.arange(ns, dtype=jnp.int32) % MOE_BLOCK
    dst = jnp.where(real, (asg % 2) * T + asg // 2, spare)
    assert T <= 1 << SRC_BITS and 2 * T + MOE_BLOCK <= 1 << (32 - SRC_BITS)
    return src | lax.shift_left(dst, SRC_BITS)


def _chunk_tokens(a, B, S):
    a = a.reshape(B, S // CMP_STRIDE, CMP_STRIDE, NSA_GROUPS, HEAD_DIM)
    return jnp.transpose(a, (0, 3, 1, 2, 4)).reshape(B, NSA_GROUPS, S // CMP_STRIDE, CMP_STRIDE * HEAD_DIM)


def kernel(x, c, norm_mix, norm_ffn, ada_w, ada_b, final_norm, dsa_w_in, dsa_g_q, dsa_w_uq, dsa_g_kv,
           dsa_w_uk, dsa_w_uv, dsa_w_iq, dsa_w_o, ffn_w1, ffn_w3, ffn_w2, nsa_w_in, nsa_cmp_pe,
           nsa_cmp_k1, nsa_cmp_k2, nsa_cmp_v1, nsa_cmp_v2, nsa_w_o, moe_router, moe_w1, moe_w3, moe_w2):
    B, S, D = x.shape
    T = B * S
    xf = x.reshape(T, D)
    mods = _ada(c, ada_w, ada_b).reshape(4, B, 3, 1, D)
    shift = lambda s: mods[s, :, 0]
    scale = lambda s: mods[s, :, 1]
    gate = lambda s: mods[s, :, 2]
    rope = _rope_tables(S)

    win, wcat, wiq = _dsa_weights(dsa_w_in[0], dsa_w_uk[0], dsa_w_iq[0])
    qcat, iq, kcat, ik, iw = _dsa_proj(
        xf, shift(0), scale(0), norm_mix[0:1], win, dsa_g_q[0:1], dsa_g_kv[0:1],
        dsa_w_uq[0].astype(BF16), wiq, wcat, rope, S)
    olat = _dsa_attn(qcat, iq, iw, kcat, ik, B, S)
    wuvo = _wuvo(dsa_w_uv[0], dsa_w_o[0])
    x1, h1 = _post(olat, wuvo, xf, gate(0), norm_ffn[0:1], scale(1), shift(1), S)
    x2 = _ffn(h1, x1, gate(1), ffn_w1[0].astype(BF16), ffn_w3[0].astype(BF16), ffn_w2[0].astype(BF16),
              S, ffn_w1.shape[2] // 2)

    wq, wkv, wg = _nsa_weights(nsa_w_in[0])
    q, kc, vc, ks, vs, kw, vw, gates = _nsa_proj(xf=x2, shift=shift(2), scale=scale(2), gn=norm_mix[1:2],
                                                 wq=wq, wkv=wkv, wg=wg, rope=rope, S=S)
    zpad = jnp.zeros((CMP_HIDDEN, HEAD_DIM), F32)
    pad2 = lambda w2: jnp.stack([jnp.concatenate([w2, zpad], axis=1),
                                 jnp.concatenate([zpad, w2], axis=1)]).astype(BF16)
    kcmp, vcmp = _compress(_chunk_tokens(kc, B, S), _chunk_tokens(vc, B, S),
                           nsa_cmp_pe[0].reshape(1, CMP_LEN * HEAD_DIM),
                           nsa_cmp_k1[0].astype(BF16), pad2(nsa_cmp_k2[0]),
                           nsa_cmp_v1[0].astype(BF16), pad2(nsa_cmp_v2[0]))
    covt, expand = _nsa_tables(S)
    o = _nsa_attn(q, gates, kcmp, vcmp, ks, vs, kw, vw, covt, expand, B, S)
    wr = jnp.concatenate([moe_router[0], jnp.zeros((D, LANES - N_EXPERTS), F32)], axis=1)
    x3, h3, logits = _post(o, nsa_w_o[0].astype(BF16), x2, gate(2), norm_ffn[1:2], scale(3), shift(3), S, wr)
    route, cnt = _router(logits)
    counts = cnt[0, :N_EXPERTS].astype(jnp.int32)
    padded = (counts + MOE_BLOCK - 1) // MOE_BLOCK * MOE_BLOCK
    ends = jnp.cumsum(padded)
    pstart = ends - padded
    eidx = route[:, 0:2].astype(jnp.int32)
    dest = (pstart[eidx] + route[:, 4:6].astype(jnp.int32)).reshape(-1)
    n_blocks = -(-(T * 2) // MOE_BLOCK) + N_EXPERTS
    block_start = jnp.arange(n_blocks, dtype=jnp.int32) * MOE_BLOCK
    block_e = jnp.minimum(jnp.sum((ends[None, :] <= block_start[:, None]).astype(jnp.int32), axis=1),
                          N_EXPERTS - 1)
    n_used = (ends[-1:] // MOE_BLOCK).astype(jnp.int32)
    y = _moe_ffn(block_e, n_used, _slot_tables(dest, T, n_blocks), h3, moe_w1[0].astype(BF16), moe_w3[0].astype(BF16),
                 moe_w2[0].astype(BF16), 512)
    out = _moe_combine(y, x3, gate(3), route, final_norm.reshape(1, D), S)
    return out.reshape(B, S, D)
```
